```python
import math
import jax
import jax.numpy as jnp
from jax import lax
import numpy as np

D_MODEL = 4096
BATCH = 1
SEQ = 8192
DEPTH = 2

GRID_W = 64
CTX_LEN = 256

HEAD_DIM = 128
ATTN_WIDTH = D_MODEL // 2
ATTN_HEADS = ATTN_WIDTH // HEAD_DIM
ATTN_KV_HEADS = 4
GQA_GROUP = ATTN_HEADS // ATTN_KV_HEADS
KV_WIDTH = ATTN_KV_HEADS * HEAD_DIM
Q_BLOCK = 128
ROPE_THETA = 10000.0
ATTN_SCALE = HEAD_DIM ** -0.5

RWKV_HEAD_DIM = 64
RWKV_WIDTH = D_MODEL // 4
RWKV_HEADS = RWKV_WIDTH // RWKV_HEAD_DIM
W_RANK = 64
A_RANK = 64
G_RANK = 128
RWKV_COLS = 3 * RWKV_WIDTH + 2 * W_RANK + 2 * A_RANK + G_RANK
W_DECAY_SCALE = math.exp(-0.5)
GN_EPS = 64e-5

FOURIER_WIDTH = D_MODEL // 4
FOURIER_GROUPS = 4
FOURIER_GROUP_DIM = FOURIER_WIDTH // FOURIER_GROUPS

D_MIX = ATTN_WIDTH + RWKV_WIDTH + FOURIER_WIDTH
IN_COLS = ATTN_WIDTH + 2 * KV_WIDTH + RWKV_COLS + FOURIER_WIDTH

N_GROUPS = 4
EXPERTS_PER_GROUP = 4
N_EXPERTS = N_GROUPS * EXPERTS_PER_GROUP
TOP_K = 2
D_EXPERT = D_MODEL // 4

DEEPNORM_ALPHA = (2 * DEPTH) ** 0.25
DEEPNORM_BETA = (8 * DEPTH) ** -0.25

kernel_name = 'hybrid_attn_rwkv7_fnet_hmoe_dit'


def layer_norm(x, gain=None, bias=None, eps=1e-6):
    xf = x.astype(jnp.float32)
    mu = jnp.mean(xf, -1, keepdims=True)
    var = jnp.mean(jnp.square(xf - mu), -1, keepdims=True)
    y = (xf - mu) * lax.rsqrt(var + eps)
    if gain is not None:
        y = y * gain.astype(jnp.float32) + bias.astype(jnp.float32)
    return y.astype(x.dtype)


def rms_norm(x, gain, eps=1e-6):
    xf = x.astype(jnp.float32)
    y = xf * lax.rsqrt(jnp.mean(xf * xf, -1, keepdims=True) + eps) * gain.astype(jnp.float32)
    return y.astype(x.dtype)


def modulate(x, shift, scale):
    return layer_norm(x) * (1.0 + scale) + shift


def split_cols(z, sizes):
    out = []
    start = 0
    for s in sizes:
        out.append(z[..., start:start + s])
        start += s
    return out


def axial_rope_tables(n_rows):
    row = jnp.repeat(jnp.arange(n_rows), GRID_W).astype(jnp.float32)
    col = jnp.tile(jnp.arange(GRID_W), n_rows).astype(jnp.float32)
    axis_dim = HEAD_DIM // 2
    inv_freq = ROPE_THETA ** (-jnp.arange(0, axis_dim, 2, dtype=jnp.float32) / axis_dim)
    ang = jnp.concatenate([row[:, None] * inv_freq, col[:, None] * inv_freq], -1)
    return jnp.cos(ang), jnp.sin(ang)


def apply_rope(t, cos, sin):
    half = t.shape[-1] // 2
    tf = t.astype(jnp.float32)
    t1, t2 = tf[..., :half], tf[..., half:]
    return jnp.concatenate([t1 * cos - t2 * sin, t1 * sin + t2 * cos], -1).astype(t.dtype)


def to_heads(t, n_heads):
    B, T, _ = t.shape
    return t.reshape(B, T, n_heads, HEAD_DIM).transpose(0, 2, 1, 3)


def merge_heads(t):
    B, H, T, hd = t.shape
    return t.transpose(0, 2, 1, 3).reshape(B, T, H * hd)


def gqa_softmax(q, keys, vals):
    s = jnp.einsum('bkgqd,bkld->bkgql', q, keys).astype(jnp.float32) * ATTN_SCALE
    p = jax.nn.softmax(s, axis=-1).astype(vals.dtype)
    return jnp.einsum('bkgql,bkld->bkgqd', p, vals)


def latent_attention(q, k, v, k_ctx, v_ctx):
    B, H, T, hd = q.shape
    keys = jnp.concatenate([k, k_ctx], axis=2)
    vals = jnp.concatenate([v, v_ctx], axis=2)
    nb = T // Q_BLOCK
    qb = q.reshape(B, ATTN_KV_HEADS, GQA_GROUP, nb, Q_BLOCK, hd).transpose(3, 0, 1, 2, 4, 5)
    out = lax.map(lambda blk: gqa_softmax(blk, keys, vals), qb)
    return out.transpose(1, 2, 3, 0, 4, 5).reshape(B, H, T, hd)


def context_attention(q, k, v):
    B, H, C, hd = q.shape
    out = gqa_softmax(q.reshape(B, ATTN_KV_HEADS, GQA_GROUP, C, hd), k, v)
    return out.reshape(B, H, C, hd)


def centred_shift(z, mu_prev, mu_next):
    zero = jnp.zeros_like(z[:, :1])
    prev = jnp.concatenate([zero, z[:, :-1]], axis=1)
    nxt = jnp.concatenate([z[:, 1:], zero], axis=1)
    return z + mu_prev * (prev - z) + mu_next * (nxt - z)


def rwkv_features(z, mu, w0, w_up, a0, a_up, g_up, k_k, k_a):
    B, T, _ = z.shape
    z = centred_shift(z, mu[0], mu[1])
    r, k, v, wd_f, wd_b, ad_f, ad_b, g_d = split_cols(
        z, (RWKV_WIDTH, RWKV_WIDTH, RWKV_WIDTH, W_RANK, W_RANK, A_RANK, A_RANK, G_RANK))
    heads = lambda t: t.reshape(B, T, RWKV_HEADS, RWKV_HEAD_DIM).astype(jnp.float32)
    kappa = heads(k * k_k)
    kappa = kappa * lax.rsqrt(jnp.maximum(jnp.sum(kappa * kappa, -1, keepdims=True), 1e-24))
    g = jax.nn.sigmoid(g_d) @ g_up
    kf = k.astype(jnp.float32)
    per_dir = []
    for d, (wd, ad) in enumerate(((wd_f, ad_f), (wd_b, ad_b))):
        w = jnp.exp(-W_DECAY_SCALE * jax.nn.sigmoid((w0[d] + jnp.tanh(wd) @ w_up[d]).astype(jnp.float32)))
        a = jax.nn.sigmoid((a0[d] + ad @ a_up[d]).astype(jnp.float32))
        kt = kf * (1.0 + (a - 1.0) * k_a.astype(jnp.float32))
        per_dir.append((heads(w), heads(a), heads(kt)))
    return heads(r), heads(v), kappa, g, per_dir


def rwkv_scan(state0, r, w, kt, kappa, a, v, reverse, emit_y):
    xs = tuple(jnp.swapaxes(t, 0, 1) for t in (r, w, kt, kappa, a, v))

    def step(s, inp):
        r_t, w_t, k_t, kap_t, a_t, v_t = inp
        s_kap = jnp.einsum('bhvk,bhk->bhv', s, kap_t)
        s = (s * w_t[:, :, None, :]
             - s_kap[..., None] * (a_t * kap_t)[:, :, None, :]
             + v_t[..., None] * k_t[:, :, None, :])
        y = jnp.einsum('bhvk,bhk->bhv', s, r_t) if emit_y else None
        return s, y

    s_final, ys = lax.scan(step, state0, xs, reverse=reverse)
    return s_final, (jnp.swapaxes(ys, 0, 1) if emit_y else None)


def rwkv_output(y_f, y_b, r, v, kt_f, kt_b, g, r_k, lnx_gain, lnx_bias, dtype):
    B, T = r.shape[:2]
    y = layer_norm(y_f + y_b, lnx_gain.reshape(RWKV_HEADS, RWKV_HEAD_DIM),
                   lnx_bias.reshape(RWKV_HEADS, RWKV_HEAD_DIM), eps=GN_EPS)
    rk = r * r_k.astype(jnp.float32)
    bonus = (jnp.sum(rk * kt_f, -1, keepdims=True) + jnp.sum(rk * kt_b, -1, keepdims=True)) * v
    return (y + bonus).reshape(B, T, RWKV_WIDTH).astype(dtype) * g


def rwkv_mixer(zr, zrc, mu, w0, w_up, a0, a_up, g_up, k_k, k_a, r_k, lnx_gain, lnx_bias, ctx_out):
    rl, vl, kapl, gl, dl = rwkv_features(zr, mu, w0, w_up, a0, a_up, g_up, k_k, k_a)
    rc, vc, kapc, gc, dc = rwkv_features(zrc, mu, w0, w_up, a0, a_up, g_up, k_k, k_a)
    B = zr.shape[0]
    s0 = jnp.zeros((B, RWKV_HEADS, RWKV_HEAD_DIM, RWKV_HEAD_DIM), jnp.float32)
    y_lat, y_ctx = [], []
    for d, reverse in enumerate((False, True)):
        wc, ac, ktc = dc[d]
        s_ctx, yc = rwkv_scan(s0, rc, wc, ktc, kapc, ac, vc, reverse, ctx_out)
        wl, al, ktl = dl[d]
        _, yl = rwkv_scan(s_ctx, rl, wl, ktl, kapl, al, vl, reverse, True)
        y_lat.append(yl)
        y_ctx.append(yc)
    out_l = rwkv_output(y_lat[0], y_lat[1], rl, vl, dl[0][2], dl[1][2], gl, r_k, lnx_gain, lnx_bias, zr.dtype)
    if not ctx_out:
        return out_l, None
    out_c = rwkv_output(y_ctx[0], y_ctx[1], rc, vc, dc[0][2], dc[1][2], gc, r_k, lnx_gain, lnx_bias, zrc.dtype)
    return out_l, out_c


def fourier_mix(z):
    B, T, _ = z.shape
    zg = z.reshape(B, T, FOURIER_GROUPS, FOURIER_GROUP_DIM).astype(jnp.float32)
    f = jnp.fft.fft2(zg, axes=(1, 3), norm='ortho').real
    return f.reshape(B, T, FOURIER_WIDTH).astype(z.dtype)


def token_mixer(u, uc, w_in, q_gain, k_gain, rwkv_mu, w0, w_up, a0, a_up, g_up, k_k, k_a, r_k,
                lnx_gain, lnx_bias, w_out, cos, sin, ctx_out):
    sizes = (ATTN_WIDTH, KV_WIDTH, KV_WIDTH, RWKV_COLS, FOURIER_WIDTH)
    q, k, v, zr, zf = split_cols(u @ w_in, sizes)
    qc, kc, vc, zrc, zfc = split_cols(uc @ w_in, sizes)
    q = apply_rope(rms_norm(to_heads(q, ATTN_HEADS), q_gain), cos, sin)
    k = apply_rope(rms_norm(to_heads(k, ATTN_KV_HEADS), k_gain), cos, sin)
    v = to_heads(v, ATTN_KV_HEADS)
    kc = rms_norm(to_heads(kc, ATTN_KV_HEADS), k_gain)
    vc = to_heads(vc, ATTN_KV_HEADS)
    attn = merge_heads(latent_attention(q, k, v, kc, vc))
    rwkv, rwkv_c = rwkv_mixer(zr, zrc, rwkv_mu, w0, w_up, a0, a_up, g_up, k_k, k_a, r_k,
                              lnx_gain, lnx_bias, ctx_out)
    m = jnp.concatenate([attn, rwkv, fourier_mix(zf)], -1) @ w_out
    if not ctx_out:
        return m, None
    qc = rms_norm(to_heads(qc, ATTN_HEADS), q_gain)
    attn_c = merge_heads(context_attention(qc, kc, vc))
    mc = jnp.concatenate([attn_c, rwkv_c, fourier_mix(zfc)], -1) @ w_out
    return m, mc


def hier_moe(h, wg, bg, we, be, w1, w3, w2):
    B, T, D = h.shape
    hf = h.reshape(B * T, D)
    lg = (hf @ wg + bg).astype(jnp.float32)
    pg = jax.nn.softmax(lg, axis=-1)
    gsel = jnp.argmax(lg, axis=-1)
    p_group = jnp.take_along_axis(pg, gsel[:, None], axis=-1)
    le = (hf @ we + be).astype(jnp.float32).reshape(-1, N_GROUPS, EXPERTS_PER_GROUP)
    le = jnp.take_along_axis(le, gsel[:, None, None], axis=1)[:, 0]
    pe = jax.nn.softmax(le, axis=-1)
    top_v, top_i = lax.top_k(pe, TOP_K)
    top_v = top_v / jnp.sum(top_v, -1, keepdims=True)
    eid = gsel[:, None] * EXPERTS_PER_GROUP + top_i
    gates = p_group * jnp.sum(top_v[..., None] * jax.nn.one_hot(eid, N_EXPERTS, dtype=jnp.float32), axis=1)
    gates = gates.astype(h.dtype)
    y = jnp.zeros_like(hf)
    for e in range(N_EXPERTS):
        he = jax.nn.silu(hf @ w1[e]) * (hf @ w3[e])
        y = y + gates[:, e:e + 1] * (he @ w2[e])
    return y.reshape(B, T, D)


def setup_inputs(seed: int = 0) -> dict:
    key = jax.random.key(seed)
    ks = iter(jax.random.split(key, 40))
    f32 = jnp.float32
    D = D_MODEL
    L = DEPTH

    def nrm(shape, scale):
        return jax.random.normal(next(ks), shape, f32) * scale

    return {
        'x': nrm((BATCH, SEQ, D), 1.0),
        'c': nrm((BATCH, D), 1.0),
        'ctx': nrm((BATCH, CTX_LEN, D), 1.0),
        'c_ctx': nrm((D,), 1.0),
        'w_mod': nrm((L, D, 6 * D), 0.5 * D ** -0.5),
        'b_mod': nrm((L, 6 * D), 0.01),
        'w_in': nrm((L, D, IN_COLS), D ** -0.5),
        'q_gain': 1.0 + nrm((L, HEAD_DIM), 0.02),
        'k_gain': 1.0 + nrm((L, HEAD_DIM), 0.02),
        'rwkv_mu': jax.random.uniform(next(ks), (L, 2, RWKV_COLS), f32, 0.0, 0.5),
        'w0': nrm((L, 2, RWKV_WIDTH), 0.5),
        'w_up': nrm((L, 2, W_RANK, RWKV_WIDTH), 0.5 * W_RANK ** -0.5),
        'a0': nrm((L, 2, RWKV_WIDTH), 0.5),
        'a_up': nrm((L, 2, A_RANK, RWKV_WIDTH), 0.5 * A_RANK ** -0.5),
        'g_up': nrm((L, G_RANK, RWKV_WIDTH), G_RANK ** -0.5),
        'k_k': 0.85 + nrm((L, RWKV_WIDTH), 0.05),
        'k_a': 1.0 + nrm((L, RWKV_WIDTH), 0.05),
        'r_k': nrm((L, RWKV_HEADS, RWKV_HEAD_DIM), 0.1),
        'lnx_gain': 1.0 + nrm((L, RWKV_WIDTH), 0.02),
        'lnx_bias': nrm((L, RWKV_WIDTH), 0.01),
        'w_out': nrm((L, D_MIX, D), DEEPNORM_BETA * D_MIX ** -0.5),
        'ln1_gain': 1.0 + nrm((L, D), 0.02),
        'ln1_bias': nrm((L, D), 0.01),
        'ln2_gain': 1.0 + nrm((L, D), 0.02),
        'ln2_bias': nrm((L, D), 0.01),
        'router_group_w': nrm((L, D, N_GROUPS), D ** -0.5),
        'router_group_b': nrm((L, N_GROUPS), 0.01),
        'router_expert_w': nrm((L, D, N_EXPERTS), D ** -0.5),
        'router_expert_b': nrm((L, N_EXPERTS), 0.01),
        'w1': nrm((L, N_EXPERTS, D, D_EXPERT), D ** -0.5),
        'w3': nrm((L, N_EXPERTS, D, D_EXPERT), D ** -0.5),
        'w2': nrm((L, N_EXPERTS, D_EXPERT, D), DEEPNORM_BETA * D_EXPERT ** -0.5),
    }


def reference(x, c, ctx, c_ctx, w_mod, b_mod, w_in, q_gain, k_gain, rwkv_mu, w0, w_up, a0, a_up, g_up,
              k_k, k_a, r_k, lnx_gain, lnx_bias, w_out, ln1_gain, ln1_bias, ln2_gain, ln2_bias,
              router_group_w, router_group_b, router_expert_w, router_expert_b, w1, w3, w2):
    B, T, D = x.shape
    C = ctx.shape[1]
    n_rows = T // GRID_W
    cos, sin = axial_rope_tables(n_rows)
    c_act = jax.nn.silu(c)
    cc_act = jax.nn.silu(c_ctx)
    for l in range(DEPTH):
        last = l == DEPTH - 1
        mod = (c_act @ w_mod[l] + b_mod[l]).reshape(B, 6, 1, D)
        modc = (cc_act @ w_mod[l] + b_mod[l]).reshape(6, D)
        sh1, sc1, g1, sh2, sc2, g2 = [mod[:, i] for i in range(6)]
        sh1c, sc1c, g1c, sh2c, sc2c, g2c = [modc[i] for i in range(6)]

        u = modulate(x, sh1, sc1)
        uc = modulate(ctx, sh1c, sc1c)
        m, mc = token_mixer(u, uc, w_in[l], q_gain[l], k_gain[l], rwkv_mu[l], w0[l], w_up[l], a0[l], a_up[l],
                            g_up[l], k_k[l], k_a[l], r_k[l], lnx_gain[l], lnx_bias[l], w_out[l], cos, sin,
                            not last)
        x = layer_norm(DEEPNORM_ALPHA * x + g1 * m, ln1_gain[l], ln1_bias[l])
        if not last:
            ctx = layer_norm(DEEPNORM_ALPHA * ctx + g1c * mc, ln1_gain[l], ln1_bias[l])

        u = modulate(x, sh2, sc2)
        if last:
            f = hier_moe(u, router_group_w[l], router_group_b[l], router_expert_w[l], router_expert_b[l],
                         w1[l], w3[l], w2[l])
        else:
            uc = modulate(ctx, sh2c, sc2c)
            f_all = hier_moe(jnp.concatenate([uc, u], axis=1), router_group_w[l], router_group_b[l],
                             router_expert_w[l], router_expert_b[l], w1[l], w3[l], w2[l])
            fc, f = f_all[:, :C], f_all[:, C:]
            ctx = layer_norm(DEEPNORM_ALPHA * ctx + g2c * fc, ln2_gain[l], ln2_bias[l])
        x = layer_norm(DEEPNORM_ALPHA * x + g2 * f, ln2_gain[l], ln2_bias[l])
    return x
```

```python
import functools
import math

import numpy as np
import jax
import jax.numpy as jnp
from jax import lax
from jax.experimental import pallas as pl
from jax.experimental.pallas import tpu as pltpu

F32 = jnp.float32
BF16 = jnp.bfloat16
HI = lax.Precision.HIGHEST

GRID_W = 64
HEAD_DIM = 128
KV_HEADS = 4
RWKV_HEAD = 64
W_RANK = 64
A_RANK = 64
G_RANK = 128
FOURIER_GROUPS = 4
N_GROUPS = 4
EXPERTS_PER_GROUP = 4
N_EXPERTS = N_GROUPS * EXPERTS_PER_GROUP
ROPE_THETA = 10000.0
W_DECAY_SCALE = math.exp(-0.5)
GN_EPS = 64e-5
LN_EPS = 1e-6
LANES = 128
CHUNK = 64

V7X_VMEM_LIMIT_MB = 56


def _params(sem, vmem_mb=V7X_VMEM_LIMIT_MB):
    return pltpu.CompilerParams(dimension_semantics=sem, vmem_limit_bytes=vmem_mb * 1024 * 1024)


def _dot(a, b, prec=None):
    return jnp.dot(a, b, preferred_element_type=F32, precision=prec)


def _dot_nt(a, b, prec=None):
    return lax.dot_general(a, b, (((1,), (1,)), ((), ())), preferred_element_type=F32, precision=prec)


def _dot_tn(a, b, prec=None):
    return lax.dot_general(a, b, (((0,), (0,)), ((), ())), preferred_element_type=F32, precision=prec)


def _sigmoid(x):
    return 1.0 / (1.0 + jnp.exp(-x))


def _mod_kernel(a_ref, w_ref, b_ref, o_ref):
    tn = o_ref.shape[2]
    rows = []
    for r in range(2):
        cols = []
        for j in range(tn // LANES):
            prod = w_ref[0, :, j * LANES:(j + 1) * LANES] * a_ref[r]
            cols.append(jnp.sum(prod, axis=0, keepdims=True))
        rows.append(jnp.concatenate(cols, axis=1) + b_ref[0])
    rows.append(jnp.zeros((6, tn), F32))
    o_ref[0] = jnp.concatenate(rows, axis=0)


def mod_vectors(c, c_ctx, w_mod, b_mod):
    L, D, D6 = w_mod.shape
    acts = jnp.stack([jax.nn.silu(c[0]), jax.nn.silu(c_ctx)], 0)
    a_b = jnp.broadcast_to(acts[:, :, None], (2, D, LANES))
    tn = 512
    return pl.pallas_call(
        _mod_kernel,
        grid=(L, D6 // tn),
        in_specs=[pl.BlockSpec((2, D, LANES), lambda l, j: (0, 0, 0)),
                  pl.BlockSpec((1, D, tn), lambda l, j: (l, 0, j)),
                  pl.BlockSpec((1, 1, tn), lambda l, j: (l, 0, j))],
        out_specs=pl.BlockSpec((1, 8, tn), lambda l, j: (l, 0, j)),
        out_shape=jax.ShapeDtypeStruct((L, 8, D6), F32),
        compiler_params=_params(("parallel", "parallel")),
        name="mod_vectors",
    )(a_b, w_mod, b_mod.reshape(L, 1, D6))


def _ln_rows(x, eps):
    mu = jnp.mean(x, axis=-1, keepdims=True)
    xc = x - mu
    var = jnp.mean(xc * xc, axis=-1, keepdims=True)
    return xc * lax.rsqrt(var + eps)


def _ln_mod_kernel(n_ctx_tiles, with_router, x_ref, ss_ref, *rest):
    if with_router:
        wr_ref, br_ref, u_ref, lg_ref = rest
    else:
        (u_ref,) = rest
    is_ctx = pl.program_id(0) < n_ctx_tiles
    sh = jnp.where(is_ctx, ss_ref[2:3, :], ss_ref[0:1, :])
    sc = jnp.where(is_ctx, ss_ref[3:4, :], ss_ref[1:2, :])
    u = _ln_rows(x_ref[...], LN_EPS) * (1.0 + sc) + sh
    u_ref[...] = u.astype(u_ref.dtype)
    if with_router:
        lg_ref[...] = _dot(u, wr_ref[...], HI) + br_ref[...]


def ln_modulate(xx, ss, n_ctx, router=None, tm=256):
    N, D = xx.shape
    assert N % tm == 0 and n_ctx % tm == 0
    in_specs = [pl.BlockSpec((tm, D), lambda i: (i, 0)), pl.BlockSpec((4, D), lambda i: (0, 0))]
    out_specs = [pl.BlockSpec((tm, D), lambda i: (i, 0))]
    out_shape = [jax.ShapeDtypeStruct((N, D), BF16)]
    args = [xx, ss]
    if router is not None:
        in_specs += [pl.BlockSpec((D, LANES), lambda i: (0, 0)), pl.BlockSpec((1, LANES), lambda i: (0, 0))]
        out_specs.append(pl.BlockSpec((tm, LANES), lambda i: (i, 0)))
        out_shape.append(jax.ShapeDtypeStruct((N, LANES), F32))
        args += list(router)
    res = pl.pallas_call(
        functools.partial(_ln_mod_kernel, n_ctx // tm, router is not None),
        grid=(N // tm,), in_specs=in_specs, out_specs=out_specs, out_shape=out_shape,
        compiler_params=_params(("parallel",)), name="ln_modulate",
    )(*args)
    return res if router is not None else res[0]


def _resid_ln_kernel(alpha, n_ctx_tiles, x_ref, m_ref, g_ref, gb_ref, o_ref):
    is_ctx = pl.program_id(0) < n_ctx_tiles
    g = jnp.where(is_ctx, g_ref[1:2, :], g_ref[0:1, :])
    y = alpha * x_ref[...] + g * m_ref[...].astype(F32)
    o_ref[...] = _ln_rows(y, LN_EPS) * gb_ref[0:1, :] + gb_ref[1:2, :]


def resid_ln(xx, m, gates2, gain_bias, n_ctx, alpha, tm=256):
    N, D = xx.shape
    return pl.pallas_call(
        functools.partial(_resid_ln_kernel, alpha, n_ctx // tm),
        grid=(N // tm,),
        in_specs=[pl.BlockSpec((tm, D), lambda i: (i, 0)), pl.BlockSpec((tm, D), lambda i: (i, 0)),
                  pl.BlockSpec((2, D), lambda i: (0, 0)), pl.BlockSpec((2, D), lambda i: (0, 0))],
        out_specs=pl.BlockSpec((tm, D), lambda i: (i, 0)),
        out_shape=jax.ShapeDtypeStruct((N, D), F32),
        compiler_params=_params(("parallel",)), name="resid_ln",
    )(xx, m, gates2, gain_bias)


def _mm_kernel(a_ref, b_ref, o_ref):
    o_ref[...] = _dot(a_ref[...], b_ref[...]).astype(o_ref.dtype)


def matmul(a, b, out_dtype, tm, tn):
    M, K = a.shape
    _, Nn = b.shape
    assert M % tm == 0 and Nn % tn == 0
    return pl.pallas_call(
        _mm_kernel, grid=(M // tm, Nn // tn),
        in_specs=[pl.BlockSpec((tm, K), lambda i, j: (i, 0)), pl.BlockSpec((K, tn), lambda i, j: (0, j))],
        out_specs=pl.BlockSpec((tm, tn), lambda i, j: (i, j)),
        out_shape=jax.ShapeDtypeStruct((M, Nn), out_dtype),
        compiler_params=_params(("parallel", "arbitrary")), name="matmul",
    )(a, b)


def _qkv_prep_kernel(n_q_heads, scale, z_ref, cos_ref, sin_ref, qg_ref, kg_ref, q_ref, k_ref, v_ref):
    cosf = cos_ref[...]
    sinf = sin_ref[...]

    def norm_rope(t, gain):
        t = t * lax.rsqrt(jnp.mean(t * t, axis=-1, keepdims=True) + LN_EPS) * gain
        return t * cosf + pltpu.roll(t, HEAD_DIM // 2, 1) * sinf

    for h in range(n_q_heads):
        sl = slice(h * HEAD_DIM, (h + 1) * HEAD_DIM)
        q_ref[:, sl] = (norm_rope(z_ref[:, sl], qg_ref[...]) * scale).astype(q_ref.dtype)
    qw = n_q_heads * HEAD_DIM
    for h in range(KV_HEADS):
        sl = slice(h * HEAD_DIM, (h + 1) * HEAD_DIM)
        zs = slice(qw + h * HEAD_DIM, qw + (h + 1) * HEAD_DIM)
        k_ref[:, sl] = norm_rope(z_ref[:, zs], kg_ref[...]).astype(k_ref.dtype)
    kvw = KV_HEADS * HEAD_DIM
    v_ref[...] = z_ref[:, qw + kvw:qw + 2 * kvw].astype(v_ref.dtype)


def qkv_prep(z, cosf, sinf, q_gain, k_gain, attn_width, tm=256):
    N = z.shape[0]
    n_q = attn_width // HEAD_DIM
    kvw = KV_HEADS * HEAD_DIM
    zw = attn_width + 2 * kvw
    return pl.pallas_call(
        functools.partial(_qkv_prep_kernel, n_q, HEAD_DIM ** -0.5),
        grid=(N // tm,),
        in_specs=[pl.BlockSpec((tm, zw), lambda i: (i, 0)),
                  pl.BlockSpec((tm, HEAD_DIM), lambda i: (i, 0)), pl.BlockSpec((tm, HEAD_DIM), lambda i: (i, 0)),
                  pl.BlockSpec((1, HEAD_DIM), lambda i: (0, 0)), pl.BlockSpec((1, HEAD_DIM), lambda i: (0, 0))],
        out_specs=[pl.BlockSpec((tm, attn_width), lambda i: (i, 0)),
                   pl.BlockSpec((tm, kvw), lambda i: (i, 0)), pl.BlockSpec((tm, kvw), lambda i: (i, 0))],
        out_shape=[jax.ShapeDtypeStruct((N, attn_width), BF16),
                   jax.ShapeDtypeStruct((N, kvw), BF16), jax.ShapeDtypeStruct((N, kvw), BF16)],
        compiler_params=_params(("parallel",)), name="qkv_prep",
    )(z, cosf, sinf, q_gain.reshape(1, HEAD_DIM), k_gain.reshape(1, HEAD_DIM))


def _flash_kernel(n_ctx, n_ctx_tiles, tk, group, q_ref, k_ref, v_ref, o_ref, m_ref, l_ref, acc_ref):
    tq = q_ref.shape[0]
    n_all = k_ref.shape[0]
    qs = jnp.concatenate([q_ref[:, h * HEAD_DIM:(h + 1) * HEAD_DIM] for h in range(group)], axis=0)
    m_ref[...] = jnp.full(m_ref.shape, -jnp.inf, F32)
    l_ref[...] = jnp.zeros(l_ref.shape, F32)
    acc_ref[...] = jnp.zeros(acc_ref.shape, F32)

    def step(kc, vc):
        s = _dot_nt(qs, kc)
        m_old = m_ref[...]
        m_new = jnp.maximum(m_old, jnp.max(s, axis=-1, keepdims=True))
        alpha = jnp.exp(m_old - m_new)
        p = jnp.exp(s - m_new)
        l_ref[...] = alpha * l_ref[...] + jnp.sum(p, axis=-1, keepdims=True)
        acc_ref[...] = alpha * acc_ref[...] + _dot(p.astype(vc.dtype), vc)
        m_ref[...] = m_new

    is_ctx = pl.program_id(1) < n_ctx_tiles

    @pl.when(is_ctx)
    def _():
        step(k_ref[0:n_ctx, :], v_ref[0:n_ctx, :])

    @pl.when(jnp.logical_not(is_ctx))
    def _():
        def body(c, carry):
            off = pl.multiple_of(c * tk, tk)
            step(k_ref[pl.ds(off, tk), :], v_ref[pl.ds(off, tk), :])
            return carry
        lax.fori_loop(0, n_all // tk, body, 0)

    out = acc_ref[...] / l_ref[...]
    for h in range(group):
        o_ref[:, h * HEAD_DIM:(h + 1) * HEAD_DIM] = out[h * tq:(h + 1) * tq].astype(o_ref.dtype)


def flash_attention(q, k, v, n_ctx, tq=256, tk=768):
    N, aw = q.shape
    group = aw // HEAD_DIM // KV_HEADS
    gw = group * HEAD_DIM
    assert N % tq == 0 and n_ctx % tq == 0 and N % tk == 0
    return pl.pallas_call(
        functools.partial(_flash_kernel, n_ctx, n_ctx // tq, tk, group),
        grid=(KV_HEADS, N // tq),
        in_specs=[pl.BlockSpec((tq, gw), lambda g, i: (i, g)),
                  pl.BlockSpec((N, HEAD_DIM), lambda g, i: (0, g)),
                  pl.BlockSpec((N, HEAD_DIM), lambda g, i: (0, g))],
        out_specs=pl.BlockSpec((tq, gw), lambda g, i: (i, g)),
        out_shape=jax.ShapeDtypeStruct((N, aw), BF16),
        scratch_shapes=[pltpu.VMEM((group * tq, 1), F32), pltpu.VMEM((group * tq, 1), F32),
                        pltpu.VMEM((group * tq, HEAD_DIM), F32)],
        compiler_params=_params(("parallel", "arbitrary")), name="flash_attention",
    )(q, k, v)


def rope_tables(n_ctx, n_lat):
    n_rows = n_lat // GRID_W
    row = jnp.repeat(jnp.arange(n_rows), GRID_W).astype(F32)
    col = jnp.tile(jnp.arange(GRID_W), n_rows).astype(F32)
    axis_dim = HEAD_DIM // 2
    inv_freq = ROPE_THETA ** (-jnp.arange(0, axis_dim, 2, dtype=F32) / axis_dim)
    ang = jnp.concatenate([row[:, None] * inv_freq, col[:, None] * inv_freq], -1)
    cos, sin = jnp.cos(ang), jnp.sin(ang)
    cosf = jnp.concatenate([cos, cos], -1)
    sinf = jnp.concatenate([-sin, sin], -1)
    cosf = jnp.concatenate([jnp.ones((n_ctx, HEAD_DIM), F32), cosf], 0)
    sinf = jnp.concatenate([jnp.zeros((n_ctx, HEAD_DIM), F32), sinf], 0)
    return cosf, sinf


def _seg_sum64(x, e128):
    return _dot(x, e128, HI)


def _rwkv_feat_kernel(n_ctx_tiles, n_tiles, w_cols,
                      z_ref, zp_ref, zn_ref, mu_ref, kk_ref, ka_ref, w0_ref, a0_ref,
                      wup_ref, aup_ref, gup_ref, e_ref,
                      r_ref, v_ref, kap_ref, g_ref, lwf_ref, bf_ref, ktf_ref, lwb_ref, bb_ref, ktb_ref,
                      scr_ref):
    i = pl.program_id(0)
    tm = z_ref.shape[0]
    first = jnp.logical_or(i == 0, i == n_ctx_tiles)
    last = jnp.logical_or(i == n_ctx_tiles - 1, i == n_tiles - 1)
    scr_ref[8:8 + tm, :] = z_ref[...]
    scr_ref[0:8, :] = jnp.where(first, 0.0, zp_ref[...])
    scr_ref[8 + tm:16 + tm, :] = jnp.where(last, 0.0, zn_ref[...])
    z = z_ref[...]
    prev = scr_ref[7:7 + tm, :]
    nxt = scr_ref[9:9 + tm, :]
    zs = z + mu_ref[0:1, :] * (prev - z) + mu_ref[1:2, :] * (nxt - z)

    r = zs[:, 0:w_cols]
    k = zs[:, w_cols:2 * w_cols]
    v = zs[:, 2 * w_cols:3 * w_cols]
    lora = zs[:, 3 * w_cols:]
    wd = jnp.tanh(lora[:, 0:LANES])
    ad = lora[:, LANES:2 * LANES]
    gd = _sigmoid(lora[:, 2 * LANES:3 * LANES])
    r_ref[...] = r
    v_ref[...] = v
    g_ref[...] = _dot(gd, gup_ref[...], HI)
    kk = k * kk_ref[...]
    e128 = e_ref[...]
    kap = jnp.concatenate(
        [kk[:, s:s + LANES] * lax.rsqrt(jnp.maximum(_seg_sum64(kk[:, s:s + LANES] * kk[:, s:s + LANES], e128), 1e-24))
         for s in range(0, w_cols, LANES)], axis=1)
    kap_ref[...] = kap
    outs = ((lwf_ref, bf_ref, ktf_ref), (lwb_ref, bb_ref, ktb_ref))
    for d in range(2):
        lw_ref, b_ref, kt_ref = outs[d]
        lw_ref[...] = -W_DECAY_SCALE * _sigmoid(w0_ref[d:d + 1, :] + _dot(wd, wup_ref[d], HI))
        a = _sigmoid(a0_ref[d:d + 1, :] + _dot(ad, aup_ref[d], HI))
        kt_ref[...] = k * (1.0 + (a - 1.0) * ka_ref[...])
        b_ref[...] = a * kap


def _seg_ones(width=LANES, seg=RWKV_HEAD):
    i = np.arange(width)
    return jnp.asarray((i[:, None] // seg == i[None, :] // seg).astype(np.float32))


def rwkv_features(zr, n_ctx, mu, w0, w_up, a0, a_up, g_up, k_k, k_a, tm=256):
    N, zw = zr.shape
    wc = k_k.shape[0]
    assert zw == 3 * wc + 3 * LANES and W_RANK + W_RANK == LANES and A_RANK + A_RANK == LANES and G_RANK == LANES
    zeros = jnp.zeros((W_RANK, wc), F32)
    wup = jnp.stack([jnp.concatenate([w_up[0], zeros], 0), jnp.concatenate([zeros, w_up[1]], 0)], 0)
    aup = jnp.stack([jnp.concatenate([a_up[0], zeros], 0), jnp.concatenate([zeros, a_up[1]], 0)], 0)
    n_tiles = N // tm
    t8 = tm // 8
    row = lambda i: (i, 0)
    full = lambda i: (0, 0)
    out_sds = jax.ShapeDtypeStruct((N, wc), F32)
    return pl.pallas_call(
        functools.partial(_rwkv_feat_kernel, n_ctx // tm, n_tiles, wc),
        grid=(n_tiles,),
        in_specs=[pl.BlockSpec((tm, zw), row),
                  pl.BlockSpec((8, zw), lambda i: (jnp.maximum(i * t8 - 1, 0), 0)),
                  pl.BlockSpec((8, zw), lambda i: (jnp.minimum((i + 1) * t8, N // 8 - 1), 0)),
                  pl.BlockSpec((2, zw), full), pl.BlockSpec((1, wc), full), pl.BlockSpec((1, wc), full),
                  pl.BlockSpec((2, wc), full), pl.BlockSpec((2, wc), full),
                  pl.BlockSpec((2, LANES, wc), lambda i: (0, 0, 0)), pl.BlockSpec((2, LANES, wc), lambda i: (0, 0, 0)),
                  pl.BlockSpec((LANES, wc), full), pl.BlockSpec((LANES, LANES), full)],
        out_specs=[pl.BlockSpec((tm, wc), row)] * 10,
        out_shape=[out_sds] * 10,
        scratch_shapes=[pltpu.VMEM((tm + 16, zw), F32)],
        compiler_params=_params(("parallel",)), name="rwkv_features",
    )(zr, zr, zr, mu, k_k.reshape(1, wc), k_a.reshape(1, wc), w0, a0, wup, aup, g_up, _seg_ones())


def _rwkv_scan_kernel(rev, pairs, n_sub, r_ref, kap_ref, v_ref, lw_ref, b_ref, kt_ref, y_ref, s_ref):
    L = CHUNK
    L2 = 2 * L

    @pl.when(pl.program_id(1) == 0)
    def _():
        s_ref[...] = jnp.zeros(s_ref.shape, F32)

    t_i = lax.broadcasted_iota(jnp.int32, (L, L), 0)
    s_i = lax.broadcasted_iota(jnp.int32, (L, L), 1)
    m_incl64 = ((s_i >= t_i) if rev else (s_i <= t_i)).astype(F32)
    ri = lax.broadcasted_iota(jnp.int32, (L2, L2), 0)
    qi = lax.broadcasted_iota(jnp.int32, (L2, L2), 1)
    rt, qt = ri % L, qi % L
    same = (ri // L) == (qi // L)
    incl = jnp.logical_and(same, (qt >= rt) if rev else (qt <= rt))
    strict = jnp.logical_and(same, (qt > rt) if rev else (qt < rt))
    eye = (ri == qi).astype(F32)
    lane = lax.broadcasted_iota(jnp.int32, (L, LANES), 1)
    hm = [(lane < RWKV_HEAD).astype(F32), (lane >= RWKV_HEAD).astype(F32)]
    bd = ((lax.broadcasted_iota(jnp.int32, (LANES, LANES), 0) // RWKV_HEAD)
          == (lax.broadcasted_iota(jnp.int32, (LANES, LANES), 1) // RWKV_HEAD)).astype(F32)
    lvl_masks = []
    bsz = 1
    while bsz < L:
        grp = (ri // (2 * bsz)) == (qi // (2 * bsz))
        r_odd = (ri // bsz) % 2 == 1
        q_odd = (qi // bsz) % 2 == 1
        off = jnp.logical_and(jnp.logical_not(r_odd), q_odd) if rev else jnp.logical_and(r_odd, jnp.logical_not(q_odd))
        lvl_masks.append(jnp.logical_and(grp, off))
        bsz *= 2

    def stack2(x):
        return jnp.concatenate([x * hm[0], x * hm[1]], axis=0)

    def bmm(a, b):
        return _dot(a.astype(BF16), b.astype(BF16))

    def chunk(ci, carry):
        cc = (n_sub - 1 - ci) if rev else ci
        rows = pl.ds(pl.multiple_of(cc * L, L), L)
        for p in range(pairs):
            cols = slice(p * LANES, (p + 1) * LANES)
            lw = lw_ref[rows, cols]
            cum = _dot(m_incl64, lw, HI)
            e_incl = jnp.exp(cum)
            e_excl = jnp.exp(cum - lw)
            e_inv = jnp.exp(-cum)
            p_tot = e_incl[0:1, :] if rev else e_incl[L - 1:L, :]
            v = v_ref[rows, cols]
            xk = stack2(kap_ref[rows, cols] * e_excl)
            xr = stack2(r_ref[rows, cols] * e_incl)
            yb = stack2(b_ref[rows, cols] * e_inv)
            yk = stack2(kt_ref[rows, cols] * e_inv)
            vs = stack2(v)
            amat = _dot_nt(jnp.concatenate([xk, xr], axis=0).astype(BF16),
                           jnp.concatenate([yb, yk], axis=0).astype(BF16))
            a_ub = jnp.where(strict, amat[0:L2, 0:L2], 0.0)
            a_uk = jnp.where(strict, amat[0:L2, L2:], 0.0)
            a_rb = jnp.where(incl, amat[L2:, 0:L2], 0.0)
            a_rk = jnp.where(incl, amat[L2:, L2:], 0.0)
            tinv = eye - jnp.where(lvl_masks[0], a_ub, 0.0)
            for lm in lvl_masks[1:]:
                nb = jnp.where(lm, a_ub, 0.0)
                tinv = tinv - _dot(_dot(tinv, nb, HI), tinv, HI)
            s_old = s_ref[p]
            rhs = -_dot_nt(xk.astype(BF16), s_old.astype(BF16)) - bmm(a_uk, vs)
            u = _dot(tinv, rhs, HI)
            ystk = _dot_nt(xr.astype(BF16), s_old.astype(BF16)) + bmm(a_rb, u) + bmm(a_rk, vs)
            y_ref[rows, cols] = ystk[0:L] + ystk[L:]
            upd = _dot_tn(jnp.concatenate([u, vs], axis=0).astype(BF16), jnp.concatenate([yb, yk], axis=0).astype(BF16))
            s_ref[p] = (s_old + upd * bd) * p_tot
        return carry

    lax.fori_loop(0, n_sub, chunk, 0)


def rwkv_scan(rev, n_ctx, r, kap, v, lw, b, kt, pairs=2, n_sub=4):
    N, wc = r.shape
    rb = n_sub * CHUNK
    bw = pairs * LANES
    assert N % rb == 0 and n_ctx % rb == 0 and wc % bw == 0
    nb, nbc = N // rb, n_ctx // rb
    if rev:
        rmap = lambda h, i: (jnp.where(i < nbc, nbc - 1 - i, nb - 1 - (i - nbc)), h)
    else:
        rmap = lambda h, i: (i, h)
    spec = pl.BlockSpec((rb, bw), rmap)
    return pl.pallas_call(
        functools.partial(_rwkv_scan_kernel, rev, pairs, n_sub),
        grid=(wc // bw, nb),
        in_specs=[spec] * 6, out_specs=spec,
        out_shape=jax.ShapeDtypeStruct((N, wc), F32),
        scratch_shapes=[pltpu.VMEM((pairs, LANES, LANES), F32)],
        compiler_params=_params(("parallel", "arbitrary")), name="rwkv_scan_rev" if rev else "rwkv_scan_fwd",
    )(r, kap, v, lw, b, kt)


def _rwkv_out_kernel(yf_ref, yb_ref, r_ref, v_ref, ktf_ref, ktb_ref, g_ref, rk_ref, gn_ref, e_ref, o_ref):
    e128 = e_ref[...]
    inv = 1.0 / RWKV_HEAD
    for s in range(0, o_ref.shape[1], LANES):
        sl = slice(s, s + LANES)
        y = yf_ref[:, sl] + yb_ref[:, sl]
        mu = _seg_sum64(y, e128) * inv
        yc = y - mu
        var = _seg_sum64(yc * yc, e128) * inv
        yn = yc * lax.rsqrt(var + GN_EPS) * gn_ref[0:1, sl] + gn_ref[1:2, sl]
        rk = r_ref[:, sl] * rk_ref[0:1, sl]
        bonus = _seg_sum64(rk * (ktf_ref[:, sl] + ktb_ref[:, sl]), e128) * v_ref[:, sl]
        o_ref[:, sl] = ((yn + bonus) * g_ref[:, sl]).astype(o_ref.dtype)


def rwkv_output(y_f, y_b, r, v, kt_f, kt_b, g, r_k, lnx_gain, lnx_bias, tm=256):
    N, wc = r.shape
    row = lambda i: (i, 0)
    full = lambda i: (0, 0)
    gn = jnp.stack([lnx_gain, lnx_bias], 0)
    return pl.pallas_call(
        _rwkv_out_kernel, grid=(N // tm,),
        in_specs=[pl.BlockSpec((tm, wc), row)] * 7 + [pl.BlockSpec((1, wc), full), pl.BlockSpec((2, wc), full),
                                                      pl.BlockSpec((LANES, LANES), full)],
        out_specs=pl.BlockSpec((tm, wc), row),
        out_shape=jax.ShapeDtypeStruct((N, wc), BF16),
        compiler_params=_params(("parallel",)), name="rwkv_output",
    )(y_f, y_b, r, v, kt_f, kt_b, g, r_k.reshape(1, wc), gn, _seg_ones())


def _angles(i, j, period):
    return (2.0 * math.pi / period) * ((i * j) % period).astype(F32)


def _chan_dft(gdim, scale):
    i = lax.iota(jnp.int32, gdim)
    ang = _angles(i[:, None], i[None, :], gdim)
    return jnp.concatenate([jnp.cos(ang), -jnp.sin(ang)], 1) * scale


def _fnet1_kernel(groups, x_ref, cs_ref, m_ref, y_ref):
    r1 = x_ref.shape[0]
    gdim = x_ref.shape[1] // groups
    for g in range(groups):
        sl = slice(g * gdim, (g + 1) * gdim)
        ab = _dot(x_ref[:, sl], cs_ref[...], HI)
        st = jnp.concatenate([ab[:, :gdim], ab[:, gdim:]], axis=0)
        y = _dot(m_ref[0], st, HI)
        y_ref[0, 0, :, sl] = y[:r1]
        y_ref[1, 0, :, sl] = y[r1:]


def _fnet3_kernel(cs_ref, y_ref, o_ref):
    o_ref[...] = _dot(cs_ref[...], y_ref[...], HI)


def fnet_latent(zf, groups=FOURIER_GROUPS):
    T, fw = zf.shape
    gdim = fw // groups
    r2 = GRID_W
    r1 = T // r2
    cs = _chan_dft(gdim, 1.0 / math.sqrt(T * gdim))
    t1p = lax.iota(jnp.int32, r1)[None, :, None]
    t1 = lax.iota(jnp.int32, r1)[None, None, :]
    t2 = lax.iota(jnp.int32, r2)[:, None, None]
    th = _angles(t1p, r2 * t1 + t2, T)
    gr, gi = jnp.cos(th), -jnp.sin(th)
    mt = jnp.concatenate([jnp.concatenate([gr, -gi], 2), jnp.concatenate([gi, gr], 2)], 1)
    y = pl.pallas_call(
        functools.partial(_fnet1_kernel, groups),
        grid=(r2,),
        in_specs=[pl.BlockSpec((r1, fw), lambda j: (0, j)),
                  pl.BlockSpec((gdim, 2 * gdim), lambda j: (0, 0)),
                  pl.BlockSpec((1, 2 * r1, 2 * r1), lambda j: (j, 0, 0))],
        out_specs=pl.BlockSpec((2, 1, r1, fw), lambda j: (0, j, 0, 0)),
        out_shape=jax.ShapeDtypeStruct((2, r2, r1, fw), F32),
        compiler_params=_params(("parallel",)), name="fnet_rows",
    )(zf.reshape(r1, r2 * fw), cs, mt)
    i2 = lax.iota(jnp.int32, r2)
    ph = _angles(i2[:, None], i2[None, :], r2)
    cs2 = jnp.concatenate([jnp.cos(ph), jnp.sin(ph)], 1)
    tc = 4 * fw if r1 % 4 == 0 else fw
    o = pl.pallas_call(
        _fnet3_kernel, grid=(r1 * fw // tc,),
        in_specs=[pl.BlockSpec((r2, 2 * r2), lambda j: (0, 0)), pl.BlockSpec((2 * r2, tc), lambda j: (0, j))],
        out_specs=pl.BlockSpec((r2, tc), lambda j: (0, j)),
        out_shape=jax.ShapeDtypeStruct((r2, r1 * fw), F32),
        compiler_params=_params(("parallel",)), name="fnet_cols",
    )(cs2, y.reshape(2 * r2, r1 * fw))
    return o.reshape(T, fw)


def _fnet_direct_kernel(x_ref, cs_ref, ct_ref, o_ref):
    gdim = x_ref.shape[1]
    ab = _dot(x_ref[...], cs_ref[...], HI)
    st = jnp.concatenate([ab[:, :gdim], ab[:, gdim:]], axis=0)
    o_ref[...] = _dot(ct_ref[...], st, HI)


def fnet_direct(zf, groups=FOURIER_GROUPS):
    C, fw = zf.shape
    gdim = fw // groups
    cs = _chan_dft(gdim, 1.0 / math.sqrt(C * gdim))
    i = lax.iota(jnp.int32, C)
    ang = _angles(i[:, None], i[None, :], C)
    ct = jnp.concatenate([jnp.cos(ang), jnp.sin(ang)], 1)
    return pl.pallas_call(
        _fnet_direct_kernel, grid=(groups,),
        in_specs=[pl.BlockSpec((C, gdim), lambda g: (0, g)), pl.BlockSpec((gdim, 2 * gdim), lambda g: (0, 0)),
                  pl.BlockSpec((C, 2 * C), lambda g: (0, 0))],
        out_specs=pl.BlockSpec((C, gdim), lambda g: (0, g)),
        out_shape=jax.ShapeDtypeStruct((C, fw), F32),
        compiler_params=_params(("parallel",)), name="fnet_direct",
    )(zf, cs, ct)


GATE_LANE0 = N_GROUPS


def _gates_kernel(lg_ref, o_ref):
    lg = lg_ref[...]
    lane = lax.broadcasted_iota(jnp.int32, lg.shape, 1)
    big = jnp.int32(LANES)
    neg = -jnp.inf
    is_g = lane < N_GROUPS
    gl = jnp.where(is_g, lg, neg)
    mg = jnp.max(gl, axis=-1, keepdims=True)
    p_group = 1.0 / jnp.sum(jnp.exp(gl - mg), axis=-1, keepdims=True)
    gsel = jnp.min(jnp.where(gl == mg, lane, big), axis=-1, keepdims=True)
    e_idx = lane - GATE_LANE0
    in_grp = jnp.logical_and(jnp.logical_and(e_idx >= 0, e_idx < N_EXPERTS), e_idx // EXPERTS_PER_GROUP == gsel)
    le = jnp.where(in_grp, lg, neg)
    m1 = jnp.max(le, axis=-1, keepdims=True)
    l1 = jnp.min(jnp.where(le == m1, lane, big), axis=-1, keepdims=True)
    le2 = jnp.where(lane == l1, neg, le)
    m2 = jnp.max(le2, axis=-1, keepdims=True)
    l2 = jnp.min(jnp.where(le2 == m2, lane, big), axis=-1, keepdims=True)
    e2 = jnp.exp(m2 - m1)
    inv = p_group / (1.0 + e2)
    o_ref[...] = jnp.where(lane == l1, inv, jnp.where(lane == l2, e2 * inv, 0.0))


def moe_gates(logits, tm=256):
    N = logits.shape[0]
    return pl.pallas_call(
        _gates_kernel, grid=(N // tm,),
        in_specs=[pl.BlockSpec((tm, LANES), lambda i: (i, 0))],
        out_specs=pl.BlockSpec((tm, LANES), lambda i: (i, 0)),
        out_shape=jax.ShapeDtypeStruct((N, LANES), F32),
        compiler_params=_params(("parallel",)), name="moe_gates",
    )(logits)


def _moe_dense_kernel(u_ref, g_ref, w1_ref, w3_ref, w2_ref, o_ref, acc_ref):
    e = pl.program_id(1)
    j = pl.program_id(2)

    @pl.when(jnp.logical_and(e == 0, j == 0))
    def _():
        acc_ref[...] = jnp.zeros(acc_ref.shape, F32)

    u = u_ref[...]
    lane = lax.broadcasted_iota(jnp.int32, g_ref.shape, 1)
    gate = jnp.sum(jnp.where(lane == e + GATE_LANE0, g_ref[...], 0.0), axis=-1, keepdims=True)
    a = _dot(u, w1_ref[0])
    b = _dot(u, w3_ref[0])
    h = (a * _sigmoid(a)) * b * gate
    acc_ref[...] += _dot(h.astype(BF16), w2_ref[0])

    @pl.when(jnp.logical_and(e == pl.num_programs(1) - 1, j == pl.num_programs(2) - 1))
    def _():
        o_ref[...] = acc_ref[...]


def moe_dense(u, gates, w1, w3, w2, tm=384, tc=256):
    N, D = u.shape
    E, _, De = w1.shape
    return pl.pallas_call(
        _moe_dense_kernel, grid=(N // tm, E, De // tc),
        in_specs=[pl.BlockSpec((tm, D), lambda i, e, j: (i, 0)), pl.BlockSpec((tm, LANES), lambda i, e, j: (i, 0)),
                  pl.BlockSpec((1, D, tc), lambda i, e, j: (e, 0, j)), pl.BlockSpec((1, D, tc), lambda i, e, j: (e, 0, j)),
                  pl.BlockSpec((1, tc, D), lambda i, e, j: (e, j, 0))],
        out_specs=pl.BlockSpec((tm, D), lambda i, e, j: (i, 0)),
        out_shape=jax.ShapeDtypeStruct((N, D), F32),
        scratch_shapes=[pltpu.VMEM((tm, D), F32)],
        compiler_params=_params(("parallel", "arbitrary", "arbitrary")), name="moe_dense",
    )(u, gates, w1, w3, w2)


def _pick_tile(n, prefer):
    for t in prefer:
        if n % t == 0:
            return t
    raise ValueError(f"no tile for {n}")


def kernel(x, c, ctx, c_ctx, w_mod, b_mod, w_in, q_gain, k_gain, rwkv_mu, w0, w_up, a0, a_up, g_up, k_k, k_a, r_k, lnx_gain, lnx_bias, w_out, ln1_gain, ln1_bias, ln2_gain, ln2_bias, router_group_w, router_group_b, router_expert_w, router_expert_b, w1, w3, w2):
    B, T, D = x.shape
    C = ctx.shape[1]
    assert B == 1
    depth = w_mod.shape[0]
    alpha = (2 * depth) ** 0.25
    N = C + T
    aw = D // 2
    kvw = KV_HEADS * HEAD_DIM
    rw = D // 4
    fw = D // 4
    rcols = 3 * rw + 2 * W_RANK + 2 * A_RANK + G_RANK
    o_zr = aw + 2 * kvw
    o_zf = o_zr + rcols
    tm_big = _pick_tile(N, (768, 384, 256))

    xx = jnp.concatenate([ctx[0], x[0]], 0)
    mods = mod_vectors(c, c_ctx, w_mod, b_mod).reshape(depth, 8, 6, D)
    cosf, sinf = rope_tables(C, T)

    for l in range(depth):
        last = l == depth - 1
        mv = mods[l]
        vec = lambda i: jnp.stack([mv[0, i], mv[1, i]], 0)
        ss1 = jnp.stack([mv[0, 0], mv[0, 1], mv[1, 0], mv[1, 1]], 0)
        ss2 = jnp.stack([mv[0, 3], mv[0, 4], mv[1, 3], mv[1, 4]], 0)

        u = ln_modulate(xx, ss1, C)
        wl = w_in[l].astype(BF16)
        zq = matmul(u, wl[:, :o_zr], F32, tm_big, _pick_tile(o_zr, (768, 512, 256, 128)))
        zr = matmul(u, wl[:, o_zr:o_zf], F32, tm_big, _pick_tile(rcols, (1152, 384, 128)))
        zf = matmul(u, wl[:, o_zf:], F32, tm_big, _pick_tile(fw, (512, 256, 128)))
        qh, kh, vh = qkv_prep(zq, cosf, sinf, q_gain[l], k_gain[l], aw)
        attn = flash_attention(qh, kh, vh, C)
        r, v, kap, g, lwf, bf, ktf, lwb, bb, ktb = rwkv_features(
            zr, C, rwkv_mu[l], w0[l], w_up[l], a0[l], a_up[l], g_up[l], k_k[l], k_a[l])
        y_f = rwkv_scan(False, C, r, kap, v, lwf, bf, ktf)
        y_b = rwkv_scan(True, C, r, kap, v, lwb, bb, ktb)
        rwkv = rwkv_output(y_f, y_b, r, v, ktf, ktb, g, r_k[l].reshape(-1), lnx_gain[l], lnx_bias[l])
        fl = fnet_latent(zf[C:])
        fc = fnet_direct(zf[:C]) if not last else jnp.zeros((C, fw), F32)
        fn = jnp.concatenate([fc, fl], 0).astype(BF16)
        cat = jnp.concatenate([attn, rwkv, fn], 1)
        m = matmul(cat, w_out[l].astype(BF16), F32, tm_big, _pick_tile(D, (512, 256, 128)))
        xx = resid_ln(xx, m, vec(2), jnp.stack([ln1_gain[l], ln1_bias[l]], 0), C, alpha)

        wr = jnp.concatenate([router_group_w[l], router_expert_w[l],
                              jnp.zeros((D, LANES - N_GROUPS - N_EXPERTS), F32)], 1)
        br = jnp.concatenate([router_group_b[l], router_expert_b[l],
                              jnp.zeros((LANES - N_GROUPS - N_EXPERTS,), F32)], 0).reshape(1, LANES)
        u2, logits = ln_modulate(xx, ss2, C, router=(wr, br))
        gates = moe_gates(logits)
        f = moe_dense(u2, gates, w1[l].astype(BF16), w3[l].astype(BF16), w2[l].astype(BF16),
                      tm=_pick_tile(N, (384, 256)))
        xx = resid_ln(xx, f, vec(5), jnp.stack([ln2_gain[l], ln2_bias[l]], 0), C, alpha)
    return xx[C:][None]
```

```python
import functools
import math

import numpy as np
import jax
import jax.numpy as jnp
from jax import lax
from jax.experimental import pallas as pl
from jax.experimental.pallas import tpu as pltpu

F32 = jnp.float32
BF16 = jnp.bfloat16
HI = lax.Precision.HIGHEST

GRID_W = 64
HEAD_DIM = 128
KV_HEADS = 4
RWKV_HEAD = 64
W_RANK = 64
A_RANK = 64
G_RANK = 128
FOURIER_GROUPS = 4
N_GROUPS = 4
EXPERTS_PER_GROUP = 4
N_EXPERTS = N_GROUPS * EXPERTS_PER_GROUP
ROPE_THETA = 10000.0
W_DECAY_SCALE = math.exp(-0.5)
GN_EPS = 64e-5
LN_EPS = 1e-6
LANES = 128
CHUNK = 64

V7X_VMEM_LIMIT_MB = 56


def _params(sem, vmem_mb=V7X_VMEM_LIMIT_MB):
    return pltpu.CompilerParams(dimension_semantics=sem, vmem_limit_bytes=vmem_mb * 1024 * 1024)


def _dot(a, b, prec=None):
    return jnp.dot(a, b, preferred_element_type=F32, precision=prec)


def _dot_nt(a, b, prec=None):
    return lax.dot_general(a, b, (((1,), (1,)), ((), ())), preferred_element_type=F32, precision=prec)


def _dot_tn(a, b, prec=None):
    return lax.dot_general(a, b, (((0,), (0,)), ((), ())), preferred_element_type=F32, precision=prec)


def _sigmoid(x):
    return 1.0 / (1.0 + jnp.exp(-x))


def _mod_kernel(a_ref, w_ref, b_ref, o_ref):
    tn = o_ref.shape[2]
    rows = []
    for r in range(2):
        cols = []
        for j in range(tn // LANES):
            prod = w_ref[0, :, j * LANES:(j + 1) * LANES] * a_ref[r]
            cols.append(jnp.sum(prod, axis=0, keepdims=True))
        rows.append(jnp.concatenate(cols, axis=1) + b_ref[0])
    rows.append(jnp.zeros((6, tn), F32))
    o_ref[0] = jnp.concatenate(rows, axis=0)


def mod_vectors(c, c_ctx, w_mod, b_mod):
    L, D, D6 = w_mod.shape
    acts = jnp.stack([jax.nn.silu(c[0]), jax.nn.silu(c_ctx)], 0)
    a_b = jnp.broadcast_to(acts[:, :, None], (2, D, LANES))
    tn = 512
    return pl.pallas_call(
        _mod_kernel,
        grid=(L, D6 // tn),
        in_specs=[pl.BlockSpec((2, D, LANES), lambda l, j: (0, 0, 0)),
                  pl.BlockSpec((1, D, tn), lambda l, j: (l, 0, j)),
                  pl.BlockSpec((1, 1, tn), lambda l, j: (l, 0, j))],
        out_specs=pl.BlockSpec((1, 8, tn), lambda l, j: (l, 0, j)),
        out_shape=jax.ShapeDtypeStruct((L, 8, D6), F32),
        compiler_params=_params(("parallel", "parallel")),
        name="mod_vectors",
    )(a_b, w_mod, b_mod.reshape(L, 1, D6))


def _ln_rows(x, eps):
    mu = jnp.mean(x, axis=-1, keepdims=True)
    xc = x - mu
    var = jnp.mean(xc * xc, axis=-1, keepdims=True)
    return xc * lax.rsqrt(var + eps)


def _ln_mod_kernel(n_ctx_tiles, with_router, x_ref, ss_ref, *rest):
    if with_router:
        wr_ref, br_ref, u_ref, lg_ref = rest
    else:
        (u_ref,) = rest
    is_ctx = pl.program_id(0) < n_ctx_tiles
    sh = jnp.where(is_ctx, ss_ref[2:3, :], ss_ref[0:1, :])
    sc = jnp.where(is_ctx, ss_ref[3:4, :], ss_ref[1:2, :])
    u = _ln_rows(x_ref[...], LN_EPS) * (1.0 + sc) + sh
    u_ref[...] = u.astype(u_ref.dtype)
    if with_router:
        lg_ref[...] = _dot(u, wr_ref[...], HI) + br_ref[...]


def ln_modulate(xx, ss, n_ctx, router=None, tm=256):
    N, D = xx.shape
    assert N % tm == 0 and n_ctx % tm == 0
    in_specs = [pl.BlockSpec((tm, D), lambda i: (i, 0)), pl.BlockSpec((4, D), lambda i: (0, 0))]
    out_specs = [pl.BlockSpec((tm, D), lambda i: (i, 0))]
    out_shape = [jax.ShapeDtypeStruct((N, D), BF16)]
    args = [xx, ss]
    if router is not None:
        in_specs += [pl.BlockSpec((D, LANES), lambda i: (0, 0)), pl.BlockSpec((1, LANES), lambda i: (0, 0))]
        out_specs.append(pl.BlockSpec((tm, LANES), lambda i: (i, 0)))
        out_shape.append(jax.ShapeDtypeStruct((N, LANES), F32))
        args += list(router)
    res = pl.pallas_call(
        functools.partial(_ln_mod_kernel, n_ctx // tm, router is not None),
        grid=(N // tm,), in_specs=in_specs, out_specs=out_specs, out_shape=out_shape,
        compiler_params=_params(("parallel",)), name="ln_modulate",
    )(*args)
    return res if router is not None else res[0]


def _resid_ln_kernel(alpha, n_ctx_tiles, x_ref, m_ref, g_ref, gb_ref, o_ref):
    is_ctx = pl.program_id(0) < n_ctx_tiles
    g = jnp.where(is_ctx, g_ref[1:2, :], g_ref[0:1, :])
    y = alpha * x_ref[...] + g * m_ref[...].astype(F32)
    o_ref[...] = _ln_rows(y, LN_EPS) * gb_ref[0:1, :] + gb_ref[1:2, :]


def resid_ln(xx, m, gates2, gain_bias, n_ctx, alpha, tm=256):
    N, D = xx.shape
    return pl.pallas_call(
        functools.partial(_resid_ln_kernel, alpha, n_ctx // tm),
        grid=(N // tm,),
        in_specs=[pl.BlockSpec((tm, D), lambda i: (i, 0)), pl.BlockSpec((tm, D), lambda i: (i, 0)),
                  pl.BlockSpec((2, D), lambda i: (0, 0)), pl.BlockSpec((2, D), lambda i: (0, 0))],
        out_specs=pl.BlockSpec((tm, D), lambda i: (i, 0)),
        out_shape=jax.ShapeDtypeStruct((N, D), F32),
        compiler_params=_params(("parallel",)), name="resid_ln",
    )(xx, m, gates2, gain_bias)


def _mm_kernel(a_ref, b_ref, o_ref):
    o_ref[...] = _dot(a_ref[...], b_ref[...]).astype(o_ref.dtype)


def matmul(a, b, out_dtype, tm, tn):
    M, K = a.shape
    _, Nn = b.shape
    assert M % tm == 0 and Nn % tn == 0
    return pl.pallas_call(
        _mm_kernel, grid=(M // tm, Nn // tn),
        in_specs=[pl.BlockSpec((tm, K), lambda i, j: (i, 0)), pl.BlockSpec((K, tn), lambda i, j: (0, j))],
        out_specs=pl.BlockSpec((tm, tn), lambda i, j: (i, j)),
        out_shape=jax.ShapeDtypeStruct((M, Nn), out_dtype),
        compiler_params=_params(("parallel", "arbitrary")), name="matmul",
    )(a, b)


def _qkv_prep_kernel(n_q_heads, scale, z_ref, cos_ref, sin_ref, qg_ref, kg_ref, q_ref, k_ref, v_ref):
    cosf = cos_ref[...]
    sinf = sin_ref[...]

    def norm_rope(t, gain):
        t = t * lax.rsqrt(jnp.mean(t * t, axis=-1, keepdims=True) + LN_EPS) * gain
        return t * cosf + pltpu.roll(t, HEAD_DIM // 2, 1) * sinf

    for h in range(n_q_heads):
        sl = slice(h * HEAD_DIM, (h + 1) * HEAD_DIM)
        q_ref[:, sl] = (norm_rope(z_ref[:, sl], qg_ref[...]) * scale).astype(q_ref.dtype)
    qw = n_q_heads * HEAD_DIM
    for h in range(KV_HEADS):
        sl = slice(h * HEAD_DIM, (h + 1) * HEAD_DIM)
        zs = slice(qw + h * HEAD_DIM, qw + (h + 1) * HEAD_DIM)
        k_ref[:, sl] = norm_rope(z_ref[:, zs], kg_ref[...]).astype(k_ref.dtype)
    kvw = KV_HEADS * HEAD_DIM
    ones = jnp.ones((z_ref.shape[0], HEAD_DIM), v_ref.dtype)
    for h in range(KV_HEADS):
        zs = slice(qw + kvw + h * HEAD_DIM, qw + kvw + (h + 1) * HEAD_DIM)
        v_ref[:, 2 * h * HEAD_DIM:(2 * h + 1) * HEAD_DIM] = z_ref[:, zs].astype(v_ref.dtype)
        v_ref[:, (2 * h + 1) * HEAD_DIM:(2 * h + 2) * HEAD_DIM] = ones


def qkv_prep(z, cosf, sinf, q_gain, k_gain, attn_width, tm=256):
    N = z.shape[0]
    n_q = attn_width // HEAD_DIM
    kvw = KV_HEADS * HEAD_DIM
    zw = attn_width + 2 * kvw
    return pl.pallas_call(
        functools.partial(_qkv_prep_kernel, n_q, HEAD_DIM ** -0.5 * math.log2(math.e)),
        grid=(N // tm,),
        in_specs=[pl.BlockSpec((tm, zw), lambda i: (i, 0)),
                  pl.BlockSpec((tm, HEAD_DIM), lambda i: (i, 0)), pl.BlockSpec((tm, HEAD_DIM), lambda i: (i, 0)),
                  pl.BlockSpec((1, HEAD_DIM), lambda i: (0, 0)), pl.BlockSpec((1, HEAD_DIM), lambda i: (0, 0))],
        out_specs=[pl.BlockSpec((tm, attn_width), lambda i: (i, 0)),
                   pl.BlockSpec((tm, kvw), lambda i: (i, 0)), pl.BlockSpec((tm, 2 * kvw), lambda i: (i, 0))],
        out_shape=[jax.ShapeDtypeStruct((N, attn_width), BF16),
                   jax.ShapeDtypeStruct((N, kvw), BF16), jax.ShapeDtypeStruct((N, 2 * kvw), BF16)],
        compiler_params=_params(("parallel",)), name="qkv_prep",
    )(z, cosf, sinf, q_gain.reshape(1, HEAD_DIM), k_gain.reshape(1, HEAD_DIM))


def _attn_kernel(n_ctx, n_ctx_tiles, group, q_ref, k_ref, v_ref, o_ref):
    tq = q_ref.shape[0]
    hh = group // 2
    halves = [jnp.concatenate([q_ref[:, h * HEAD_DIM:(h + 1) * HEAD_DIM] for h in range(i * hh, (i + 1) * hh)], axis=0)
              for i in range(2)]

    def attend(k, v):
        s = [_dot_nt(qh, k) for qh in halves]
        p = [jnp.exp2(x - jnp.max(x, axis=-1, keepdims=True)).astype(v.dtype) for x in s]
        acc = [_dot(x, v) for x in p]
        for h in range(group):
            a = acc[h // hh][(h % hh) * tq:(h % hh + 1) * tq]
            o_ref[:, h * HEAD_DIM:(h + 1) * HEAD_DIM] = (a[:, :HEAD_DIM] / a[:, HEAD_DIM:]).astype(o_ref.dtype)

    is_ctx = pl.program_id(1) < n_ctx_tiles

    @pl.when(is_ctx)
    def _():
        attend(k_ref[0:n_ctx, :], v_ref[0:n_ctx, :])

    @pl.when(jnp.logical_not(is_ctx))
    def _():
        attend(k_ref[...], v_ref[...])


def attention(q, k, v, n_ctx, tq=128):
    N, aw = q.shape
    group = aw // HEAD_DIM // KV_HEADS
    gw = group * HEAD_DIM
    assert N % tq == 0 and n_ctx % tq == 0
    return pl.pallas_call(
        functools.partial(_attn_kernel, n_ctx, n_ctx // tq, group),
        grid=(KV_HEADS, N // tq),
        in_specs=[pl.BlockSpec((tq, gw), lambda g, i: (i, g)),
                  pl.BlockSpec((N, HEAD_DIM), lambda g, i: (0, g)),
                  pl.BlockSpec((N, 2 * HEAD_DIM), lambda g, i: (0, g))],
        out_specs=pl.BlockSpec((tq, gw), lambda g, i: (i, g)),
        out_shape=jax.ShapeDtypeStruct((N, aw), BF16),
        compiler_params=_params(("parallel", "arbitrary")), name="attention",
    )(q, k, v)


def rope_tables(n_ctx, n_lat):
    n_rows = n_lat // GRID_W
    row = jnp.repeat(jnp.arange(n_rows), GRID_W).astype(F32)
    col = jnp.tile(jnp.arange(GRID_W), n_rows).astype(F32)
    axis_dim = HEAD_DIM // 2
    inv_freq = ROPE_THETA ** (-jnp.arange(0, axis_dim, 2, dtype=F32) / axis_dim)
    ang = jnp.concatenate([row[:, None] * inv_freq, col[:, None] * inv_freq], -1)
    cos, sin = jnp.cos(ang), jnp.sin(ang)
    cosf = jnp.concatenate([cos, cos], -1)
    sinf = jnp.concatenate([-sin, sin], -1)
    cosf = jnp.concatenate([jnp.ones((n_ctx, HEAD_DIM), F32), cosf], 0)
    sinf = jnp.concatenate([jnp.zeros((n_ctx, HEAD_DIM), F32), sinf], 0)
    return cosf, sinf


def _seg_sum64(x, e128):
    return _dot(x, e128, HI)


def _rwkv_feat_kernel(n_ctx_tiles, n_tiles, w_cols,
                      z_ref, zp_ref, zn_ref, mu_ref, kk_ref, ka_ref, w0_ref, a0_ref,
                      wup_ref, aup_ref, gup_ref, e_ref,
                      r_ref, v_ref, kap_ref, g_ref, lwf_ref, bf_ref, ktf_ref, lwb_ref, bb_ref, ktb_ref,
                      scr_ref):
    i = pl.program_id(0)
    tm = z_ref.shape[0]
    first = jnp.logical_or(i == 0, i == n_ctx_tiles)
    last = jnp.logical_or(i == n_ctx_tiles - 1, i == n_tiles - 1)
    scr_ref[8:8 + tm, :] = z_ref[...]
    scr_ref[0:8, :] = jnp.where(first, 0.0, zp_ref[...])
    scr_ref[8 + tm:16 + tm, :] = jnp.where(last, 0.0, zn_ref[...])
    z = z_ref[...]
    prev = scr_ref[7:7 + tm, :]
    nxt = scr_ref[9:9 + tm, :]
    zs = z + mu_ref[0:1, :] * (prev - z) + mu_ref[1:2, :] * (nxt - z)

    r = zs[:, 0:w_cols]
    k = zs[:, w_cols:2 * w_cols]
    v = zs[:, 2 * w_cols:3 * w_cols]
    lora = zs[:, 3 * w_cols:]
    wd = jnp.tanh(lora[:, 0:LANES])
    ad = lora[:, LANES:2 * LANES]
    gd = _sigmoid(lora[:, 2 * LANES:3 * LANES])
    r_ref[...] = r
    v_ref[...] = v
    g_ref[...] = _dot(gd, gup_ref[...], HI)
    kk = k * kk_ref[...]
    e128 = e_ref[...]
    kap = jnp.concatenate(
        [kk[:, s:s + LANES] * lax.rsqrt(jnp.maximum(_seg_sum64(kk[:, s:s + LANES] * kk[:, s:s + LANES], e128), 1e-24))
         for s in range(0, w_cols, LANES)], axis=1)
    kap_ref[...] = kap
    outs = ((lwf_ref, bf_ref, ktf_ref), (lwb_ref, bb_ref, ktb_ref))
    for d in range(2):
        lw_ref, b_ref, kt_ref = outs[d]
        lw_ref[...] = -W_DECAY_SCALE * _sigmoid(w0_ref[d:d + 1, :] + _dot(wd, wup_ref[d], HI))
        a = _sigmoid(a0_ref[d:d + 1, :] + _dot(ad, aup_ref[d], HI))
        kt_ref[...] = k * (1.0 + (a - 1.0) * ka_ref[...])
        b_ref[...] = a * kap


def _seg_ones(width=LANES, seg=RWKV_HEAD):
    i = np.arange(width)
    return jnp.asarray((i[:, None] // seg == i[None, :] // seg).astype(np.float32))


def rwkv_features(zr, n_ctx, mu, w0, w_up, a0, a_up, g_up, k_k, k_a, tm=256):
    N, zw = zr.shape
    wc = k_k.shape[0]
    assert zw == 3 * wc + 3 * LANES and W_RANK + W_RANK == LANES and A_RANK + A_RANK == LANES and G_RANK == LANES
    zeros = jnp.zeros((W_RANK, wc), F32)
    wup = jnp.stack([jnp.concatenate([w_up[0], zeros], 0), jnp.concatenate([zeros, w_up[1]], 0)], 0)
    aup = jnp.stack([jnp.concatenate([a_up[0], zeros], 0), jnp.concatenate([zeros, a_up[1]], 0)], 0)
    n_tiles = N // tm
    t8 = tm // 8
    row = lambda i: (i, 0)
    full = lambda i: (0, 0)
    out_sds = jax.ShapeDtypeStruct((N, wc), F32)
    return pl.pallas_call(
        functools.partial(_rwkv_feat_kernel, n_ctx // tm, n_tiles, wc),
        grid=(n_tiles,),
        in_specs=[pl.BlockSpec((tm, zw), row),
                  pl.BlockSpec((8, zw), lambda i: (jnp.maximum(i * t8 - 1, 0), 0)),
                  pl.BlockSpec((8, zw), lambda i: (jnp.minimum((i + 1) * t8, N // 8 - 1), 0)),
                  pl.BlockSpec((2, zw), full), pl.BlockSpec((1, wc), full), pl.BlockSpec((1, wc), full),
                  pl.BlockSpec((2, wc), full), pl.BlockSpec((2, wc), full),
                  pl.BlockSpec((2, LANES, wc), lambda i: (0, 0, 0)), pl.BlockSpec((2, LANES, wc), lambda i: (0, 0, 0)),
                  pl.BlockSpec((LANES, wc), full), pl.BlockSpec((LANES, LANES), full)],
        out_specs=[pl.BlockSpec((tm, wc), row)] * 10,
        out_shape=[out_sds] * 10,
        scratch_shapes=[pltpu.VMEM((tm + 16, zw), F32)],
        compiler_params=_params(("parallel",)), name="rwkv_features",
    )(zr, zr, zr, mu, k_k.reshape(1, wc), k_a.reshape(1, wc), w0, a0, wup, aup, g_up, _seg_ones())


def _rwkv_scan_kernel(rev, pairs, n_sub, r_ref, kap_ref, v_ref, lw_ref, b_ref, kt_ref, y_ref, s_ref):
    L = CHUNK
    L2 = 2 * L

    @pl.when(pl.program_id(1) == 0)
    def _():
        s_ref[...] = jnp.zeros(s_ref.shape, F32)

    t_i = lax.broadcasted_iota(jnp.int32, (L, L), 0)
    s_i = lax.broadcasted_iota(jnp.int32, (L, L), 1)
    m_incl64 = ((s_i >= t_i) if rev else (s_i <= t_i)).astype(F32)
    ri = lax.broadcasted_iota(jnp.int32, (L2, L2), 0)
    qi = lax.broadcasted_iota(jnp.int32, (L2, L2), 1)
    rt, qt = ri % L, qi % L
    same = (ri // L) == (qi // L)
    incl = jnp.logical_and(same, (qt >= rt) if rev else (qt <= rt))
    strict = jnp.logical_and(same, (qt > rt) if rev else (qt < rt))
    eye = (ri == qi).astype(F32)
    lane = lax.broadcasted_iota(jnp.int32, (L, LANES), 1)
    hm = [(lane < RWKV_HEAD).astype(F32), (lane >= RWKV_HEAD).astype(F32)]
    bd = ((lax.broadcasted_iota(jnp.int32, (LANES, LANES), 0) // RWKV_HEAD)
          == (lax.broadcasted_iota(jnp.int32, (LANES, LANES), 1) // RWKV_HEAD)).astype(F32)
    lvl_masks = []
    bsz = 1
    while bsz < L:
        grp = (ri // (2 * bsz)) == (qi // (2 * bsz))
        r_odd = (ri // bsz) % 2 == 1
        q_odd = (qi // bsz) % 2 == 1
        off = jnp.logical_and(jnp.logical_not(r_odd), q_odd) if rev else jnp.logical_and(r_odd, jnp.logical_not(q_odd))
        lvl_masks.append(jnp.logical_and(grp, off))
        bsz *= 2

    def stack2(x):
        return jnp.concatenate([x * hm[0], x * hm[1]], axis=0)

    def bmm(a, b):
        return _dot(a.astype(BF16), b.astype(BF16))

    m_incl_bf = m_incl64.astype(BF16)

    units = [(ci, p) for ci in range(n_sub) for p in range(pairs)]

    def blk(ref, unit):
        ci, p = unit
        cc = (n_sub - 1 - ci) if rev else ci
        return ref[cc * L:(cc + 1) * L, p * LANES:(p + 1) * LANES]

    def each(fn, *dicts):
        return {un: fn(*(d[un] for d in dicts)) for un in units}

    def split3(x):
        hi = x.astype(BF16)
        r1 = x - hi.astype(F32)
        mid = r1.astype(BF16)
        lo = (r1 - mid.astype(F32)).astype(BF16)
        return jnp.concatenate([hi, mid, lo], axis=1)

    lw = {un: blk(lw_ref, un) for un in units}
    c3 = each(lambda x: _dot(m_incl_bf, split3(x)), lw)
    cum = each(lambda c: c[:, 0:LANES] + c[:, LANES:2 * LANES] + c[:, 2 * LANES:], c3)
    e_incl = each(jnp.exp, cum)
    e_inv = each(lambda c: jnp.exp(-c), cum)
    p_tot = each(lambda e: e[0:1, :] if rev else e[L - 1:L, :], e_incl)
    xk = {un: stack2(blk(kap_ref, un) * jnp.exp(cum[un] - lw[un])).astype(BF16) for un in units}
    xr = {un: stack2(blk(r_ref, un) * e_incl[un]).astype(BF16) for un in units}
    yb = {un: stack2(blk(b_ref, un) * e_inv[un]).astype(BF16) for un in units}
    yk = {un: stack2(blk(kt_ref, un) * e_inv[un]).astype(BF16) for un in units}
    vs = {un: stack2(blk(v_ref, un)).astype(BF16) for un in units}
    amat = each(lambda a, b, c, d: _dot_nt(jnp.concatenate([a, b], axis=0), jnp.concatenate([c, d], axis=0)),
                xk, xr, yb, yk)
    a_ub = each(lambda a: jnp.where(strict, a[0:L2, 0:L2], 0.0), amat)
    a_uk = each(lambda a: jnp.where(strict, a[0:L2, L2:], 0.0).astype(BF16), amat)
    a_rb = each(lambda a: jnp.where(incl, a[L2:, 0:L2], 0.0).astype(BF16), amat)
    a_rk = each(lambda a: jnp.where(incl, a[L2:, L2:], 0.0).astype(BF16), amat)
    tinv = each(lambda a: eye - jnp.where(lvl_masks[0], a, 0.0), a_ub)
    for lm in lvl_masks[1:]:
        tb = each(lambda t: t.astype(BF16), tinv)
        x1 = each(lambda a, t: _dot(jnp.where(lm, a, 0.0).astype(BF16), t).astype(BF16), a_ub, tb)
        tinv = each(lambda t, tbf, x: t - _dot(tbf, x), tinv, tb, x1)
    tb = each(lambda t: t.astype(BF16), tinv)
    w1 = each(lambda t, x: _dot(t, x).astype(BF16), tb, xk)
    avs = each(lambda a, v: _dot(a, v).astype(BF16), a_uk, vs)
    y_ind = each(_dot, a_rk, vs)
    k2 = each(_dot_tn, vs, yk)
    w2 = each(_dot, tb, avs)

    state = [s_ref[p] for p in range(pairs)]
    for ci in range(n_sub):
        wx = [_dot_nt(jnp.concatenate([w1[(ci, p)], xr[(ci, p)]], axis=0), state[p].astype(BF16))
              for p in range(pairs)]
        ub = [(-(wx[p][0:L2] + w2[(ci, p)])).astype(BF16) for p in range(pairs)]
        upd = [_dot_tn(ub[p], yb[(ci, p)]) + k2[(ci, p)] for p in range(pairs)]
        state = [(state[p] + upd[p] * bd) * p_tot[(ci, p)] for p in range(pairs)]
        cc = (n_sub - 1 - ci) if rev else ci
        for p in range(pairs):
            ystk = wx[p][L2:] + _dot(a_rb[(ci, p)], ub[p]) + y_ind[(ci, p)]
            y_ref[cc * L:(cc + 1) * L, p * LANES:(p + 1) * LANES] = ystk[0:L] + ystk[L:]
    for p in range(pairs):
        s_ref[p] = state[p]


def rwkv_scan(rev, n_ctx, r, kap, v, lw, b, kt, pairs=4, n_sub=4):
    N, wc = r.shape
    rb = n_sub * CHUNK
    bw = pairs * LANES
    assert N % rb == 0 and n_ctx % rb == 0 and wc % bw == 0
    nb, nbc = N // rb, n_ctx // rb
    if rev:
        rmap = lambda h, i: (jnp.where(i < nbc, nbc - 1 - i, nb - 1 - (i - nbc)), h)
    else:
        rmap = lambda h, i: (i, h)
    spec = pl.BlockSpec((rb, bw), rmap)
    return pl.pallas_call(
        functools.partial(_rwkv_scan_kernel, rev, pairs, n_sub),
        grid=(wc // bw, nb),
        in_specs=[spec] * 6, out_specs=spec,
        out_shape=jax.ShapeDtypeStruct((N, wc), F32),
        scratch_shapes=[pltpu.VMEM((pairs, LANES, LANES), F32)],
        compiler_params=_params(("parallel", "arbitrary")), name="rwkv_scan_rev" if rev else "rwkv_scan_fwd",
    )(r, kap, v, lw, b, kt)


def _rwkv_out_kernel(yf_ref, yb_ref, r_ref, v_ref, ktf_ref, ktb_ref, g_ref, rk_ref, gn_ref, e_ref, o_ref):
    e128 = e_ref[...]
    inv = 1.0 / RWKV_HEAD
    for s in range(0, o_ref.shape[1], LANES):
        sl = slice(s, s + LANES)
        y = yf_ref[:, sl] + yb_ref[:, sl]
        mu = _seg_sum64(y, e128) * inv
        yc = y - mu
        var = _seg_sum64(yc * yc, e128) * inv
        yn = yc * lax.rsqrt(var + GN_EPS) * gn_ref[0:1, sl] + gn_ref[1:2, sl]
        rk = r_ref[:, sl] * rk_ref[0:1, sl]
        bonus = _seg_sum64(rk * (ktf_ref[:, sl] + ktb_ref[:, sl]), e128) * v_ref[:, sl]
        o_ref[:, sl] = ((yn + bonus) * g_ref[:, sl]).astype(o_ref.dtype)


def rwkv_output(y_f, y_b, r, v, kt_f, kt_b, g, r_k, lnx_gain, lnx_bias, tm=256):
    N, wc = r.shape
    row = lambda i: (i, 0)
    full = lambda i: (0, 0)
    gn = jnp.stack([lnx_gain, lnx_bias], 0)
    return pl.pallas_call(
        _rwkv_out_kernel, grid=(N // tm,),
        in_specs=[pl.BlockSpec((tm, wc), row)] * 7 + [pl.BlockSpec((1, wc), full), pl.BlockSpec((2, wc), full),
                                                      pl.BlockSpec((LANES, LANES), full)],
        out_specs=pl.BlockSpec((tm, wc), row),
        out_shape=jax.ShapeDtypeStruct((N, wc), BF16),
        compiler_params=_params(("parallel",)), name="rwkv_output",
    )(y_f, y_b, r, v, kt_f, kt_b, g, r_k.reshape(1, wc), gn, _seg_ones())


def _angles(i, j, period):
    return (2.0 * math.pi / period) * ((i * j) % period).astype(F32)


def _chan_dft(gdim, scale):
    i = lax.iota(jnp.int32, gdim)
    ang = _angles(i[:, None], i[None, :], gdim)
    return jnp.concatenate([jnp.cos(ang), -jnp.sin(ang)], 1) * scale


def _fnet1_kernel(groups, x_ref, cs_ref, m_ref, y_ref):
    r1 = x_ref.shape[0]
    gdim = x_ref.shape[1] // groups
    for g in range(groups):
        sl = slice(g * gdim, (g + 1) * gdim)
        ab = _dot(x_ref[:, sl], cs_ref[...], HI)
        st = jnp.concatenate([ab[:, :gdim], ab[:, gdim:]], axis=0)
        y = _dot(m_ref[0], st, HI)
        y_ref[0, 0, :, sl] = y[:r1]
        y_ref[1, 0, :, sl] = y[r1:]


def _fnet3_kernel(cs_ref, y_ref, o_ref):
    o_ref[...] = _dot(cs_ref[...], y_ref[...], HI)


def fnet_latent(zf, groups=FOURIER_GROUPS):
    T, fw = zf.shape
    gdim = fw // groups
    r2 = GRID_W
    r1 = T // r2
    cs = _chan_dft(gdim, 1.0 / math.sqrt(T * gdim))
    t1p = lax.iota(jnp.int32, r1)[None, :, None]
    t1 = lax.iota(jnp.int32, r1)[None, None, :]
    t2 = lax.iota(jnp.int32, r2)[:, None, None]
    th = _angles(t1p, r2 * t1 + t2, T)
    gr, gi = jnp.cos(th), -jnp.sin(th)
    mt = jnp.concatenate([jnp.concatenate([gr, -gi], 2), jnp.concatenate([gi, gr], 2)], 1)
    y = pl.pallas_call(
        functools.partial(_fnet1_kernel, groups),
        grid=(r2,),
        in_specs=[pl.BlockSpec((r1, fw), lambda j: (0, j)),
                  pl.BlockSpec((gdim, 2 * gdim), lambda j: (0, 0)),
                  pl.BlockSpec((1, 2 * r1, 2 * r1), lambda j: (j, 0, 0))],
        out_specs=pl.BlockSpec((2, 1, r1, fw), lambda j: (0, j, 0, 0)),
        out_shape=jax.ShapeDtypeStruct((2, r2, r1, fw), F32),
        compiler_params=_params(("parallel",)), name="fnet_rows",
    )(zf.reshape(r1, r2 * fw), cs, mt)
    i2 = lax.iota(jnp.int32, r2)
    ph = _angles(i2[:, None], i2[None, :], r2)
    cs2 = jnp.concatenate([jnp.cos(ph), jnp.sin(ph)], 1)
    tc = 4 * fw if r1 % 4 == 0 else fw
    o = pl.pallas_call(
        _fnet3_kernel, grid=(r1 * fw // tc,),
        in_specs=[pl.BlockSpec((r2, 2 * r2), lambda j: (0, 0)), pl.BlockSpec((2 * r2, tc), lambda j: (0, j))],
        out_specs=pl.BlockSpec((r2, tc), lambda j: (0, j)),
        out_shape=jax.ShapeDtypeStruct((r2, r1 * fw), F32),
        compiler_params=_params(("parallel",)), name="fnet_cols",
    )(cs2, y.reshape(2 * r2, r1 * fw))
    return o.reshape(T, fw)


def _fnet_direct_kernel(x_ref, cs_ref, ct_ref, o_ref):
    gdim = x_ref.shape[1]
    ab = _dot(x_ref[...], cs_ref[...], HI)
    st = jnp.concatenate([ab[:, :gdim], ab[:, gdim:]], axis=0)
    o_ref[...] = _dot(ct_ref[...], st, HI)


def fnet_direct(zf, groups=FOURIER_GROUPS):
    C, fw = zf.shape
    gdim = fw // groups
    cs = _chan_dft(gdim, 1.0 / math.sqrt(C * gdim))
    i = lax.iota(jnp.int32, C)
    ang = _angles(i[:, None], i[None, :], C)
    ct = jnp.concatenate([jnp.cos(ang), jnp.sin(ang)], 1)
    return pl.pallas_call(
        _fnet_direct_kernel, grid=(groups,),
        in_specs=[pl.BlockSpec((C, gdim), lambda g: (0, g)), pl.BlockSpec((gdim, 2 * gdim), lambda g: (0, 0)),
                  pl.BlockSpec((C, 2 * C), lambda g: (0, 0))],
        out_specs=pl.BlockSpec((C, gdim), lambda g: (0, g)),
        out_shape=jax.ShapeDtypeStruct((C, fw), F32),
        compiler_params=_params(("parallel",)), name="fnet_direct",
    )(zf, cs, ct)


GATE_LANE0 = N_GROUPS


def _gates_kernel(lg_ref, o_ref):
    lg = lg_ref[...]
    lane = lax.broadcasted_iota(jnp.int32, lg.shape, 1)
    big = jnp.int32(LANES)
    neg = -jnp.inf
    is_g = lane < N_GROUPS
    gl = jnp.where(is_g, lg, neg)
    mg = jnp.max(gl, axis=-1, keepdims=True)
    p_group = 1.0 / jnp.sum(jnp.exp(gl - mg), axis=-1, keepdims=True)
    gsel = jnp.min(jnp.where(gl == mg, lane, big), axis=-1, keepdims=True)
    e_idx = lane - GATE_LANE0
    in_grp = jnp.logical_and(jnp.logical_and(e_idx >= 0, e_idx < N_EXPERTS), e_idx // EXPERTS_PER_GROUP == gsel)
    le = jnp.where(in_grp, lg, neg)
    m1 = jnp.max(le, axis=-1, keepdims=True)
    l1 = jnp.min(jnp.where(le == m1, lane, big), axis=-1, keepdims=True)
    le2 = jnp.where(lane == l1, neg, le)
    m2 = jnp.max(le2, axis=-1, keepdims=True)
    l2 = jnp.min(jnp.where(le2 == m2, lane, big), axis=-1, keepdims=True)
    e2 = jnp.exp(m2 - m1)
    inv = p_group / (1.0 + e2)
    o_ref[...] = jnp.where(lane == l1, inv, jnp.where(lane == l2, e2 * inv, 0.0))


def moe_gates(logits, tm=256):
    N = logits.shape[0]
    return pl.pallas_call(
        _gates_kernel, grid=(N // tm,),
        in_specs=[pl.BlockSpec((tm, LANES), lambda i: (i, 0))],
        out_specs=pl.BlockSpec((tm, LANES), lambda i: (i, 0)),
        out_shape=jax.ShapeDtypeStruct((N, LANES), F32),
        compiler_params=_params(("parallel",)), name="moe_gates",
    )(logits)


def _moe_dense_kernel(u_ref, g_ref, w1_ref, w3_ref, w2_ref, o_ref, acc_ref):
    e = pl.program_id(1)
    j = pl.program_id(2)

    @pl.when(jnp.logical_and(e == 0, j == 0))
    def _():
        acc_ref[...] = jnp.zeros(acc_ref.shape, F32)

    u = u_ref[...]
    lane = lax.broadcasted_iota(jnp.int32, g_ref.shape, 1)
    gate = jnp.sum(jnp.where(lane == e + GATE_LANE0, g_ref[...], 0.0), axis=-1, keepdims=True)
    a = _dot(u, w1_ref[0])
    b = _dot(u, w3_ref[0])
    h = (a * _sigmoid(a)) * b * gate
    acc_ref[...] += _dot(h.astype(BF16), w2_ref[0])

    @pl.when(jnp.logical_and(e == pl.num_programs(1) - 1, j == pl.num_programs(2) - 1))
    def _():
        o_ref[...] = acc_ref[...]


def moe_dense(u, gates, w1, w3, w2, tm=384, tc=256):
    N, D = u.shape
    E, _, De = w1.shape
    return pl.pallas_call(
        _moe_dense_kernel, grid=(N // tm, E, De // tc),
        in_specs=[pl.BlockSpec((tm, D), lambda i, e, j: (i, 0)), pl.BlockSpec((tm, LANES), lambda i, e, j: (i, 0)),
                  pl.BlockSpec((1, D, tc), lambda i, e, j: (e, 0, j)), pl.BlockSpec((1, D, tc), lambda i, e, j: (e, 0, j)),
                  pl.BlockSpec((1, tc, D), lambda i, e, j: (e, j, 0))],
        out_specs=pl.BlockSpec((tm, D), lambda i, e, j: (i, 0)),
        out_shape=jax.ShapeDtypeStruct((N, D), F32),
        scratch_shapes=[pltpu.VMEM((tm, D), F32)],
        compiler_params=_params(("parallel", "arbitrary", "arbitrary")), name="moe_dense",
    )(u, gates, w1, w3, w2)


def _pick_tile(n, prefer):
    for t in prefer:
        if n % t == 0:
            return t
    raise ValueError(f"no tile for {n}")


def kernel(x, c, ctx, c_ctx, w_mod, b_mod, w_in, q_gain, k_gain, rwkv_mu, w0, w_up, a0, a_up, g_up, k_k, k_a, r_k, lnx_gain, lnx_bias, w_out, ln1_gain, ln1_bias, ln2_gain, ln2_bias, router_group_w, router_group_b, router_expert_w, router_expert_b, w1, w3, w2):
    B, T, D = x.shape
    C = ctx.shape[1]
    assert B == 1
    depth = w_mod.shape[0]
    alpha = (2 * depth) ** 0.25
    N = C + T
    aw = D // 2
    kvw = KV_HEADS * HEAD_DIM
    rw = D // 4
    fw = D // 4
    rcols = 3 * rw + 2 * W_RANK + 2 * A_RANK + G_RANK
    o_zr = aw + 2 * kvw
    o_zf = o_zr + rcols
    tm_big = _pick_tile(N, (768, 384, 256))

    xx = jnp.concatenate([ctx[0], x[0]], 0)
    mods = mod_vectors(c, c_ctx, w_mod, b_mod).reshape(depth, 8, 6, D)
    cosf, sinf = rope_tables(C, T)

    for l in range(depth):
        last = l == depth - 1
        mv = mods[l]
        vec = lambda i: jnp.stack([mv[0, i], mv[1, i]], 0)
        ss1 = jnp.stack([mv[0, 0], mv[0, 1], mv[1, 0], mv[1, 1]], 0)
        ss2 = jnp.stack([mv[0, 3], mv[0, 4], mv[1, 3], mv[1, 4]], 0)

        u = ln_modulate(xx, ss1, C)
        wl = w_in[l].astype(BF16)
        zq = matmul(u, wl[:, :o_zr], F32, tm_big, _pick_tile(o_zr, (768, 512, 256, 128)))
        zr = matmul(u, wl[:, o_zr:o_zf], F32, tm_big, _pick_tile(rcols, (1152, 384, 128)))
        zf = matmul(u, wl[:, o_zf:], F32, tm_big, _pick_tile(fw, (512, 256, 128)))
        qh, kh, vh = qkv_prep(zq, cosf, sinf, q_gain[l], k_gain[l], aw)
        attn = attention(qh, kh, vh, C)
        r, v, kap, g, lwf, bf, ktf, lwb, bb, ktb = rwkv_features(
            zr, C, rwkv_mu[l], w0[l], w_up[l], a0[l], a_up[l], g_up[l], k_k[l], k_a[l])
        y_f = rwkv_scan(False, C, r, kap, v, lwf, bf, ktf)
        y_b = rwkv_scan(True, C, r, kap, v, lwb, bb, ktb)
        rwkv = rwkv_output(y_f, y_b, r, v, ktf, ktb, g, r_k[l].reshape(-1), lnx_gain[l], lnx_bias[l])
        fl = fnet_latent(zf[C:])
        fc = fnet_direct(zf[:C]) if not last else jnp.zeros((C, fw), F32)
        fn = jnp.concatenate([fc, fl], 0).astype(BF16)
        cat = jnp.concatenate([attn, rwkv, fn], 1)
        m = matmul(cat, w_out[l].astype(BF16), F32, tm_big, _pick_tile(D, (512, 256, 128)))
        xx = resid_ln(xx, m, vec(2), jnp.stack([ln1_gain[l], ln1_bias[l]], 0), C, alpha)

        wr = jnp.concatenate([router_group_w[l], router_expert_w[l],
                              jnp.zeros((D, LANES - N_GROUPS - N_EXPERTS), F32)], 1)
        br = jnp.concatenate([router_group_b[l], router_expert_b[l],
                              jnp.zeros((LANES - N_GROUPS - N_EXPERTS,), F32)], 0).reshape(1, LANES)
        u2, logits = ln_modulate(xx, ss2, C, router=(wr, br))
        gates = moe_gates(logits)
        f = moe_dense(u2, gates, w1[l].astype(BF16), w3[l].astype(BF16), w2[l].astype(BF16),
                      tm=_pick_tile(N, (384, 256)))
        xx = resid_ln(xx, f, vec(5), jnp.stack([ln2_gain[l], ln2_bias[l]], 0), C, alpha)
    return xx[C:][None]
```

```python
import functools
import math

import numpy as np
import jax
import jax.numpy as jnp
from jax import lax
from jax.experimental import pallas as pl
from jax.experimental.pallas import tpu as pltpu

F32 = jnp.float32
BF16 = jnp.bfloat16
HI = lax.Precision.HIGHEST

GRID_W = 64
HEAD_DIM = 128
KV_HEADS = 4
RWKV_HEAD = 64
W_RANK = 64
A_RANK = 64
G_RANK = 128
FOURIER_GROUPS = 4
N_GROUPS = 4
EXPERTS_PER_GROUP = 4
N_EXPERTS = N_GROUPS * EXPERTS_PER_GROUP
ROPE_THETA = 10000.0
W_DECAY_SCALE = math.exp(-0.5)
GN_EPS = 64e-5
LN_EPS = 1e-6
LANES = 128
CHUNK = 64

V7X_VMEM_LIMIT_MB = 56


def _params(sem, vmem_mb=V7X_VMEM_LIMIT_MB):
    return pltpu.CompilerParams(dimension_semantics=sem, vmem_limit_bytes=vmem_mb * 1024 * 1024)


def _dot(a, b, prec=None):
    return jnp.dot(a, b, preferred_element_type=F32, precision=prec)


def _dot_nt(a, b, prec=None):
    return lax.dot_general(a, b, (((1,), (1,)), ((), ())), preferred_element_type=F32, precision=prec)


def _dot_tn(a, b, prec=None):
    return lax.dot_general(a, b, (((0,), (0,)), ((), ())), preferred_element_type=F32, precision=prec)


def _sigmoid(x):
    return 1.0 / (1.0 + jnp.exp(-x))


def _mod_kernel(a_ref, w_ref, b_ref, o_ref):
    tn = o_ref.shape[2]
    rows = []
    for r in range(2):
        cols = []
        for j in range(tn // LANES):
            prod = w_ref[0, :, j * LANES:(j + 1) * LANES] * a_ref[r]
            cols.append(jnp.sum(prod, axis=0, keepdims=True))
        rows.append(jnp.concatenate(cols, axis=1) + b_ref[0])
    rows.append(jnp.zeros((6, tn), F32))
    o_ref[0] = jnp.concatenate(rows, axis=0)


def mod_vectors(c, c_ctx, w_mod, b_mod):
    L, D, D6 = w_mod.shape
    acts = jnp.stack([jax.nn.silu(c[0]), jax.nn.silu(c_ctx)], 0)
    a_b = jnp.broadcast_to(acts[:, :, None], (2, D, LANES))
    tn = 512
    return pl.pallas_call(
        _mod_kernel,
        grid=(L, D6 // tn),
        in_specs=[pl.BlockSpec((2, D, LANES), lambda l, j: (0, 0, 0)),
                  pl.BlockSpec((1, D, tn), lambda l, j: (l, 0, j)),
                  pl.BlockSpec((1, 1, tn), lambda l, j: (l, 0, j))],
        out_specs=pl.BlockSpec((1, 8, tn), lambda l, j: (l, 0, j)),
        out_shape=jax.ShapeDtypeStruct((L, 8, D6), F32),
        compiler_params=_params(("parallel", "parallel")),
        name="mod_vectors",
    )(a_b, w_mod, b_mod.reshape(L, 1, D6))


def _ln_rows(x, eps):
    mu = jnp.mean(x, axis=-1, keepdims=True)
    xc = x - mu
    var = jnp.mean(xc * xc, axis=-1, keepdims=True)
    return xc * lax.rsqrt(var + eps)


def _ln_mod_kernel(n_ctx_tiles, with_router, x_ref, ss_ref, *rest):
    if with_router:
        wr_ref, br_ref, u_ref, lg_ref = rest
    else:
        (u_ref,) = rest
    is_ctx = pl.program_id(0) < n_ctx_tiles
    sh = jnp.where(is_ctx, ss_ref[2:3, :], ss_ref[0:1, :])
    sc = jnp.where(is_ctx, ss_ref[3:4, :], ss_ref[1:2, :])
    u = _ln_rows(x_ref[...], LN_EPS) * (1.0 + sc) + sh
    u_ref[...] = u.astype(u_ref.dtype)
    if with_router:
        lg_ref[...] = _dot(u, wr_ref[...], HI) + br_ref[...]


def ln_modulate(xx, ss, n_ctx, router=None, out_dtype=BF16, tm=256):
    N, D = xx.shape
    assert N % tm == 0 and n_ctx % tm == 0
    in_specs = [pl.BlockSpec((tm, D), lambda i: (i, 0)), pl.BlockSpec((4, D), lambda i: (0, 0))]
    out_specs = [pl.BlockSpec((tm, D), lambda i: (i, 0))]
    out_shape = [jax.ShapeDtypeStruct((N, D), out_dtype)]
    args = [xx, ss]
    if router is not None:
        in_specs += [pl.BlockSpec((D, LANES), lambda i: (0, 0)), pl.BlockSpec((1, LANES), lambda i: (0, 0))]
        out_specs.append(pl.BlockSpec((tm, LANES), lambda i: (i, 0)))
        out_shape.append(jax.ShapeDtypeStruct((N, LANES), F32))
        args += list(router)
    res = pl.pallas_call(
        functools.partial(_ln_mod_kernel, n_ctx // tm, router is not None),
        grid=(N // tm,), in_specs=in_specs, out_specs=out_specs, out_shape=out_shape,
        compiler_params=_params(("parallel",)), name="ln_modulate",
    )(*args)
    return res if router is not None else res[0]


def _resid_ln_kernel(alpha, n_ctx_tiles, x_ref, m_ref, g_ref, gb_ref, o_ref):
    is_ctx = pl.program_id(0) < n_ctx_tiles
    g = jnp.where(is_ctx, g_ref[1:2, :], g_ref[0:1, :])
    y = alpha * x_ref[...] + g * m_ref[...].astype(F32)
    o_ref[...] = _ln_rows(y, LN_EPS) * gb_ref[0:1, :] + gb_ref[1:2, :]


def resid_ln(xx, m, gates2, gain_bias, n_ctx, alpha, tm=256):
    N, D = xx.shape
    return pl.pallas_call(
        functools.partial(_resid_ln_kernel, alpha, n_ctx // tm),
        grid=(N // tm,),
        in_specs=[pl.BlockSpec((tm, D), lambda i: (i, 0)), pl.BlockSpec((tm, D), lambda i: (i, 0)),
                  pl.BlockSpec((2, D), lambda i: (0, 0)), pl.BlockSpec((2, D), lambda i: (0, 0))],
        out_specs=pl.BlockSpec((tm, D), lambda i: (i, 0)),
        out_shape=jax.ShapeDtypeStruct((N, D), F32),
        compiler_params=_params(("parallel",)), name="resid_ln",
    )(xx, m, gates2, gain_bias)


def _mm_kernel(a_ref, b_ref, o_ref):
    o_ref[...] = _dot(a_ref[...], b_ref[...]).astype(o_ref.dtype)


def matmul(a, b, out_dtype, tm, tn):
    M, K = a.shape
    _, Nn = b.shape
    assert M % tm == 0 and Nn % tn == 0
    return pl.pallas_call(
        _mm_kernel, grid=(M // tm, Nn // tn),
        in_specs=[pl.BlockSpec((tm, K), lambda i, j: (i, 0)), pl.BlockSpec((K, tn), lambda i, j: (0, j))],
        out_specs=pl.BlockSpec((tm, tn), lambda i, j: (i, j)),
        out_shape=jax.ShapeDtypeStruct((M, Nn), out_dtype),
        compiler_params=_params(("parallel", "arbitrary")), name="matmul",
    )(a, b)


def _qkv_prep_kernel(n_q_heads, scale, z_ref, cos_ref, sin_ref, qg_ref, kg_ref, q_ref, k_ref, v_ref):
    cosf = cos_ref[...]
    sinf = sin_ref[...]

    def norm_rope(t, gain):
        t = t * lax.rsqrt(jnp.mean(t * t, axis=-1, keepdims=True) + LN_EPS) * gain
        return t * cosf + pltpu.roll(t, HEAD_DIM // 2, 1) * sinf

    for h in range(n_q_heads):
        sl = slice(h * HEAD_DIM, (h + 1) * HEAD_DIM)
        q_ref[:, sl] = (norm_rope(z_ref[:, sl], qg_ref[...]) * scale).astype(q_ref.dtype)
    qw = n_q_heads * HEAD_DIM
    for h in range(KV_HEADS):
        sl = slice(h * HEAD_DIM, (h + 1) * HEAD_DIM)
        zs = slice(qw + h * HEAD_DIM, qw + (h + 1) * HEAD_DIM)
        k_ref[:, sl] = norm_rope(z_ref[:, zs], kg_ref[...]).astype(k_ref.dtype)
    kvw = KV_HEADS * HEAD_DIM
    ones = jnp.ones((z_ref.shape[0], HEAD_DIM), v_ref.dtype)
    for h in range(KV_HEADS):
        zs = slice(qw + kvw + h * HEAD_DIM, qw + kvw + (h + 1) * HEAD_DIM)
        v_ref[:, 2 * h * HEAD_DIM:(2 * h + 1) * HEAD_DIM] = z_ref[:, zs].astype(v_ref.dtype)
        v_ref[:, (2 * h + 1) * HEAD_DIM:(2 * h + 2) * HEAD_DIM] = ones


def qkv_prep(z, cosf, sinf, q_gain, k_gain, attn_width, tm=256):
    N = z.shape[0]
    n_q = attn_width // HEAD_DIM
    kvw = KV_HEADS * HEAD_DIM
    zw = attn_width + 2 * kvw
    return pl.pallas_call(
        functools.partial(_qkv_prep_kernel, n_q, HEAD_DIM ** -0.5 * math.log2(math.e)),
        grid=(N // tm,),
        in_specs=[pl.BlockSpec((tm, zw), lambda i: (i, 0)),
                  pl.BlockSpec((tm, HEAD_DIM), lambda i: (i, 0)), pl.BlockSpec((tm, HEAD_DIM), lambda i: (i, 0)),
                  pl.BlockSpec((1, HEAD_DIM), lambda i: (0, 0)), pl.BlockSpec((1, HEAD_DIM), lambda i: (0, 0))],
        out_specs=[pl.BlockSpec((tm, attn_width), lambda i: (i, 0)),
                   pl.BlockSpec((tm, kvw), lambda i: (i, 0)), pl.BlockSpec((tm, 2 * kvw), lambda i: (i, 0))],
        out_shape=[jax.ShapeDtypeStruct((N, attn_width), BF16),
                   jax.ShapeDtypeStruct((N, kvw), BF16), jax.ShapeDtypeStruct((N, 2 * kvw), BF16)],
        compiler_params=_params(("parallel",)), name="qkv_prep",
    )(z, cosf, sinf, q_gain.reshape(1, HEAD_DIM), k_gain.reshape(1, HEAD_DIM))


def _attn_kernel(n_ctx, n_ctx_tiles, group, q_ref, k_ref, v_ref, o_ref):
    tq = q_ref.shape[0]
    hh = group // 2
    halves = [jnp.concatenate([q_ref[:, h * HEAD_DIM:(h + 1) * HEAD_DIM] for h in range(i * hh, (i + 1) * hh)], axis=0)
              for i in range(2)]

    def attend(k, v):
        s = [_dot_nt(qh, k) for qh in halves]
        p = [jnp.exp2(x - jnp.max(x, axis=-1, keepdims=True)).astype(v.dtype) for x in s]
        acc = [_dot(x, v) for x in p]
        for h in range(group):
            a = acc[h // hh][(h % hh) * tq:(h % hh + 1) * tq]
            o_ref[:, h * HEAD_DIM:(h + 1) * HEAD_DIM] = (a[:, :HEAD_DIM] / a[:, HEAD_DIM:]).astype(o_ref.dtype)

    is_ctx = pl.program_id(1) < n_ctx_tiles

    @pl.when(is_ctx)
    def _():
        attend(k_ref[0:n_ctx, :], v_ref[0:n_ctx, :])

    @pl.when(jnp.logical_not(is_ctx))
    def _():
        attend(k_ref[...], v_ref[...])


def attention(q, k, v, n_ctx, tq=128):
    N, aw = q.shape
    group = aw // HEAD_DIM // KV_HEADS
    gw = group * HEAD_DIM
    assert N % tq == 0 and n_ctx % tq == 0
    return pl.pallas_call(
        functools.partial(_attn_kernel, n_ctx, n_ctx // tq, group),
        grid=(KV_HEADS, N // tq),
        in_specs=[pl.BlockSpec((tq, gw), lambda g, i: (i, g)),
                  pl.BlockSpec((N, HEAD_DIM), lambda g, i: (0, g)),
                  pl.BlockSpec((N, 2 * HEAD_DIM), lambda g, i: (0, g))],
        out_specs=pl.BlockSpec((tq, gw), lambda g, i: (i, g)),
        out_shape=jax.ShapeDtypeStruct((N, aw), BF16),
        compiler_params=_params(("parallel", "arbitrary")), name="attention",
    )(q, k, v)


def rope_tables(n_ctx, n_lat):
    n_rows = n_lat // GRID_W
    row = jnp.repeat(jnp.arange(n_rows), GRID_W).astype(F32)
    col = jnp.tile(jnp.arange(GRID_W), n_rows).astype(F32)
    axis_dim = HEAD_DIM // 2
    inv_freq = ROPE_THETA ** (-jnp.arange(0, axis_dim, 2, dtype=F32) / axis_dim)
    ang = jnp.concatenate([row[:, None] * inv_freq, col[:, None] * inv_freq], -1)
    cos, sin = jnp.cos(ang), jnp.sin(ang)
    cosf = jnp.concatenate([cos, cos], -1)
    sinf = jnp.concatenate([-sin, sin], -1)
    cosf = jnp.concatenate([jnp.ones((n_ctx, HEAD_DIM), F32), cosf], 0)
    sinf = jnp.concatenate([jnp.zeros((n_ctx, HEAD_DIM), F32), sinf], 0)
    return cosf, sinf


def _seg_sum64(x, e128):
    return _dot(x, e128, HI)


def _rwkv_feat_kernel(n_ctx_tiles, n_tiles, w_cols,
                      z_ref, zp_ref, zn_ref, mu_ref, kk_ref, ka_ref, w0_ref, a0_ref,
                      wup_ref, aup_ref, gup_ref, e_ref,
                      r_ref, v_ref, kap_ref, g_ref, lwf_ref, bf_ref, ktf_ref, lwb_ref, bb_ref, ktb_ref,
                      scr_ref):
    i = pl.program_id(0)
    tm = z_ref.shape[0]
    first = jnp.logical_or(i == 0, i == n_ctx_tiles)
    last = jnp.logical_or(i == n_ctx_tiles - 1, i == n_tiles - 1)
    scr_ref[8:8 + tm, :] = z_ref[...]
    scr_ref[0:8, :] = jnp.where(first, 0.0, zp_ref[...])
    scr_ref[8 + tm:16 + tm, :] = jnp.where(last, 0.0, zn_ref[...])
    z = z_ref[...]
    prev = scr_ref[7:7 + tm, :]
    nxt = scr_ref[9:9 + tm, :]
    zs = z + mu_ref[0:1, :] * (prev - z) + mu_ref[1:2, :] * (nxt - z)

    r = zs[:, 0:w_cols]
    k = zs[:, w_cols:2 * w_cols]
    v = zs[:, 2 * w_cols:3 * w_cols]
    lora = zs[:, 3 * w_cols:]
    wd = jnp.tanh(lora[:, 0:LANES])
    ad = lora[:, LANES:2 * LANES]
    gd = _sigmoid(lora[:, 2 * LANES:3 * LANES])
    r_ref[...] = r
    v_ref[...] = v
    g_ref[...] = _dot(gd, gup_ref[...], HI)
    kk = k * kk_ref[...]
    e128 = e_ref[...]
    kap = jnp.concatenate(
        [kk[:, s:s + LANES] * lax.rsqrt(jnp.maximum(_seg_sum64(kk[:, s:s + LANES] * kk[:, s:s + LANES], e128), 1e-24))
         for s in range(0, w_cols, LANES)], axis=1)
    kap_ref[...] = kap
    outs = ((lwf_ref, bf_ref, ktf_ref), (lwb_ref, bb_ref, ktb_ref))
    for d in range(2):
        lw_ref, b_ref, kt_ref = outs[d]
        lw_ref[...] = -W_DECAY_SCALE * _sigmoid(w0_ref[d:d + 1, :] + _dot(wd, wup_ref[d], HI))
        a = _sigmoid(a0_ref[d:d + 1, :] + _dot(ad, aup_ref[d], HI))
        kt_ref[...] = k * (1.0 + (a - 1.0) * ka_ref[...])
        b_ref[...] = a * kap


def _seg_ones(width=LANES, seg=RWKV_HEAD):
    i = np.arange(width)
    return jnp.asarray((i[:, None] // seg == i[None, :] // seg).astype(np.float32))


def rwkv_features(zr, n_ctx, mu, w0, w_up, a0, a_up, g_up, k_k, k_a, tm=256):
    N, zw = zr.shape
    wc = k_k.shape[0]
    assert zw == 3 * wc + 3 * LANES and W_RANK + W_RANK == LANES and A_RANK + A_RANK == LANES and G_RANK == LANES
    zeros = jnp.zeros((W_RANK, wc), F32)
    wup = jnp.stack([jnp.concatenate([w_up[0], zeros], 0), jnp.concatenate([zeros, w_up[1]], 0)], 0)
    aup = jnp.stack([jnp.concatenate([a_up[0], zeros], 0), jnp.concatenate([zeros, a_up[1]], 0)], 0)
    n_tiles = N // tm
    t8 = tm // 8
    row = lambda i: (i, 0)
    full = lambda i: (0, 0)
    out_sds = jax.ShapeDtypeStruct((N, wc), F32)
    return pl.pallas_call(
        functools.partial(_rwkv_feat_kernel, n_ctx // tm, n_tiles, wc),
        grid=(n_tiles,),
        in_specs=[pl.BlockSpec((tm, zw), row),
                  pl.BlockSpec((8, zw), lambda i: (jnp.maximum(i * t8 - 1, 0), 0)),
                  pl.BlockSpec((8, zw), lambda i: (jnp.minimum((i + 1) * t8, N // 8 - 1), 0)),
                  pl.BlockSpec((2, zw), full), pl.BlockSpec((1, wc), full), pl.BlockSpec((1, wc), full),
                  pl.BlockSpec((2, wc), full), pl.BlockSpec((2, wc), full),
                  pl.BlockSpec((2, LANES, wc), lambda i: (0, 0, 0)), pl.BlockSpec((2, LANES, wc), lambda i: (0, 0, 0)),
                  pl.BlockSpec((LANES, wc), full), pl.BlockSpec((LANES, LANES), full)],
        out_specs=[pl.BlockSpec((tm, wc), row)] * 10,
        out_shape=[out_sds] * 10,
        scratch_shapes=[pltpu.VMEM((tm + 16, zw), F32)],
        compiler_params=_params(("parallel",)), name="rwkv_features",
    )(zr, zr, zr, mu, k_k.reshape(1, wc), k_a.reshape(1, wc), w0, a0, wup, aup, g_up, _seg_ones())


def _rwkv_scan_kernel(rev, pairs, n_sub, r_ref, kap_ref, v_ref, lw_ref, b_ref, kt_ref, y_ref, s_ref):
    L = CHUNK
    L2 = 2 * L

    @pl.when(pl.program_id(1) == 0)
    def _():
        s_ref[...] = jnp.zeros(s_ref.shape, F32)

    t_i = lax.broadcasted_iota(jnp.int32, (L, L), 0)
    s_i = lax.broadcasted_iota(jnp.int32, (L, L), 1)
    m_incl64 = ((s_i >= t_i) if rev else (s_i <= t_i)).astype(F32)
    ri = lax.broadcasted_iota(jnp.int32, (L2, L2), 0)
    qi = lax.broadcasted_iota(jnp.int32, (L2, L2), 1)
    rt, qt = ri % L, qi % L
    same = (ri // L) == (qi // L)
    incl = jnp.logical_and(same, (qt >= rt) if rev else (qt <= rt))
    strict = jnp.logical_and(same, (qt > rt) if rev else (qt < rt))
    eye = (ri == qi).astype(F32)
    lane = lax.broadcasted_iota(jnp.int32, (L, LANES), 1)
    hm = [(lane < RWKV_HEAD).astype(F32), (lane >= RWKV_HEAD).astype(F32)]
    bd = ((lax.broadcasted_iota(jnp.int32, (LANES, LANES), 0) // RWKV_HEAD)
          == (lax.broadcasted_iota(jnp.int32, (LANES, LANES), 1) // RWKV_HEAD)).astype(F32)
    lvl_masks = []
    bsz = 1
    while bsz < L:
        grp = (ri // (2 * bsz)) == (qi // (2 * bsz))
        r_odd = (ri // bsz) % 2 == 1
        q_odd = (qi // bsz) % 2 == 1
        off = jnp.logical_and(jnp.logical_not(r_odd), q_odd) if rev else jnp.logical_and(r_odd, jnp.logical_not(q_odd))
        lvl_masks.append(jnp.logical_and(grp, off))
        bsz *= 2

    def stack2(x):
        return jnp.concatenate([x * hm[0], x * hm[1]], axis=0)

    def bmm(a, b):
        return _dot(a.astype(BF16), b.astype(BF16))

    m_incl_bf = m_incl64.astype(BF16)

    units = [(ci, p) for ci in range(n_sub) for p in range(pairs)]

    def blk(ref, unit):
        ci, p = unit
        cc = (n_sub - 1 - ci) if rev else ci
        return ref[cc * L:(cc + 1) * L, p * LANES:(p + 1) * LANES]

    def each(fn, *dicts):
        return {un: fn(*(d[un] for d in dicts)) for un in units}

    def split3(x):
        hi = x.astype(BF16)
        r1 = x - hi.astype(F32)
        mid = r1.astype(BF16)
        lo = (r1 - mid.astype(F32)).astype(BF16)
        return jnp.concatenate([hi, mid, lo], axis=1)

    lw = {un: blk(lw_ref, un) for un in units}
    c3 = each(lambda x: _dot(m_incl_bf, split3(x)), lw)
    cum = each(lambda c: c[:, 0:LANES] + c[:, LANES:2 * LANES] + c[:, 2 * LANES:], c3)
    e_incl = each(jnp.exp, cum)
    e_inv = each(lambda c: jnp.exp(-c), cum)
    p_tot = each(lambda e: e[0:1, :] if rev else e[L - 1:L, :], e_incl)
    xk = {un: stack2(blk(kap_ref, un) * jnp.exp(cum[un] - lw[un])).astype(BF16) for un in units}
    xr = {un: stack2(blk(r_ref, un) * e_incl[un]).astype(BF16) for un in units}
    yb = {un: stack2(blk(b_ref, un) * e_inv[un]).astype(BF16) for un in units}
    yk = {un: stack2(blk(kt_ref, un) * e_inv[un]).astype(BF16) for un in units}
    vs = {un: stack2(blk(v_ref, un)).astype(BF16) for un in units}
    amat = each(lambda a, b, c, d: _dot_nt(jnp.concatenate([a, b], axis=0), jnp.concatenate([c, d], axis=0)),
                xk, xr, yb, yk)
    a_ub = each(lambda a: jnp.where(strict, a[0:L2, 0:L2], 0.0), amat)
    a_uk = each(lambda a: jnp.where(strict, a[0:L2, L2:], 0.0).astype(BF16), amat)
    a_rb = each(lambda a: jnp.where(incl, a[L2:, 0:L2], 0.0).astype(BF16), amat)
    a_rk = each(lambda a: jnp.where(incl, a[L2:, L2:], 0.0).astype(BF16), amat)
    tinv = each(lambda a: eye - jnp.where(lvl_masks[0], a, 0.0), a_ub)
    for lm in lvl_masks[1:]:
        tb = each(lambda t: t.astype(BF16), tinv)
        x1 = each(lambda a, t: _dot(jnp.where(lm, a, 0.0).astype(BF16), t).astype(BF16), a_ub, tb)
        tinv = each(lambda t, tbf, x: t - _dot(tbf, x), tinv, tb, x1)
    tb = each(lambda t: t.astype(BF16), tinv)
    w1 = each(lambda t, x: _dot(t, x).astype(BF16), tb, xk)
    avs = each(lambda a, v: _dot(a, v).astype(BF16), a_uk, vs)
    y_ind = each(_dot, a_rk, vs)
    k2 = each(_dot_tn, vs, yk)
    w2 = each(_dot, tb, avs)

    state = [s_ref[p] for p in range(pairs)]
    for ci in range(n_sub):
        wx = [_dot_nt(jnp.concatenate([w1[(ci, p)], xr[(ci, p)]], axis=0), state[p].astype(BF16))
              for p in range(pairs)]
        ub = [(-(wx[p][0:L2] + w2[(ci, p)])).astype(BF16) for p in range(pairs)]
        upd = [_dot_tn(ub[p], yb[(ci, p)]) + k2[(ci, p)] for p in range(pairs)]
        state = [(state[p] + upd[p] * bd) * p_tot[(ci, p)] for p in range(pairs)]
        cc = (n_sub - 1 - ci) if rev else ci
        for p in range(pairs):
            ystk = wx[p][L2:] + _dot(a_rb[(ci, p)], ub[p]) + y_ind[(ci, p)]
            y_ref[cc * L:(cc + 1) * L, p * LANES:(p + 1) * LANES] = ystk[0:L] + ystk[L:]
    for p in range(pairs):
        s_ref[p] = state[p]


def rwkv_scan(rev, n_ctx, r, kap, v, lw, b, kt, pairs=4, n_sub=4):
    N, wc = r.shape
    rb = n_sub * CHUNK
    bw = pairs * LANES
    assert N % rb == 0 and n_ctx % rb == 0 and wc % bw == 0
    nb, nbc = N // rb, n_ctx // rb
    if rev:
        rmap = lambda h, i: (jnp.where(i < nbc, nbc - 1 - i, nb - 1 - (i - nbc)), h)
    else:
        rmap = lambda h, i: (i, h)
    spec = pl.BlockSpec((rb, bw), rmap)
    return pl.pallas_call(
        functools.partial(_rwkv_scan_kernel, rev, pairs, n_sub),
        grid=(wc // bw, nb),
        in_specs=[spec] * 6, out_specs=spec,
        out_shape=jax.ShapeDtypeStruct((N, wc), F32),
        scratch_shapes=[pltpu.VMEM((pairs, LANES, LANES), F32)],
        compiler_params=_params(("parallel", "arbitrary")), name="rwkv_scan_rev" if rev else "rwkv_scan_fwd",
    )(r, kap, v, lw, b, kt)


def _rwkv_out_kernel(yf_ref, yb_ref, r_ref, v_ref, ktf_ref, ktb_ref, g_ref, rk_ref, gn_ref, e_ref, o_ref):
    e128 = e_ref[...]
    inv = 1.0 / RWKV_HEAD
    for s in range(0, o_ref.shape[1], LANES):
        sl = slice(s, s + LANES)
        y = yf_ref[:, sl] + yb_ref[:, sl]
        mu = _seg_sum64(y, e128) * inv
        yc = y - mu
        var = _seg_sum64(yc * yc, e128) * inv
        yn = yc * lax.rsqrt(var + GN_EPS) * gn_ref[0:1, sl] + gn_ref[1:2, sl]
        rk = r_ref[:, sl] * rk_ref[0:1, sl]
        bonus = _seg_sum64(rk * (ktf_ref[:, sl] + ktb_ref[:, sl]), e128) * v_ref[:, sl]
        o_ref[:, sl] = ((yn + bonus) * g_ref[:, sl]).astype(o_ref.dtype)


def rwkv_output(y_f, y_b, r, v, kt_f, kt_b, g, r_k, lnx_gain, lnx_bias, tm=256):
    N, wc = r.shape
    row = lambda i: (i, 0)
    full = lambda i: (0, 0)
    gn = jnp.stack([lnx_gain, lnx_bias], 0)
    return pl.pallas_call(
        _rwkv_out_kernel, grid=(N // tm,),
        in_specs=[pl.BlockSpec((tm, wc), row)] * 7 + [pl.BlockSpec((1, wc), full), pl.BlockSpec((2, wc), full),
                                                      pl.BlockSpec((LANES, LANES), full)],
        out_specs=pl.BlockSpec((tm, wc), row),
        out_shape=jax.ShapeDtypeStruct((N, wc), BF16),
        compiler_params=_params(("parallel",)), name="rwkv_output",
    )(y_f, y_b, r, v, kt_f, kt_b, g, r_k.reshape(1, wc), gn, _seg_ones())


def _angles(i, j, period):
    return (2.0 * math.pi / period) * ((i * j) % period).astype(F32)


def _chan_dft(gdim, scale):
    i = lax.iota(jnp.int32, gdim)
    ang = _angles(i[:, None], i[None, :], gdim)
    return jnp.concatenate([jnp.cos(ang), -jnp.sin(ang)], 1) * scale


def _fnet1_kernel(groups, x_ref, cs_ref, m_ref, y_ref):
    r1 = x_ref.shape[0]
    gdim = x_ref.shape[1] // groups
    for g in range(groups):
        sl = slice(g * gdim, (g + 1) * gdim)
        ab = _dot(x_ref[:, sl], cs_ref[...], HI)
        st = jnp.concatenate([ab[:, :gdim], ab[:, gdim:]], axis=0)
        y = _dot(m_ref[0], st, HI)
        y_ref[0, 0, :, sl] = y[:r1]
        y_ref[1, 0, :, sl] = y[r1:]


def _fnet3_kernel(cs_ref, y_ref, o_ref):
    o_ref[...] = _dot(cs_ref[...], y_ref[...], HI)


def fnet_latent(zf, groups=FOURIER_GROUPS):
    T, fw = zf.shape
    gdim = fw // groups
    r2 = GRID_W
    r1 = T // r2
    cs = _chan_dft(gdim, 1.0 / math.sqrt(T * gdim))
    t1p = lax.iota(jnp.int32, r1)[None, :, None]
    t1 = lax.iota(jnp.int32, r1)[None, None, :]
    t2 = lax.iota(jnp.int32, r2)[:, None, None]
    th = _angles(t1p, r2 * t1 + t2, T)
    gr, gi = jnp.cos(th), -jnp.sin(th)
    mt = jnp.concatenate([jnp.concatenate([gr, -gi], 2), jnp.concatenate([gi, gr], 2)], 1)
    y = pl.pallas_call(
        functools.partial(_fnet1_kernel, groups),
        grid=(r2,),
        in_specs=[pl.BlockSpec((r1, fw), lambda j: (0, j)),
                  pl.BlockSpec((gdim, 2 * gdim), lambda j: (0, 0)),
                  pl.BlockSpec((1, 2 * r1, 2 * r1), lambda j: (j, 0, 0))],
        out_specs=pl.BlockSpec((2, 1, r1, fw), lambda j: (0, j, 0, 0)),
        out_shape=jax.ShapeDtypeStruct((2, r2, r1, fw), F32),
        compiler_params=_params(("parallel",)), name="fnet_rows",
    )(zf.reshape(r1, r2 * fw), cs, mt)
    i2 = lax.iota(jnp.int32, r2)
    ph = _angles(i2[:, None], i2[None, :], r2)
    cs2 = jnp.concatenate([jnp.cos(ph), jnp.sin(ph)], 1)
    tc = 4 * fw if r1 % 4 == 0 else fw
    o = pl.pallas_call(
        _fnet3_kernel, grid=(r1 * fw // tc,),
        in_specs=[pl.BlockSpec((r2, 2 * r2), lambda j: (0, 0)), pl.BlockSpec((2 * r2, tc), lambda j: (0, j))],
        out_specs=pl.BlockSpec((r2, tc), lambda j: (0, j)),
        out_shape=jax.ShapeDtypeStruct((r2, r1 * fw), F32),
        compiler_params=_params(("parallel",)), name="fnet_cols",
    )(cs2, y.reshape(2 * r2, r1 * fw))
    return o.reshape(T, fw)


def _fnet_direct_kernel(x_ref, cs_ref, ct_ref, o_ref):
    gdim = x_ref.shape[1]
    ab = _dot(x_ref[...], cs_ref[...], HI)
    st = jnp.concatenate([ab[:, :gdim], ab[:, gdim:]], axis=0)
    o_ref[...] = _dot(ct_ref[...], st, HI)


def fnet_direct(zf, groups=FOURIER_GROUPS):
    C, fw = zf.shape
    gdim = fw // groups
    cs = _chan_dft(gdim, 1.0 / math.sqrt(C * gdim))
    i = lax.iota(jnp.int32, C)
    ang = _angles(i[:, None], i[None, :], C)
    ct = jnp.concatenate([jnp.cos(ang), jnp.sin(ang)], 1)
    return pl.pallas_call(
        _fnet_direct_kernel, grid=(groups,),
        in_specs=[pl.BlockSpec((C, gdim), lambda g: (0, g)), pl.BlockSpec((gdim, 2 * gdim), lambda g: (0, 0)),
                  pl.BlockSpec((C, 2 * C), lambda g: (0, 0))],
        out_specs=pl.BlockSpec((C, gdim), lambda g: (0, g)),
        out_shape=jax.ShapeDtypeStruct((C, fw), F32),
        compiler_params=_params(("parallel",)), name="fnet_direct",
    )(zf, cs, ct)


GATE_LANE0 = N_GROUPS


def _gates_kernel(lg_ref, id_ref, gv_ref):
    lg = lg_ref[...]
    lane = lax.broadcasted_iota(jnp.int32, lg.shape, 1)
    big = jnp.int32(LANES)
    neg = -jnp.inf
    is_g = lane < N_GROUPS
    gl = jnp.where(is_g, lg, neg)
    mg = jnp.max(gl, axis=-1, keepdims=True)
    p_group = 1.0 / jnp.sum(jnp.exp(gl - mg), axis=-1, keepdims=True)
    gsel = jnp.min(jnp.where(gl == mg, lane, big), axis=-1, keepdims=True)
    e_idx = lane - GATE_LANE0
    in_grp = jnp.logical_and(jnp.logical_and(e_idx >= 0, e_idx < N_EXPERTS), e_idx // EXPERTS_PER_GROUP == gsel)
    le = jnp.where(in_grp, lg, neg)
    m1 = jnp.max(le, axis=-1, keepdims=True)
    l1 = jnp.min(jnp.where(le == m1, lane, big), axis=-1, keepdims=True)
    le2 = jnp.where(lane == l1, neg, le)
    m2 = jnp.max(le2, axis=-1, keepdims=True)
    l2 = jnp.min(jnp.where(le2 == m2, lane, big), axis=-1, keepdims=True)
    e2 = jnp.exp(m2 - m1)
    inv = p_group / (1.0 + e2)
    id_ref[...] = jnp.where(lane == 0, l1 - GATE_LANE0, jnp.where(lane == 1, l2 - GATE_LANE0, 0))
    gv_ref[...] = jnp.where(lane == 0, inv, jnp.where(lane == 1, e2 * inv, 0.0))


def moe_gates(logits, tm=256):
    N = logits.shape[0]
    spec = pl.BlockSpec((tm, LANES), lambda i: (i, 0))
    return pl.pallas_call(
        _gates_kernel, grid=(N // tm,),
        in_specs=[spec], out_specs=[spec, spec],
        out_shape=[jax.ShapeDtypeStruct((N, LANES), jnp.int32), jax.ShapeDtypeStruct((N, LANES), F32)],
        compiler_params=_params(("parallel",)), name="moe_gates",
    )(logits)


def moe_dispatch(expert_ids, ts):
    N = expert_ids.shape[0]
    flat = expert_ids.reshape(-1)
    onehot = (flat[:, None] == jnp.arange(N_EXPERTS, dtype=jnp.int32)[None, :]).astype(jnp.int32)
    csum = jnp.cumsum(onehot, axis=0)
    rank = jnp.take_along_axis(csum, flat[:, None], axis=1)[:, 0] - 1
    counts = csum[-1]
    tiles_per = (counts + ts - 1) // ts
    tiles_end = jnp.cumsum(tiles_per)
    pad_start = (tiles_end - tiles_per) * ts
    slot = pad_start[flat] + rank
    n_tiles = -(-2 * N // ts) + N_EXPERTS
    slot_token = jnp.zeros((n_tiles * ts,), jnp.int32).at[slot].set(jnp.arange(2 * N, dtype=jnp.int32) // 2)
    n_used = tiles_end[-1]
    tile_idx = jnp.minimum(jnp.arange(n_tiles, dtype=jnp.int32), n_used - 1)
    tile_expert = jnp.sum((tile_idx[:, None] >= tiles_end[None, :]).astype(jnp.int32), axis=1)
    return slot_token, slot.reshape(N, 2), tile_expert.astype(jnp.int32), n_used.reshape(1).astype(jnp.int32)


def _moe_group_kernel(ts, te_ref, nu_ref, st_ref, u_hbm, w1_ref, w3_ref, w2_ref, o_ref, ubuf, ub16, sem):
    i = pl.program_id(0)
    j = pl.program_id(1)
    used = nu_ref[0]

    def gather(tile, b):
        def row(r, carry):
            tok = st_ref[tile * ts + r]
            pltpu.make_async_copy(u_hbm.at[pl.ds(tok, 1), :], ubuf.at[b, pl.ds(r, 1), :], sem.at[b]).start()
            return carry
        lax.fori_loop(0, ts, row, 0)

    @pl.when(jnp.logical_and(j == 0, i < used))
    def _():
        @pl.when(i == 0)
        def _():
            gather(0, 0)

        @pl.when(i + 1 < used)
        def _():
            gather(i + 1, (i + 1) % 2)

        b = i % 2
        pltpu.make_async_copy(u_hbm.at[pl.ds(0, ts), :], ubuf.at[b], sem.at[b]).wait()
        ub16[...] = ubuf[b].astype(BF16)

    @pl.when(i < used)
    def _():
        u = ub16[...]
        a = _dot(u, w1_ref[0])
        g = _dot(u, w3_ref[0])
        h = ((a * _sigmoid(a)) * g).astype(BF16)
        y = _dot(h, w2_ref[0])

        @pl.when(j == 0)
        def _():
            o_ref[...] = y

        @pl.when(j > 0)
        def _():
            o_ref[...] += y

    @pl.when(jnp.logical_and(i >= used, j == 0))
    def _():
        o_ref[...] = jnp.zeros(o_ref.shape, o_ref.dtype)


def moe_grouped(u, slot_token, tile_expert, n_used, w1, w3, w2, ts, tc=256):
    N, D = u.shape
    E, _, De = w1.shape
    n_tiles = tile_expert.shape[0]
    grid_spec = pltpu.PrefetchScalarGridSpec(
        num_scalar_prefetch=3,
        grid=(n_tiles, De // tc),
        in_specs=[pl.BlockSpec(memory_space=pl.ANY),
                  pl.BlockSpec((1, D, tc), lambda i, j, te, nu, st: (te[i], 0, j)),
                  pl.BlockSpec((1, D, tc), lambda i, j, te, nu, st: (te[i], 0, j)),
                  pl.BlockSpec((1, tc, D), lambda i, j, te, nu, st: (te[i], j, 0))],
        out_specs=pl.BlockSpec((ts, D), lambda i, j, te, nu, st: (i, 0)),
        scratch_shapes=[pltpu.VMEM((2, ts, D), F32), pltpu.VMEM((ts, D), BF16), pltpu.SemaphoreType.DMA((2,))],
    )
    return pl.pallas_call(
        functools.partial(_moe_group_kernel, ts), grid_spec=grid_spec,
        out_shape=jax.ShapeDtypeStruct((n_tiles * ts, D), F32),
        compiler_params=_params(("arbitrary", "arbitrary")), name="moe_grouped",
    )(tile_expert, n_used, slot_token, u, w1, w3, w2)


def _moe_combine_kernel(alpha, n_ctx_tiles, sa_ref, sb_ref, o_hbm, x_ref, gv_ref, g_ref, gb_ref, out_ref,
                        abuf, bbuf, sem):
    i = pl.program_id(0)
    tm = x_ref.shape[0]

    def gather(tile, b):
        def row(r, carry):
            t = tile * tm + r
            pltpu.make_async_copy(o_hbm.at[pl.ds(sa_ref[t], 1), :], abuf.at[b, pl.ds(r, 1), :], sem.at[0, b]).start()
            pltpu.make_async_copy(o_hbm.at[pl.ds(sb_ref[t], 1), :], bbuf.at[b, pl.ds(r, 1), :], sem.at[1, b]).start()
            return carry
        lax.fori_loop(0, tm, row, 0)

    @pl.when(i == 0)
    def _():
        gather(0, 0)

    @pl.when(i + 1 < pl.num_programs(0))
    def _():
        gather(i + 1, (i + 1) % 2)

    b = i % 2
    pltpu.make_async_copy(o_hbm.at[pl.ds(0, tm), :], abuf.at[b], sem.at[0, b]).wait()
    pltpu.make_async_copy(o_hbm.at[pl.ds(0, tm), :], bbuf.at[b], sem.at[1, b]).wait()
    gv = gv_ref[...]
    f = gv[:, 0:1] * abuf[b] + gv[:, 1:2] * bbuf[b]
    g = jnp.where(i < n_ctx_tiles, g_ref[1:2, :], g_ref[0:1, :])
    y = alpha * x_ref[...] + g * f
    out_ref[...] = _ln_rows(y, LN_EPS) * gb_ref[0:1, :] + gb_ref[1:2, :]


def moe_combine_ln(xx, o, slots, gate_vals, gates2, gain_bias, n_ctx, alpha, tm=256):
    N, D = xx.shape
    grid_spec = pltpu.PrefetchScalarGridSpec(
        num_scalar_prefetch=2,
        grid=(N // tm,),
        in_specs=[pl.BlockSpec(memory_space=pl.ANY),
                  pl.BlockSpec((tm, D), lambda i, sa, sb: (i, 0)),
                  pl.BlockSpec((tm, LANES), lambda i, sa, sb: (i, 0)),
                  pl.BlockSpec((2, D), lambda i, sa, sb: (0, 0)), pl.BlockSpec((2, D), lambda i, sa, sb: (0, 0))],
        out_specs=pl.BlockSpec((tm, D), lambda i, sa, sb: (i, 0)),
        scratch_shapes=[pltpu.VMEM((2, tm, D), F32), pltpu.VMEM((2, tm, D), F32), pltpu.SemaphoreType.DMA((2, 2))],
    )
    return pl.pallas_call(
        functools.partial(_moe_combine_kernel, alpha, n_ctx // tm), grid_spec=grid_spec,
        out_shape=jax.ShapeDtypeStruct((N, D), F32),
        compiler_params=_params(("arbitrary",)), name="moe_combine_ln",
    )(slots[:, 0], slots[:, 1], o, xx, gate_vals, gates2, gain_bias)


MOE_SLOT_TILE = 512

def _pick_tile(n, prefer):
    for t in prefer:
        if n % t == 0:
            return t
    raise ValueError(f"no tile for {n}")


def kernel(x, c, ctx, c_ctx, w_mod, b_mod, w_in, q_gain, k_gain, rwkv_mu, w0, w_up, a0, a_up, g_up, k_k, k_a, r_k, lnx_gain, lnx_bias, w_out, ln1_gain, ln1_bias, ln2_gain, ln2_bias, router_group_w, router_group_b, router_expert_w, router_expert_b, w1, w3, w2):
    B, T, D = x.shape
    C = ctx.shape[1]
    assert B == 1
    depth = w_mod.shape[0]
    alpha = (2 * depth) ** 0.25
    N = C + T
    aw = D // 2
    kvw = KV_HEADS * HEAD_DIM
    rw = D // 4
    fw = D // 4
    rcols = 3 * rw + 2 * W_RANK + 2 * A_RANK + G_RANK
    o_zr = aw + 2 * kvw
    o_zf = o_zr + rcols
    tm_big = _pick_tile(N, (768, 384, 256))

    xx = jnp.concatenate([ctx[0], x[0]], 0)
    mods = mod_vectors(c, c_ctx, w_mod, b_mod).reshape(depth, 8, 6, D)
    cosf, sinf = rope_tables(C, T)

    for l in range(depth):
        last = l == depth - 1
        mv = mods[l]
        vec = lambda i: jnp.stack([mv[0, i], mv[1, i]], 0)
        ss1 = jnp.stack([mv[0, 0], mv[0, 1], mv[1, 0], mv[1, 1]], 0)
        ss2 = jnp.stack([mv[0, 3], mv[0, 4], mv[1, 3], mv[1, 4]], 0)

        u = ln_modulate(xx, ss1, C)
        wl = w_in[l].astype(BF16)
        zq = matmul(u, wl[:, :o_zr], F32, tm_big, _pick_tile(o_zr, (768, 512, 256, 128)))
        zr = matmul(u, wl[:, o_zr:o_zf], F32, tm_big, _pick_tile(rcols, (1152, 384, 128)))
        zf = matmul(u, wl[:, o_zf:], F32, tm_big, _pick_tile(fw, (512, 256, 128)))
        qh, kh, vh = qkv_prep(zq, cosf, sinf, q_gain[l], k_gain[l], aw)
        attn = attention(qh, kh, vh, C)
        r, v, kap, g, lwf, bf, ktf, lwb, bb, ktb = rwkv_features(
            zr, C, rwkv_mu[l], w0[l], w_up[l], a0[l], a_up[l], g_up[l], k_k[l], k_a[l])
        y_f = rwkv_scan(False, C, r, kap, v, lwf, bf, ktf)
        y_b = rwkv_scan(True, C, r, kap, v, lwb, bb, ktb)
        rwkv = rwkv_output(y_f, y_b, r, v, ktf, ktb, g, r_k[l].reshape(-1), lnx_gain[l], lnx_bias[l])
        fl = fnet_latent(zf[C:])
        fc = fnet_direct(zf[:C]) if not last else jnp.zeros((C, fw), F32)
        fn = jnp.concatenate([fc, fl], 0).astype(BF16)
        cat = jnp.concatenate([attn, rwkv, fn], 1)
        m = matmul(cat, w_out[l].astype(BF16), F32, tm_big, _pick_tile(D, (512, 256, 128)))
        xx = resid_ln(xx, m, vec(2), jnp.stack([ln1_gain[l], ln1_bias[l]], 0), C, alpha)

        wr = jnp.concatenate([router_group_w[l], router_expert_w[l],
                              jnp.zeros((D, LANES - N_GROUPS - N_EXPERTS), F32)], 1)
        br = jnp.concatenate([router_group_b[l], router_expert_b[l],
                              jnp.zeros((LANES - N_GROUPS - N_EXPERTS,), F32)], 0).reshape(1, LANES)
        u2, logits = ln_modulate(xx, ss2, C, router=(wr, br), out_dtype=F32)
        expert_ids, gate_vals = moe_gates(logits)
        slot_token, slots, tile_expert, n_used = moe_dispatch(expert_ids[:, :2], MOE_SLOT_TILE)
        o = moe_grouped(u2, slot_token, tile_expert, n_used,
                        w1[l].astype(BF16), w3[l].astype(BF16), w2[l].astype(BF16), MOE_SLOT_TILE)
        xx = moe_combine_ln(xx, o, slots, gate_vals, vec(5), jnp.stack([ln2_gain[l], ln2_bias[l]], 0), C, alpha)
    return xx[C:][None]
```

```python
import functools
import math

import numpy as np
import jax
import jax.numpy as jnp
from jax import lax
from jax.experimental import pallas as pl
from jax.experimental.pallas import tpu as pltpu

F32 = jnp.float32
BF16 = jnp.bfloat16

GRID_W = 64
HEAD_DIM = 128
KV_HEADS = 4
RWKV_HEAD = 64
W_RANK = 64
A_RANK = 64
G_RANK = 128
FOURIER_GROUPS = 4
N_GROUPS = 4
EXPERTS_PER_GROUP = 4
N_EXPERTS = N_GROUPS * EXPERTS_PER_GROUP
ROPE_THETA = 10000.0
W_DECAY_SCALE = math.exp(-0.5)
GN_EPS = 64e-5
LN_EPS = 1e-6
LANES = 128
CHUNK = 64

V7X_VMEM_LIMIT_MB = 56


def _params(sem, vmem_mb=V7X_VMEM_LIMIT_MB):
    return pltpu.CompilerParams(dimension_semantics=sem, vmem_limit_bytes=vmem_mb * 1024 * 1024)


def _dot(a, b, prec=None):
    return jnp.dot(a, b, preferred_element_type=F32, precision=prec)


def _dot_nt(a, b, prec=None):
    return lax.dot_general(a, b, (((1,), (1,)), ((), ())), preferred_element_type=F32, precision=prec)


def _dot_tn(a, b, prec=None):
    return lax.dot_general(a, b, (((0,), (0,)), ((), ())), preferred_element_type=F32, precision=prec)


def _sigmoid(x):
    return 1.0 / (1.0 + jnp.exp(-x))


def _split2(x):
    hi = x.astype(BF16)
    return hi, (x - hi.astype(F32)).astype(BF16)


def _dot3(a, b):
    a_hi, a_lo = _split2(a)
    b_hi, b_lo = _split2(b)
    return _dot(a_hi, b_hi) + (_dot(a_hi, b_lo) + _dot(a_lo, b_hi))


def _dot_exact_rhs(x, e):
    hi = x.astype(BF16)
    r1 = x - hi.astype(F32)
    mid = r1.astype(BF16)
    lo = (r1 - mid.astype(F32)).astype(BF16)
    return _dot(hi, e) + (_dot(mid, e) + _dot(lo, e))


def _mod_kernel(a_ref, w_ref, b_ref, o_ref):
    tn = o_ref.shape[2]
    rows = []
    for r in range(2):
        cols = []
        for j in range(tn // LANES):
            prod = w_ref[0, :, j * LANES:(j + 1) * LANES] * a_ref[r]
            cols.append(jnp.sum(prod, axis=0, keepdims=True))
        rows.append(jnp.concatenate(cols, axis=1) + b_ref[0])
    rows.append(jnp.zeros((6, tn), F32))
    o_ref[0] = jnp.concatenate(rows, axis=0)


def mod_vectors(c, c_ctx, w_mod, b_mod):
    L, D, D6 = w_mod.shape
    acts = jnp.stack([jax.nn.silu(c[0]), jax.nn.silu(c_ctx)], 0)
    a_b = jnp.broadcast_to(acts[:, :, None], (2, D, LANES))
    tn = 512
    return pl.pallas_call(
        _mod_kernel,
        grid=(L, D6 // tn),
        in_specs=[pl.BlockSpec((2, D, LANES), lambda l, j: (0, 0, 0)),
                  pl.BlockSpec((1, D, tn), lambda l, j: (l, 0, j)),
                  pl.BlockSpec((1, 1, tn), lambda l, j: (l, 0, j))],
        out_specs=pl.BlockSpec((1, 8, tn), lambda l, j: (l, 0, j)),
        out_shape=jax.ShapeDtypeStruct((L, 8, D6), F32),
        compiler_params=_params(("parallel", "parallel")),
        name="mod_vectors",
    )(a_b, w_mod, b_mod.reshape(L, 1, D6))


def _ln_rows(x, eps):
    mu = jnp.mean(x, axis=-1, keepdims=True)
    xc = x - mu
    var = jnp.mean(xc * xc, axis=-1, keepdims=True)
    return xc * lax.rsqrt(var + eps)


def _ln_mod_kernel(n_ctx_tiles, with_router, x_ref, ss_ref, *rest):
    if with_router:
        wr_ref, br_ref, u_ref, lg_ref = rest
    else:
        (u_ref,) = rest
    is_ctx = pl.program_id(0) < n_ctx_tiles
    sh = jnp.where(is_ctx, ss_ref[2:3, :], ss_ref[0:1, :])
    sc = jnp.where(is_ctx, ss_ref[3:4, :], ss_ref[1:2, :])
    u = _ln_rows(x_ref[...], LN_EPS) * (1.0 + sc) + sh
    u_ref[...] = u.astype(u_ref.dtype)
    if with_router:
        lg_ref[...] = _dot3(u, wr_ref[...]) + br_ref[...]


def ln_modulate(xx, ss, n_ctx, router=None, out_dtype=BF16, tm=256):
    N, D = xx.shape
    assert N % tm == 0 and n_ctx % tm == 0
    in_specs = [pl.BlockSpec((tm, D), lambda i: (i, 0)), pl.BlockSpec((4, D), lambda i: (0, 0))]
    out_specs = [pl.BlockSpec((tm, D), lambda i: (i, 0))]
    out_shape = [jax.ShapeDtypeStruct((N, D), out_dtype)]
    args = [xx, ss]
    if router is not None:
        in_specs += [pl.BlockSpec((D, LANES), lambda i: (0, 0)), pl.BlockSpec((1, LANES), lambda i: (0, 0))]
        out_specs.append(pl.BlockSpec((tm, LANES), lambda i: (i, 0)))
        out_shape.append(jax.ShapeDtypeStruct((N, LANES), F32))
        args += list(router)
    res = pl.pallas_call(
        functools.partial(_ln_mod_kernel, n_ctx // tm, router is not None),
        grid=(N // tm,), in_specs=in_specs, out_specs=out_specs, out_shape=out_shape,
        compiler_params=_params(("parallel",)), name="ln_modulate",
    )(*args)
    return res if router is not None else res[0]


def _resid_ln_kernel(alpha, n_ctx_tiles, x_ref, m_ref, g_ref, gb_ref, o_ref):
    is_ctx = pl.program_id(0) < n_ctx_tiles
    g = jnp.where(is_ctx, g_ref[1:2, :], g_ref[0:1, :])
    y = alpha * x_ref[...] + g * m_ref[...].astype(F32)
    o_ref[...] = _ln_rows(y, LN_EPS) * gb_ref[0:1, :] + gb_ref[1:2, :]


def resid_ln(xx, m, gates2, gain_bias, n_ctx, alpha, tm=256):
    N, D = xx.shape
    return pl.pallas_call(
        functools.partial(_resid_ln_kernel, alpha, n_ctx // tm),
        grid=(N // tm,),
        in_specs=[pl.BlockSpec((tm, D), lambda i: (i, 0)), pl.BlockSpec((tm, D), lambda i: (i, 0)),
                  pl.BlockSpec((2, D), lambda i: (0, 0)), pl.BlockSpec((2, D), lambda i: (0, 0))],
        out_specs=pl.BlockSpec((tm, D), lambda i: (i, 0)),
        out_shape=jax.ShapeDtypeStruct((N, D), F32),
        compiler_params=_params(("parallel",)), name="resid_ln",
    )(xx, m, gates2, gain_bias)


def _mm_kernel(a_ref, b_ref, o_ref):
    o_ref[...] = _dot(a_ref[...], b_ref[0]).astype(o_ref.dtype)


def matmul(a, b, layer, out_dtype, tm, tn):
    M, K = a.shape
    _, _, Nn = b.shape
    assert M % tm == 0 and Nn % tn == 0
    return pl.pallas_call(
        _mm_kernel, grid=(M // tm, Nn // tn),
        in_specs=[pl.BlockSpec((tm, K), lambda i, j: (i, 0)), pl.BlockSpec((1, K, tn), lambda i, j: (layer, 0, j))],
        out_specs=pl.BlockSpec((tm, tn), lambda i, j: (i, j)),
        out_shape=jax.ShapeDtypeStruct((M, Nn), out_dtype),
        compiler_params=_params(("parallel", "arbitrary")), name="matmul",
    )(a, b)


def _qkv_prep_kernel(n_q_heads, scale, z_ref, cos_ref, sin_ref, qg_ref, kg_ref, q_ref, k_ref, v_ref):
    cosf = cos_ref[...]
    sinf = sin_ref[...]

    def norm_rope(t, gain):
        t = t * lax.rsqrt(jnp.mean(t * t, axis=-1, keepdims=True) + LN_EPS) * gain
        return t * cosf + pltpu.roll(t, HEAD_DIM // 2, 1) * sinf

    for h in range(n_q_heads):
        sl = slice(h * HEAD_DIM, (h + 1) * HEAD_DIM)
        q_ref[:, sl] = (norm_rope(z_ref[:, sl], qg_ref[...]) * scale).astype(q_ref.dtype)
    qw = n_q_heads * HEAD_DIM
    for h in range(KV_HEADS):
        sl = slice(h * HEAD_DIM, (h + 1) * HEAD_DIM)
        zs = slice(qw + h * HEAD_DIM, qw + (h + 1) * HEAD_DIM)
        k_ref[:, sl] = norm_rope(z_ref[:, zs], kg_ref[...]).astype(k_ref.dtype)
    kvw = KV_HEADS * HEAD_DIM
    ones = jnp.ones((z_ref.shape[0], HEAD_DIM), v_ref.dtype)
    for h in range(KV_HEADS):
        zs = slice(qw + kvw + h * HEAD_DIM, qw + kvw + (h + 1) * HEAD_DIM)
        v_ref[:, 2 * h * HEAD_DIM:(2 * h + 1) * HEAD_DIM] = z_ref[:, zs].astype(v_ref.dtype)
        v_ref[:, (2 * h + 1) * HEAD_DIM:(2 * h + 2) * HEAD_DIM] = ones


def qkv_prep(z, cosf, sinf, q_gain, k_gain, attn_width, tm=256):
    N = z.shape[0]
    n_q = attn_width // HEAD_DIM
    kvw = KV_HEADS * HEAD_DIM
    zw = attn_width + 2 * kvw
    return pl.pallas_call(
        functools.partial(_qkv_prep_kernel, n_q, HEAD_DIM ** -0.5 * math.log2(math.e)),
        grid=(N // tm,),
        in_specs=[pl.BlockSpec((tm, zw), lambda i: (i, 0)),
                  pl.BlockSpec((tm, HEAD_DIM), lambda i: (i, 0)), pl.BlockSpec((tm, HEAD_DIM), lambda i: (i, 0)),
                  pl.BlockSpec((1, HEAD_DIM), lambda i: (0, 0)), pl.BlockSpec((1, HEAD_DIM), lambda i: (0, 0))],
        out_specs=[pl.BlockSpec((tm, attn_width), lambda i: (i, 0)),
                   pl.BlockSpec((tm, kvw), lambda i: (i, 0)), pl.BlockSpec((tm, 2 * kvw), lambda i: (i, 0))],
        out_shape=[jax.ShapeDtypeStruct((N, attn_width), BF16),
                   jax.ShapeDtypeStruct((N, kvw), BF16), jax.ShapeDtypeStruct((N, 2 * kvw), BF16)],
        compiler_params=_params(("parallel",)), name="qkv_prep",
    )(z, cosf, sinf, q_gain.reshape(1, HEAD_DIM), k_gain.reshape(1, HEAD_DIM))


def _attn_kernel(n_ctx, n_ctx_tiles, group, q_ref, k_ref, v_ref, o_ref):
    tq = q_ref.shape[0]
    hh = group // 2
    halves = [jnp.concatenate([q_ref[:, h * HEAD_DIM:(h + 1) * HEAD_DIM] for h in range(i * hh, (i + 1) * hh)], axis=0)
              for i in range(2)]

    def attend(k, v):
        s = [_dot_nt(qh, k) for qh in halves]
        p = [jnp.exp2(x - jnp.max(x, axis=-1, keepdims=True)).astype(v.dtype) for x in s]
        acc = [_dot(x, v) for x in p]
        for h in range(group):
            a = acc[h // hh][(h % hh) * tq:(h % hh + 1) * tq]
            o_ref[:, h * HEAD_DIM:(h + 1) * HEAD_DIM] = (a[:, :HEAD_DIM] / a[:, HEAD_DIM:]).astype(o_ref.dtype)

    is_ctx = pl.program_id(1) < n_ctx_tiles

    @pl.when(is_ctx)
    def _():
        attend(k_ref[0:n_ctx, :], v_ref[0:n_ctx, :])

    @pl.when(jnp.logical_not(is_ctx))
    def _():
        attend(k_ref[...], v_ref[...])


def attention(q, k, v, n_ctx, tq=128):
    N, aw = q.shape
    group = aw // HEAD_DIM // KV_HEADS
    gw = group * HEAD_DIM
    assert N % tq == 0 and n_ctx % tq == 0
    return pl.pallas_call(
        functools.partial(_attn_kernel, n_ctx, n_ctx // tq, group),
        grid=(KV_HEADS, N // tq),
        in_specs=[pl.BlockSpec((tq, gw), lambda g, i: (i, g)),
                  pl.BlockSpec((N, HEAD_DIM), lambda g, i: (0, g)),
                  pl.BlockSpec((N, 2 * HEAD_DIM), lambda g, i: (0, g))],
        out_specs=pl.BlockSpec((tq, gw), lambda g, i: (i, g)),
        out_shape=jax.ShapeDtypeStruct((N, aw), BF16),
        compiler_params=_params(("parallel", "arbitrary")), name="attention",
    )(q, k, v)


def rope_tables(n_ctx, n_lat):
    n_rows = n_lat // GRID_W
    row = jnp.repeat(jnp.arange(n_rows), GRID_W).astype(F32)
    col = jnp.tile(jnp.arange(GRID_W), n_rows).astype(F32)
    axis_dim = HEAD_DIM // 2
    inv_freq = ROPE_THETA ** (-jnp.arange(0, axis_dim, 2, dtype=F32) / axis_dim)
    ang = jnp.concatenate([row[:, None] * inv_freq, col[:, None] * inv_freq], -1)
    cos, sin = jnp.cos(ang), jnp.sin(ang)
    cosf = jnp.concatenate([cos, cos], -1)
    sinf = jnp.concatenate([-sin, sin], -1)
    cosf = jnp.concatenate([jnp.ones((n_ctx, HEAD_DIM), F32), cosf], 0)
    sinf = jnp.concatenate([jnp.zeros((n_ctx, HEAD_DIM), F32), sinf], 0)
    return cosf, sinf


def _seg_sum64(x, e128):
    return _dot_exact_rhs(x, e128)


def _rwkv_feat_kernel(n_ctx_tiles, n_tiles, w_cols,
                      z_ref, zp_ref, zn_ref, mu_ref, kk_ref, ka_ref, w0_ref, a0_ref,
                      wup_ref, aup_ref, gup_ref, e_ref,
                      r_ref, v_ref, kap_ref, g_ref, lwf_ref, bf_ref, ktf_ref, lwb_ref, bb_ref, ktb_ref,
                      scr_ref):
    i = pl.program_id(0)
    tm = z_ref.shape[0]
    first = jnp.logical_or(i == 0, i == n_ctx_tiles)
    last = jnp.logical_or(i == n_ctx_tiles - 1, i == n_tiles - 1)
    scr_ref[8:8 + tm, :] = z_ref[...]
    scr_ref[0:8, :] = jnp.where(first, 0.0, zp_ref[...])
    scr_ref[8 + tm:16 + tm, :] = jnp.where(last, 0.0, zn_ref[...])
    z = z_ref[...]
    prev = scr_ref[7:7 + tm, :]
    nxt = scr_ref[9:9 + tm, :]
    zs = z + mu_ref[0:1, :] * (prev - z) + mu_ref[1:2, :] * (nxt - z)

    r = zs[:, 0:w_cols]
    k = zs[:, w_cols:2 * w_cols]
    v = zs[:, 2 * w_cols:3 * w_cols]
    lora = zs[:, 3 * w_cols:]
    wd = jnp.tanh(lora[:, 0:LANES])
    ad = lora[:, LANES:2 * LANES]
    gd = _sigmoid(lora[:, 2 * LANES:3 * LANES])
    r_ref[...] = r
    v_ref[...] = v
    g_ref[...] = _dot(gd.astype(BF16), gup_ref[...])
    kk = k * kk_ref[...]
    e128 = e_ref[...]
    kap = jnp.concatenate(
        [kk[:, s:s + LANES] * lax.rsqrt(jnp.maximum(_seg_sum64(kk[:, s:s + LANES] * kk[:, s:s + LANES], e128), 1e-24))
         for s in range(0, w_cols, LANES)], axis=1)
    kap_ref[...] = kap
    outs = ((lwf_ref, bf_ref, ktf_ref), (lwb_ref, bb_ref, ktb_ref))
    for d in range(2):
        lw_ref, b_ref, kt_ref = outs[d]
        lw_ref[...] = -W_DECAY_SCALE * _sigmoid(w0_ref[d:d + 1, :] + _dot(wd.astype(BF16), wup_ref[d]))
        a = _sigmoid(a0_ref[d:d + 1, :] + _dot(ad.astype(BF16), aup_ref[d]))
        kt_ref[...] = k * (1.0 + (a - 1.0) * ka_ref[...])
        b_ref[...] = a * kap


def _seg_ones(width=LANES, seg=RWKV_HEAD):
    i = np.arange(width)
    return jnp.asarray((i[:, None] // seg == i[None, :] // seg).astype(np.float32)).astype(BF16)


def rwkv_features(zr, n_ctx, mu, w0, w_up, a0, a_up, g_up, k_k, k_a, tm=256):
    N, zw = zr.shape
    wc = k_k.shape[0]
    assert zw == 3 * wc + 3 * LANES and W_RANK + W_RANK == LANES and A_RANK + A_RANK == LANES and G_RANK == LANES
    zeros = jnp.zeros((W_RANK, wc), F32)
    wup = jnp.stack([jnp.concatenate([w_up[0], zeros], 0), jnp.concatenate([zeros, w_up[1]], 0)], 0)
    aup = jnp.stack([jnp.concatenate([a_up[0], zeros], 0), jnp.concatenate([zeros, a_up[1]], 0)], 0)
    n_tiles = N // tm
    t8 = tm // 8
    row = lambda i: (i, 0)
    full = lambda i: (0, 0)
    out_sds = jax.ShapeDtypeStruct((N, wc), F32)
    return pl.pallas_call(
        functools.partial(_rwkv_feat_kernel, n_ctx // tm, n_tiles, wc),
        grid=(n_tiles,),
        in_specs=[pl.BlockSpec((tm, zw), row),
                  pl.BlockSpec((8, zw), lambda i: (jnp.maximum(i * t8 - 1, 0), 0)),
                  pl.BlockSpec((8, zw), lambda i: (jnp.minimum((i + 1) * t8, N // 8 - 1), 0)),
                  pl.BlockSpec((2, zw), full), pl.BlockSpec((1, wc), full), pl.BlockSpec((1, wc), full),
                  pl.BlockSpec((2, wc), full), pl.BlockSpec((2, wc), full),
                  pl.BlockSpec((2, LANES, wc), lambda i: (0, 0, 0)), pl.BlockSpec((2, LANES, wc), lambda i: (0, 0, 0)),
                  pl.BlockSpec((LANES, wc), full), pl.BlockSpec((LANES, LANES), full)],
        out_specs=[pl.BlockSpec((tm, wc), row)] * 10,
        out_shape=[out_sds] * 10,
        scratch_shapes=[pltpu.VMEM((tm + 16, zw), F32)],
        compiler_params=_params(("parallel",)), name="rwkv_features",
    )(zr, zr, zr, mu, k_k.reshape(1, wc), k_a.reshape(1, wc), w0, a0,
      wup.astype(BF16), aup.astype(BF16), g_up.astype(BF16), _seg_ones())


def _rwkv_scan_kernel(rev, pairs, n_sub, r_ref, kap_ref, v_ref, lw_ref, b_ref, kt_ref, y_ref, s_ref):
    L = CHUNK
    L2 = 2 * L

    @pl.when(pl.program_id(1) == 0)
    def _():
        s_ref[...] = jnp.zeros(s_ref.shape, F32)

    t_i = lax.broadcasted_iota(jnp.int32, (L, L), 0)
    s_i = lax.broadcasted_iota(jnp.int32, (L, L), 1)
    m_incl64 = ((s_i >= t_i) if rev else (s_i <= t_i)).astype(F32)
    ri = lax.broadcasted_iota(jnp.int32, (L2, L2), 0)
    qi = lax.broadcasted_iota(jnp.int32, (L2, L2), 1)
    rt, qt = ri % L, qi % L
    same = (ri // L) == (qi // L)
    incl = jnp.logical_and(same, (qt >= rt) if rev else (qt <= rt))
    strict = jnp.logical_and(same, (qt > rt) if rev else (qt < rt))
    eye = (ri == qi).astype(F32)
    lane = lax.broadcasted_iota(jnp.int32, (L, LANES), 1)
    hm = [(lane < RWKV_HEAD).astype(F32), (lane >= RWKV_HEAD).astype(F32)]
    bd = ((lax.broadcasted_iota(jnp.int32, (LANES, LANES), 0) // RWKV_HEAD)
          == (lax.broadcasted_iota(jnp.int32, (LANES, LANES), 1) // RWKV_HEAD)).astype(F32)
    lvl_masks = []
    bsz = 1
    while bsz < L:
        grp = (ri // (2 * bsz)) == (qi // (2 * bsz))
        r_odd = (ri // bsz) % 2 == 1
        q_odd = (qi // bsz) % 2 == 1
        off = jnp.logical_and(jnp.logical_not(r_odd), q_odd) if rev else jnp.logical_and(r_odd, jnp.logical_not(q_odd))
        lvl_masks.append(jnp.logical_and(grp, off))
        bsz *= 2

    def stack2(x):
        return jnp.concatenate([x * hm[0], x * hm[1]], axis=0)

    def bmm(a, b):
        return _dot(a.astype(BF16), b.astype(BF16))

    m_incl_bf = m_incl64.astype(BF16)

    units = [(ci, p) for ci in range(n_sub) for p in range(pairs)]

    def blk(ref, unit):
        ci, p = unit
        cc = (n_sub - 1 - ci) if rev else ci
        return ref[cc * L:(cc + 1) * L, p * LANES:(p + 1) * LANES]

    def each(fn, *dicts):
        return {un: fn(*(d[un] for d in dicts)) for un in units}

    def split3(x):
        hi = x.astype(BF16)
        r1 = x - hi.astype(F32)
        mid = r1.astype(BF16)
        lo = (r1 - mid.astype(F32)).astype(BF16)
        return jnp.concatenate([hi, mid, lo], axis=1)

    lw = {un: blk(lw_ref, un) for un in units}
    c3 = each(lambda x: _dot(m_incl_bf, split3(x)), lw)
    cum = each(lambda c: c[:, 0:LANES] + c[:, LANES:2 * LANES] + c[:, 2 * LANES:], c3)
    e_incl = each(jnp.exp, cum)
    e_inv = each(lambda c: jnp.exp(-c), cum)
    p_tot = each(lambda e: e[0:1, :] if rev else e[L - 1:L, :], e_incl)
    xk = {un: stack2(blk(kap_ref, un) * jnp.exp(cum[un] - lw[un])).astype(BF16) for un in units}
    xr = {un: stack2(blk(r_ref, un) * e_incl[un]).astype(BF16) for un in units}
    yb = {un: stack2(blk(b_ref, un) * e_inv[un]).astype(BF16) for un in units}
    yk = {un: stack2(blk(kt_ref, un) * e_inv[un]).astype(BF16) for un in units}
    vs = {un: stack2(blk(v_ref, un)).astype(BF16) for un in units}
    amat = each(lambda a, b, c, d: _dot_nt(jnp.concatenate([a, b], axis=0), jnp.concatenate([c, d], axis=0)),
                xk, xr, yb, yk)
    a_ub = each(lambda a: jnp.where(strict, a[0:L2, 0:L2], 0.0), amat)
    a_uk = each(lambda a: jnp.where(strict, a[0:L2, L2:], 0.0).astype(BF16), amat)
    a_rb = each(lambda a: jnp.where(incl, a[L2:, 0:L2], 0.0).astype(BF16), amat)
    a_rk = each(lambda a: jnp.where(incl, a[L2:, L2:], 0.0).astype(BF16), amat)
    tinv = each(lambda a: eye - jnp.where(lvl_masks[0], a, 0.0), a_ub)
    for lm in lvl_masks[1:]:
        tb = each(lambda t: t.astype(BF16), tinv)
        x1 = each(lambda a, t: _dot(jnp.where(lm, a, 0.0).astype(BF16), t).astype(BF16), a_ub, tb)
        tinv = each(lambda t, tbf, x: t - _dot(tbf, x), tinv, tb, x1)
    tb = each(lambda t: t.astype(BF16), tinv)
    w1 = each(lambda t, x: _dot(t, x).astype(BF16), tb, xk)
    avs = each(lambda a, v: _dot(a, v).astype(BF16), a_uk, vs)
    y_ind = each(_dot, a_rk, vs)
    k2 = each(_dot_tn, vs, yk)
    w2 = each(_dot, tb, avs)

    state = [s_ref[p] for p in range(pairs)]
    for ci in range(n_sub):
        wx = [_dot_nt(jnp.concatenate([w1[(ci, p)], xr[(ci, p)]], axis=0), state[p].astype(BF16))
              for p in range(pairs)]
        ub = [(-(wx[p][0:L2] + w2[(ci, p)])).astype(BF16) for p in range(pairs)]
        upd = [_dot_tn(ub[p], yb[(ci, p)]) + k2[(ci, p)] for p in range(pairs)]
        state = [(state[p] + upd[p] * bd) * p_tot[(ci, p)] for p in range(pairs)]
        cc = (n_sub - 1 - ci) if rev else ci
        for p in range(pairs):
            ystk = wx[p][L2:] + _dot(a_rb[(ci, p)], ub[p]) + y_ind[(ci, p)]
            y_ref[cc * L:(cc + 1) * L, p * LANES:(p + 1) * LANES] = ystk[0:L] + ystk[L:]
    for p in range(pairs):
        s_ref[p] = state[p]


def rwkv_scan(rev, n_ctx, r, kap, v, lw, b, kt, pairs=4, n_sub=4):
    N, wc = r.shape
    rb = n_sub * CHUNK
    bw = pairs * LANES
    assert N % rb == 0 and n_ctx % rb == 0 and wc % bw == 0
    nb, nbc = N // rb, n_ctx // rb
    if rev:
        rmap = lambda h, i: (jnp.where(i < nbc, nbc - 1 - i, nb - 1 - (i - nbc)), h)
    else:
        rmap = lambda h, i: (i, h)
    spec = pl.BlockSpec((rb, bw), rmap)
    return pl.pallas_call(
        functools.partial(_rwkv_scan_kernel, rev, pairs, n_sub),
        grid=(wc // bw, nb),
        in_specs=[spec] * 6, out_specs=spec,
        out_shape=jax.ShapeDtypeStruct((N, wc), F32),
        scratch_shapes=[pltpu.VMEM((pairs, LANES, LANES), F32)],
        compiler_params=_params(("parallel", "arbitrary")), name="rwkv_scan_rev" if rev else "rwkv_scan_fwd",
    )(r, kap, v, lw, b, kt)


def _rwkv_out_kernel(yf_ref, yb_ref, r_ref, v_ref, ktf_ref, ktb_ref, g_ref, rk_ref, gn_ref, e_ref, o_ref):
    e128 = e_ref[...]
    inv = 1.0 / RWKV_HEAD
    for s in range(0, o_ref.shape[1], LANES):
        sl = slice(s, s + LANES)
        y = yf_ref[:, sl] + yb_ref[:, sl]
        mu = _seg_sum64(y, e128) * inv
        yc = y - mu
        var = _seg_sum64(yc * yc, e128) * inv
        yn = yc * lax.rsqrt(var + GN_EPS) * gn_ref[0:1, sl] + gn_ref[1:2, sl]
        rk = r_ref[:, sl] * rk_ref[0:1, sl]
        bonus = _seg_sum64(rk * (ktf_ref[:, sl] + ktb_ref[:, sl]), e128) * v_ref[:, sl]
        o_ref[:, sl] = ((yn + bonus) * g_ref[:, sl]).astype(o_ref.dtype)


def rwkv_output(y_f, y_b, r, v, kt_f, kt_b, g, r_k, lnx_gain, lnx_bias, tm=256):
    N, wc = r.shape
    row = lambda i: (i, 0)
    full = lambda i: (0, 0)
    gn = jnp.stack([lnx_gain, lnx_bias], 0)
    return pl.pallas_call(
        _rwkv_out_kernel, grid=(N // tm,),
        in_specs=[pl.BlockSpec((tm, wc), row)] * 7 + [pl.BlockSpec((1, wc), full), pl.BlockSpec((2, wc), full),
                                                      pl.BlockSpec((LANES, LANES), full)],
        out_specs=pl.BlockSpec((tm, wc), row),
        out_shape=jax.ShapeDtypeStruct((N, wc), BF16),
        compiler_params=_params(("parallel",)), name="rwkv_output",
    )(y_f, y_b, r, v, kt_f, kt_b, g, r_k.reshape(1, wc), gn, _seg_ones())


def _angles(i, j, period):
    return (2.0 * math.pi / period) * ((i * j) % period).astype(F32)


def _chan_dft(gdim, scale):
    i = lax.iota(jnp.int32, gdim)
    ang = _angles(i[:, None], i[None, :], gdim)
    return jnp.concatenate([jnp.cos(ang), -jnp.sin(ang)], 1) * scale


def _fnet1_kernel(groups, x_ref, cs_ref, m_ref, y_ref):
    r1 = x_ref.shape[0]
    gdim = x_ref.shape[1] // groups
    for g in range(groups):
        sl = slice(g * gdim, (g + 1) * gdim)
        ab = _dot3(x_ref[:, sl], cs_ref[...])
        st = jnp.concatenate([ab[:, :gdim], ab[:, gdim:]], axis=0)
        y = _dot3(m_ref[0], st)
        y_ref[0, 0, :, sl] = y[:r1]
        y_ref[1, 0, :, sl] = y[r1:]


def _fnet3_kernel(cs_ref, y_ref, o_ref):
    o_ref[...] = _dot3(cs_ref[...], y_ref[...])


def fnet_latent(zf, groups=FOURIER_GROUPS):
    T, fw = zf.shape
    gdim = fw // groups
    r2 = GRID_W
    r1 = T // r2
    cs = _chan_dft(gdim, 1.0 / math.sqrt(T * gdim))
    t1p = lax.iota(jnp.int32, r1)[None, :, None]
    t1 = lax.iota(jnp.int32, r1)[None, None, :]
    t2 = lax.iota(jnp.int32, r2)[:, None, None]
    th = _angles(t1p, r2 * t1 + t2, T)
    gr, gi = jnp.cos(th), -jnp.sin(th)
    mt = jnp.concatenate([jnp.concatenate([gr, -gi], 2), jnp.concatenate([gi, gr], 2)], 1)
    y = pl.pallas_call(
        functools.partial(_fnet1_kernel, groups),
        grid=(r2,),
        in_specs=[pl.BlockSpec((r1, fw), lambda j: (0, j)),
                  pl.BlockSpec((gdim, 2 * gdim), lambda j: (0, 0)),
                  pl.BlockSpec((1, 2 * r1, 2 * r1), lambda j: (j, 0, 0))],
        out_specs=pl.BlockSpec((2, 1, r1, fw), lambda j: (0, j, 0, 0)),
        out_shape=jax.ShapeDtypeStruct((2, r2, r1, fw), F32),
        compiler_params=_params(("parallel",)), name="fnet_rows",
    )(zf.reshape(r1, r2 * fw), cs, mt)
    i2 = lax.iota(jnp.int32, r2)
    ph = _angles(i2[:, None], i2[None, :], r2)
    cs2 = jnp.concatenate([jnp.cos(ph), jnp.sin(ph)], 1)
    tc = 4 * fw if r1 % 4 == 0 else fw
    o = pl.pallas_call(
        _fnet3_kernel, grid=(r1 * fw // tc,),
        in_specs=[pl.BlockSpec((r2, 2 * r2), lambda j: (0, 0)), pl.BlockSpec((2 * r2, tc), lambda j: (0, j))],
        out_specs=pl.BlockSpec((r2, tc), lambda j: (0, j)),
        out_shape=jax.ShapeDtypeStruct((r2, r1 * fw), F32),
        compiler_params=_params(("parallel",)), name="fnet_cols",
    )(cs2, y.reshape(2 * r2, r1 * fw))
    return o.reshape(T, fw)


def _fnet_direct_kernel(x_ref, cs_ref, ct_ref, o_ref):
    gdim = x_ref.shape[1]
    ab = _dot3(x_ref[...], cs_ref[...])
    st = jnp.concatenate([ab[:, :gdim], ab[:, gdim:]], axis=0)
    o_ref[...] = _dot3(ct_ref[...], st)


def fnet_direct(zf, groups=FOURIER_GROUPS):
    C, fw = zf.shape
    gdim = fw // groups
    cs = _chan_dft(gdim, 1.0 / math.sqrt(C * gdim))
    i = lax.iota(jnp.int32, C)
    ang = _angles(i[:, None], i[None, :], C)
    ct = jnp.concatenate([jnp.cos(ang), jnp.sin(ang)], 1)
    return pl.pallas_call(
        _fnet_direct_kernel, grid=(groups,),
        in_specs=[pl.BlockSpec((C, gdim), lambda g: (0, g)), pl.BlockSpec((gdim, 2 * gdim), lambda g: (0, 0)),
                  pl.BlockSpec((C, 2 * C), lambda g: (0, 0))],
        out_specs=pl.BlockSpec((C, gdim), lambda g: (0, g)),
        out_shape=jax.ShapeDtypeStruct((C, fw), F32),
        compiler_params=_params(("parallel",)), name="fnet_direct",
    )(zf, cs, ct)


GATE_LANE0 = N_GROUPS


def _gates_kernel(lg_ref, id_ref, gv_ref):
    lg = lg_ref[...]
    lane = lax.broadcasted_iota(jnp.int32, lg.shape, 1)
    big = jnp.int32(LANES)
    neg = -jnp.inf
    is_g = lane < N_GROUPS
    gl = jnp.where(is_g, lg, neg)
    mg = jnp.max(gl, axis=-1, keepdims=True)
    p_group = 1.0 / jnp.sum(jnp.exp(gl - mg), axis=-1, keepdims=True)
    gsel = jnp.min(jnp.where(gl == mg, lane, big), axis=-1, keepdims=True)
    e_idx = lane - GATE_LANE0
    in_grp = jnp.logical_and(jnp.logical_and(e_idx >= 0, e_idx < N_EXPERTS), e_idx // EXPERTS_PER_GROUP == gsel)
    le = jnp.where(in_grp, lg, neg)
    m1 = jnp.max(le, axis=-1, keepdims=True)
    l1 = jnp.min(jnp.where(le == m1, lane, big), axis=-1, keepdims=True)
    le2 = jnp.where(lane == l1, neg, le)
    m2 = jnp.max(le2, axis=-1, keepdims=True)
    l2 = jnp.min(jnp.where(le2 == m2, lane, big), axis=-1, keepdims=True)
    e2 = jnp.exp(m2 - m1)
    inv = p_group / (1.0 + e2)
    id_ref[...] = jnp.where(lane == 0, l1 - GATE_LANE0, jnp.where(lane == 1, l2 - GATE_LANE0, 0))
    gv_ref[...] = jnp.where(lane == 0, inv, jnp.where(lane == 1, e2 * inv, 0.0))


def moe_gates(logits, tm=256):
    N = logits.shape[0]
    spec = pl.BlockSpec((tm, LANES), lambda i: (i, 0))
    return pl.pallas_call(
        _gates_kernel, grid=(N // tm,),
        in_specs=[spec], out_specs=[spec, spec],
        out_shape=[jax.ShapeDtypeStruct((N, LANES), jnp.int32), jax.ShapeDtypeStruct((N, LANES), F32)],
        compiler_params=_params(("parallel",)), name="moe_gates",
    )(logits)


def moe_dispatch(expert_ids, ts):
    N = expert_ids.shape[0]
    flat = expert_ids.reshape(-1)
    onehot = (flat[:, None] == jnp.arange(N_EXPERTS, dtype=jnp.int32)[None, :]).astype(jnp.int32)
    csum = jnp.cumsum(onehot, axis=0)
    rank = jnp.take_along_axis(csum, flat[:, None], axis=1)[:, 0] - 1
    counts = csum[-1]
    tiles_per = (counts + ts - 1) // ts
    tiles_end = jnp.cumsum(tiles_per)
    pad_start = (tiles_end - tiles_per) * ts
    slot = pad_start[flat] + rank
    n_tiles = -(-2 * N // ts) + N_EXPERTS
    slot_token = jnp.zeros((n_tiles * ts,), jnp.int32).at[slot].set(jnp.arange(2 * N, dtype=jnp.int32) // 2)
    n_used = tiles_end[-1]
    tile_idx = jnp.minimum(jnp.arange(n_tiles, dtype=jnp.int32), n_used - 1)
    tile_expert = jnp.sum((tile_idx[:, None] >= tiles_end[None, :]).astype(jnp.int32), axis=1)
    return slot_token, slot.reshape(N, 2), tile_expert.astype(jnp.int32), n_used.reshape(1).astype(jnp.int32)


def _moe_group_kernel(ts, te_ref, nu_ref, st_ref, u_hbm, w1_ref, w3_ref, w2_ref, o_ref, ubuf, ub16, sem):
    i = pl.program_id(0)
    j = pl.program_id(1)
    used = nu_ref[0]

    def gather(tile, b):
        def row(r, carry):
            tok = st_ref[tile * ts + r]
            pltpu.make_async_copy(u_hbm.at[pl.ds(tok, 1), :], ubuf.at[b, pl.ds(r, 1), :], sem.at[b]).start()
            return carry
        lax.fori_loop(0, ts, row, 0)

    @pl.when(jnp.logical_and(j == 0, i < used))
    def _():
        @pl.when(i == 0)
        def _():
            gather(0, 0)

        @pl.when(i + 1 < used)
        def _():
            gather(i + 1, (i + 1) % 2)

        b = i % 2
        pltpu.make_async_copy(u_hbm.at[pl.ds(0, ts), :], ubuf.at[b], sem.at[b]).wait()
        ub16[...] = ubuf[b].astype(BF16)

    @pl.when(j == 0)
    def _():
        o_ref[...] = jnp.zeros(o_ref.shape, o_ref.dtype)

    @pl.when(i < used)
    def _():
        u = ub16[...]
        a = _dot(u, w1_ref[0, 0])
        g = _dot(u, w3_ref[0, 0])
        h = ((a * _sigmoid(a)) * g).astype(BF16)
        nw = o_ref.shape[1] // 4
        for n in range(0, o_ref.shape[1], nw):
            o_ref[:, n:n + nw] += _dot(h, w2_ref[0, 0, :, n:n + nw])


def moe_grouped(u, slot_token, tile_expert, n_used, w1, w3, w2, layer, ts, tc=256):
    N, D = u.shape
    _, E, _, De = w1.shape
    n_tiles = tile_expert.shape[0]
    grid_spec = pltpu.PrefetchScalarGridSpec(
        num_scalar_prefetch=3,
        grid=(n_tiles, De // tc),
        in_specs=[pl.BlockSpec(memory_space=pl.ANY),
                  pl.BlockSpec((1, 1, D, tc), lambda i, j, te, nu, st: (layer, te[i], 0, j)),
                  pl.BlockSpec((1, 1, D, tc), lambda i, j, te, nu, st: (layer, te[i], 0, j)),
                  pl.BlockSpec((1, 1, tc, D), lambda i, j, te, nu, st: (layer, te[i], j, 0))],
        out_specs=pl.BlockSpec((ts, D), lambda i, j, te, nu, st: (i, 0)),
        scratch_shapes=[pltpu.VMEM((2, ts, D), F32), pltpu.VMEM((ts, D), BF16), pltpu.SemaphoreType.DMA((2,))],
    )
    return pl.pallas_call(
        functools.partial(_moe_group_kernel, ts), grid_spec=grid_spec,
        out_shape=jax.ShapeDtypeStruct((n_tiles * ts, D), F32),
        compiler_params=_params(("arbitrary", "arbitrary")), name="moe_grouped",
    )(tile_expert, n_used, slot_token, u, w1, w3, w2)


def _moe_combine_kernel(alpha, n_ctx_tiles, sa_ref, sb_ref, o_hbm, x_ref, gv_ref, g_ref, gb_ref, out_ref,
                        abuf, bbuf, sem):
    i = pl.program_id(0)
    tm = x_ref.shape[0]

    def gather(tile, b):
        def row(r, carry):
            t = tile * tm + r
            pltpu.make_async_copy(o_hbm.at[pl.ds(sa_ref[t], 1), :], abuf.at[b, pl.ds(r, 1), :], sem.at[0, b]).start()
            pltpu.make_async_copy(o_hbm.at[pl.ds(sb_ref[t], 1), :], bbuf.at[b, pl.ds(r, 1), :], sem.at[1, b]).start()
            return carry
        lax.fori_loop(0, tm, row, 0)

    @pl.when(i == 0)
    def _():
        gather(0, 0)

    @pl.when(i + 1 < pl.num_programs(0))
    def _():
        gather(i + 1, (i + 1) % 2)

    b = i % 2
    pltpu.make_async_copy(o_hbm.at[pl.ds(0, tm), :], abuf.at[b], sem.at[0, b]).wait()
    pltpu.make_async_copy(o_hbm.at[pl.ds(0, tm), :], bbuf.at[b], sem.at[1, b]).wait()
    gv = gv_ref[...]
    f = gv[:, 0:1] * abuf[b] + gv[:, 1:2] * bbuf[b]
    g = jnp.where(i < n_ctx_tiles, g_ref[1:2, :], g_ref[0:1, :])
    y = alpha * x_ref[...] + g * f
    out_ref[...] = _ln_rows(y, LN_EPS) * gb_ref[0:1, :] + gb_ref[1:2, :]


def moe_combine_ln(xx, o, slots, gate_vals, gates2, gain_bias, n_ctx, alpha, tm=256):
    N, D = xx.shape
    grid_spec = pltpu.PrefetchScalarGridSpec(
        num_scalar_prefetch=2,
        grid=(N // tm,),
        in_specs=[pl.BlockSpec(memory_space=pl.ANY),
                  pl.BlockSpec((tm, D), lambda i, sa, sb: (i, 0)),
                  pl.BlockSpec((tm, LANES), lambda i, sa, sb: (i, 0)),
                  pl.BlockSpec((2, D), lambda i, sa, sb: (0, 0)), pl.BlockSpec((2, D), lambda i, sa, sb: (0, 0))],
        out_specs=pl.BlockSpec((tm, D), lambda i, sa, sb: (i, 0)),
        scratch_shapes=[pltpu.VMEM((2, tm, D), F32), pltpu.VMEM((2, tm, D), F32), pltpu.SemaphoreType.DMA((2, 2))],
    )
    return pl.pallas_call(
        functools.partial(_moe_combine_kernel, alpha, n_ctx // tm), grid_spec=grid_spec,
        out_shape=jax.ShapeDtypeStruct((N, D), F32),
        compiler_params=_params(("arbitrary",)), name="moe_combine_ln",
    )(slots[:, 0], slots[:, 1], o, xx, gate_vals, gates2, gain_bias)


MOE_SLOT_TILE = 512

def _pick_tile(n, prefer):
    for t in prefer:
        if n % t == 0:
            return t
    raise ValueError(f"no tile for {n}")


def kernel(x, c, ctx, c_ctx, w_mod, b_mod, w_in, q_gain, k_gain, rwkv_mu, w0, w_up, a0, a_up, g_up, k_k, k_a, r_k, lnx_gain, lnx_bias, w_out, ln1_gain, ln1_bias, ln2_gain, ln2_bias, router_group_w, router_group_b, router_expert_w, router_expert_b, w1, w3, w2):
    B, T, D = x.shape
    C = ctx.shape[1]
    assert B == 1
    depth = w_mod.shape[0]
    alpha = (2 * depth) ** 0.25
    N = C + T
    aw = D // 2
    kvw = KV_HEADS * HEAD_DIM
    rw = D // 4
    fw = D // 4
    rcols = 3 * rw + 2 * W_RANK + 2 * A_RANK + G_RANK
    o_zr = aw + 2 * kvw
    o_zf = o_zr + rcols
    tm_big = _pick_tile(N, (768, 384, 256))

    xx = jnp.concatenate([ctx[0], x[0]], 0)
    mods = mod_vectors(c, c_ctx, w_mod, b_mod).reshape(depth, 8, 6, D)
    cosf, sinf = rope_tables(C, T)
    w_qkv = w_in[:, :, :o_zr].astype(BF16)
    w_rwkv = w_in[:, :, o_zr:o_zf].astype(BF16)
    w_four = w_in[:, :, o_zf:].astype(BF16)
    w_out_b = w_out.astype(BF16)
    w1_b, w3_b, w2_b = w1.astype(BF16), w3.astype(BF16), w2.astype(BF16)

    for l in range(depth):
        last = l == depth - 1
        mv = mods[l]
        vec = lambda i: jnp.stack([mv[0, i], mv[1, i]], 0)
        ss1 = jnp.stack([mv[0, 0], mv[0, 1], mv[1, 0], mv[1, 1]], 0)
        ss2 = jnp.stack([mv[0, 3], mv[0, 4], mv[1, 3], mv[1, 4]], 0)

        u = ln_modulate(xx, ss1, C)
        zq = matmul(u, w_qkv, l, F32, tm_big, _pick_tile(o_zr, (768, 512, 256, 128)))
        zr = matmul(u, w_rwkv, l, F32, tm_big, _pick_tile(rcols, (1152, 384, 128)))
        zf = matmul(u, w_four, l, F32, tm_big, _pick_tile(fw, (512, 256, 128)))
        qh, kh, vh = qkv_prep(zq, cosf, sinf, q_gain[l], k_gain[l], aw)
        attn = attention(qh, kh, vh, C)
        r, v, kap, g, lwf, bf, ktf, lwb, bb, ktb = rwkv_features(
            zr, C, rwkv_mu[l], w0[l], w_up[l], a0[l], a_up[l], g_up[l], k_k[l], k_a[l])
        y_f = rwkv_scan(False, C, r, kap, v, lwf, bf, ktf)
        y_b = rwkv_scan(True, C, r, kap, v, lwb, bb, ktb)
        rwkv = rwkv_output(y_f, y_b, r, v, ktf, ktb, g, r_k[l].reshape(-1), lnx_gain[l], lnx_bias[l])
        fl = fnet_latent(zf[C:])
        fc = fnet_direct(zf[:C]) if not last else jnp.zeros((C, fw), F32)
        fn = jnp.concatenate([fc, fl], 0).astype(BF16)
        cat = jnp.concatenate([attn, rwkv, fn], 1)
        m = matmul(cat, w_out_b, l, F32, tm_big, _pick_tile(D, (512, 256, 128)))
        xx = resid_ln(xx, m, vec(2), jnp.stack([ln1_gain[l], ln1_bias[l]], 0), C, alpha)

        wr = jnp.concatenate([router_group_w[l], router_expert_w[l],
                              jnp.zeros((D, LANES - N_GROUPS - N_EXPERTS), F32)], 1)
        br = jnp.concatenate([router_group_b[l], router_expert_b[l],
                              jnp.zeros((LANES - N_GROUPS - N_EXPERTS,), F32)], 0).reshape(1, LANES)
        u2, logits = ln_modulate(xx, ss2, C, router=(wr, br), out_dtype=F32)
        expert_ids, gate_vals = moe_gates(logits)
        slot_token, slots, tile_expert, n_used = moe_dispatch(expert_ids[:, :2], MOE_SLOT_TILE)
        o = moe_grouped(u2, slot_token, tile_expert, n_used, w1_b, w3_b, w2_b, l, MOE_SLOT_TILE)
        xx = moe_combine_ln(xx, o, slots, gate_vals, vec(5), jnp.stack([ln2_gain[l], ln2_bias[l]], 0), C, alpha)
    return xx[C:][None]
```

```python
import functools
import math

import numpy as np
import jax
import jax.numpy as jnp
from jax import lax
from jax.experimental import pallas as pl
from jax.experimental.pallas import tpu as pltpu

F32 = jnp.float32
BF16 = jnp.bfloat16

GRID_W = 64
HEAD_DIM = 128
KV_HEADS = 4
RWKV_HEAD = 64
W_RANK = 64
A_RANK = 64
G_RANK = 128
FOURIER_GROUPS = 4
N_GROUPS = 4
EXPERTS_PER_GROUP = 4
N_EXPERTS = N_GROUPS * EXPERTS_PER_GROUP
ROPE_THETA = 10000.0
W_DECAY_SCALE = math.exp(-0.5)
GN_EPS = 64e-5
LN_EPS = 1e-6
LANES = 128
CHUNK = 64

V7X_VMEM_LIMIT_MB = 56


def _params(sem, vmem_mb=V7X_VMEM_LIMIT_MB):
    return pltpu.CompilerParams(dimension_semantics=sem, vmem_limit_bytes=vmem_mb * 1024 * 1024)


def _dot(a, b, prec=None):
    return jnp.dot(a, b, preferred_element_type=F32, precision=prec)


def _dot_nt(a, b, prec=None):
    return lax.dot_general(a, b, (((1,), (1,)), ((), ())), preferred_element_type=F32, precision=prec)


def _dot_tn(a, b, prec=None):
    return lax.dot_general(a, b, (((0,), (0,)), ((), ())), preferred_element_type=F32, precision=prec)


def _sigmoid(x):
    return 1.0 / (1.0 + jnp.exp(-x))


def _split2(x):
    hi = x.astype(BF16)
    return hi, (x - hi.astype(F32)).astype(BF16)


def _dot3(a, b):
    a_hi, a_lo = _split2(a)
    b_hi, b_lo = _split2(b)
    return _dot(a_hi, b_hi) + (_dot(a_hi, b_lo) + _dot(a_lo, b_hi))


def _dot_exact_rhs(x, e):
    hi = x.astype(BF16)
    r1 = x - hi.astype(F32)
    mid = r1.astype(BF16)
    lo = (r1 - mid.astype(F32)).astype(BF16)
    return _dot(hi, e) + (_dot(mid, e) + _dot(lo, e))


def _mod_kernel(a_ref, w_ref, b_ref, o_ref):
    tn = o_ref.shape[2]
    rows = []
    for r in range(2):
        cols = []
        for j in range(tn // LANES):
            prod = w_ref[0, :, j * LANES:(j + 1) * LANES] * a_ref[r]
            cols.append(jnp.sum(prod, axis=0, keepdims=True))
        rows.append(jnp.concatenate(cols, axis=1) + b_ref[0])
    rows.append(jnp.zeros((6, tn), F32))
    o_ref[0] = jnp.concatenate(rows, axis=0)


def mod_vectors(c, c_ctx, w_mod, b_mod):
    L, D, D6 = w_mod.shape
    acts = jnp.stack([jax.nn.silu(c[0]), jax.nn.silu(c_ctx)], 0)
    a_b = jnp.broadcast_to(acts[:, :, None], (2, D, LANES))
    tn = 512
    return pl.pallas_call(
        _mod_kernel,
        grid=(L, D6 // tn),
        in_specs=[pl.BlockSpec((2, D, LANES), lambda l, j: (0, 0, 0)),
                  pl.BlockSpec((1, D, tn), lambda l, j: (l, 0, j)),
                  pl.BlockSpec((1, 1, tn), lambda l, j: (l, 0, j))],
        out_specs=pl.BlockSpec((1, 8, tn), lambda l, j: (l, 0, j)),
        out_shape=jax.ShapeDtypeStruct((L, 8, D6), F32),
        compiler_params=_params(("parallel", "parallel")),
        name="mod_vectors",
    )(a_b, w_mod, b_mod.reshape(L, 1, D6))


def _ln_rows(x, eps):
    mu = jnp.mean(x, axis=-1, keepdims=True)
    xc = x - mu
    var = jnp.mean(xc * xc, axis=-1, keepdims=True)
    return xc * lax.rsqrt(var + eps)


def _ln_mod_kernel(n_ctx_tiles, with_router, x_ref, ss_ref, *rest):
    if with_router:
        wr_ref, br_ref, u_ref, lg_ref = rest
    else:
        (u_ref,) = rest
    is_ctx = pl.program_id(0) < n_ctx_tiles
    sh = jnp.where(is_ctx, ss_ref[2:3, :], ss_ref[0:1, :])
    sc = jnp.where(is_ctx, ss_ref[3:4, :], ss_ref[1:2, :])
    u = _ln_rows(x_ref[...], LN_EPS) * (1.0 + sc) + sh
    u_ref[...] = u.astype(u_ref.dtype)
    if with_router:
        lg_ref[...] = _dot3(u, wr_ref[...]) + br_ref[...]


def ln_modulate(xx, ss, n_ctx, router=None, out_dtype=BF16, tm=256):
    N, D = xx.shape
    assert N % tm == 0 and n_ctx % tm == 0
    in_specs = [pl.BlockSpec((tm, D), lambda i: (i, 0)), pl.BlockSpec((4, D), lambda i: (0, 0))]
    out_specs = [pl.BlockSpec((tm, D), lambda i: (i, 0))]
    out_shape = [jax.ShapeDtypeStruct((N, D), out_dtype)]
    args = [xx, ss]
    if router is not None:
        in_specs += [pl.BlockSpec((D, LANES), lambda i: (0, 0)), pl.BlockSpec((1, LANES), lambda i: (0, 0))]
        out_specs.append(pl.BlockSpec((tm, LANES), lambda i: (i, 0)))
        out_shape.append(jax.ShapeDtypeStruct((N, LANES), F32))
        args += list(router)
    res = pl.pallas_call(
        functools.partial(_ln_mod_kernel, n_ctx // tm, router is not None),
        grid=(N // tm,), in_specs=in_specs, out_specs=out_specs, out_shape=out_shape,
        compiler_params=_params(("parallel",)), name="ln_modulate",
    )(*args)
    return res if router is not None else res[0]


def _resid_ln_kernel(alpha, n_ctx_tiles, x_ref, m_ref, g_ref, gb_ref, o_ref):
    is_ctx = pl.program_id(0) < n_ctx_tiles
    g = jnp.where(is_ctx, g_ref[1:2, :], g_ref[0:1, :])
    y = alpha * x_ref[...] + g * m_ref[...].astype(F32)
    o_ref[...] = _ln_rows(y, LN_EPS) * gb_ref[0:1, :] + gb_ref[1:2, :]


def resid_ln(xx, m, gates2, gain_bias, n_ctx, alpha, tm=256):
    N, D = xx.shape
    return pl.pallas_call(
        functools.partial(_resid_ln_kernel, alpha, n_ctx // tm),
        grid=(N // tm,),
        in_specs=[pl.BlockSpec((tm, D), lambda i: (i, 0)), pl.BlockSpec((tm, D), lambda i: (i, 0)),
                  pl.BlockSpec((2, D), lambda i: (0, 0)), pl.BlockSpec((2, D), lambda i: (0, 0))],
        out_specs=pl.BlockSpec((tm, D), lambda i: (i, 0)),
        out_shape=jax.ShapeDtypeStruct((N, D), F32),
        compiler_params=_params(("parallel",)), name="resid_ln",
    )(xx, m, gates2, gain_bias)


def _mm_kernel(a_ref, b_ref, o_ref):
    o_ref[...] = _dot(a_ref[...], b_ref[0]).astype(o_ref.dtype)


def matmul(a, b, layer, out_dtype, tm, tn):
    M, K = a.shape
    _, _, Nn = b.shape
    assert M % tm == 0 and Nn % tn == 0
    return pl.pallas_call(
        _mm_kernel, grid=(M // tm, Nn // tn),
        in_specs=[pl.BlockSpec((tm, K), lambda i, j: (i, 0)), pl.BlockSpec((1, K, tn), lambda i, j: (layer, 0, j))],
        out_specs=pl.BlockSpec((tm, tn), lambda i, j: (i, j)),
        out_shape=jax.ShapeDtypeStruct((M, Nn), out_dtype),
        compiler_params=_params(("parallel", "arbitrary")), name="matmul",
    )(a, b)


def _qkv_prep_kernel(n_q_heads, scale, z_ref, cos_ref, sin_ref, qg_ref, kg_ref, q_ref, k_ref, v_ref):
    cosf = cos_ref[...]
    sinf = sin_ref[...]

    def norm_rope(t, gain):
        t = t * lax.rsqrt(jnp.mean(t * t, axis=-1, keepdims=True) + LN_EPS) * gain
        return t * cosf + pltpu.roll(t, HEAD_DIM // 2, 1) * sinf

    for h in range(n_q_heads):
        sl = slice(h * HEAD_DIM, (h + 1) * HEAD_DIM)
        q_ref[:, sl] = (norm_rope(z_ref[:, sl], qg_ref[...]) * scale).astype(q_ref.dtype)
    qw = n_q_heads * HEAD_DIM
    for h in range(KV_HEADS):
        sl = slice(h * HEAD_DIM, (h + 1) * HEAD_DIM)
        zs = slice(qw + h * HEAD_DIM, qw + (h + 1) * HEAD_DIM)
        k_ref[:, sl] = norm_rope(z_ref[:, zs], kg_ref[...]).astype(k_ref.dtype)
    kvw = KV_HEADS * HEAD_DIM
    ones = jnp.ones((z_ref.shape[0], HEAD_DIM), v_ref.dtype)
    for h in range(KV_HEADS):
        zs = slice(qw + kvw + h * HEAD_DIM, qw + kvw + (h + 1) * HEAD_DIM)
        v_ref[:, 2 * h * HEAD_DIM:(2 * h + 1) * HEAD_DIM] = z_ref[:, zs].astype(v_ref.dtype)
        v_ref[:, (2 * h + 1) * HEAD_DIM:(2 * h + 2) * HEAD_DIM] = ones


def qkv_prep(z, cosf, sinf, q_gain, k_gain, attn_width, tm=256):
    N = z.shape[0]
    n_q = attn_width // HEAD_DIM
    kvw = KV_HEADS * HEAD_DIM
    zw = attn_width + 2 * kvw
    return pl.pallas_call(
        functools.partial(_qkv_prep_kernel, n_q, HEAD_DIM ** -0.5 * math.log2(math.e)),
        grid=(N // tm,),
        in_specs=[pl.BlockSpec((tm, zw), lambda i: (i, 0)),
                  pl.BlockSpec((tm, HEAD_DIM), lambda i: (i, 0)), pl.BlockSpec((tm, HEAD_DIM), lambda i: (i, 0)),
                  pl.BlockSpec((1, HEAD_DIM), lambda i: (0, 0)), pl.BlockSpec((1, HEAD_DIM), lambda i: (0, 0))],
        out_specs=[pl.BlockSpec((tm, attn_width), lambda i: (i, 0)),
                   pl.BlockSpec((tm, kvw), lambda i: (i, 0)), pl.BlockSpec((tm, 2 * kvw), lambda i: (i, 0))],
        out_shape=[jax.ShapeDtypeStruct((N, attn_width), BF16),
                   jax.ShapeDtypeStruct((N, kvw), BF16), jax.ShapeDtypeStruct((N, 2 * kvw), BF16)],
        compiler_params=_params(("parallel",)), name="qkv_prep",
    )(z, cosf, sinf, q_gain.reshape(1, HEAD_DIM), k_gain.reshape(1, HEAD_DIM))


def _attn_kernel(n_ctx, n_ctx_tiles, group, q_ref, k_ref, v_ref, o_ref):
    tq = q_ref.shape[0]
    hh = group // 2
    halves = [jnp.concatenate([q_ref[:, h * HEAD_DIM:(h + 1) * HEAD_DIM] for h in range(i * hh, (i + 1) * hh)], axis=0)
              for i in range(2)]

    def attend(k, v):
        s = [_dot_nt(qh, k) for qh in halves]
        p = [jnp.exp2(x - jnp.max(x, axis=-1, keepdims=True)).astype(v.dtype) for x in s]
        acc = [_dot(x, v) for x in p]
        for h in range(group):
            a = acc[h // hh][(h % hh) * tq:(h % hh + 1) * tq]
            o_ref[:, h * HEAD_DIM:(h + 1) * HEAD_DIM] = (a[:, :HEAD_DIM] / a[:, HEAD_DIM:]).astype(o_ref.dtype)

    is_ctx = pl.program_id(1) < n_ctx_tiles

    @pl.when(is_ctx)
    def _():
        attend(k_ref[0:n_ctx, :], v_ref[0:n_ctx, :])

    @pl.when(jnp.logical_not(is_ctx))
    def _():
        attend(k_ref[...], v_ref[...])


def attention(q, k, v, n_ctx, tq=128):
    N, aw = q.shape
    group = aw // HEAD_DIM // KV_HEADS
    gw = group * HEAD_DIM
    assert N % tq == 0 and n_ctx % tq == 0
    return pl.pallas_call(
        functools.partial(_attn_kernel, n_ctx, n_ctx // tq, group),
        grid=(KV_HEADS, N // tq),
        in_specs=[pl.BlockSpec((tq, gw), lambda g, i: (i, g)),
                  pl.BlockSpec((N, HEAD_DIM), lambda g, i: (0, g)),
                  pl.BlockSpec((N, 2 * HEAD_DIM), lambda g, i: (0, g))],
        out_specs=pl.BlockSpec((tq, gw), lambda g, i: (i, g)),
        out_shape=jax.ShapeDtypeStruct((N, aw), BF16),
        compiler_params=_params(("parallel", "arbitrary")), name="attention",
    )(q, k, v)


def rope_tables(n_ctx, n_lat):
    n_rows = n_lat // GRID_W
    row = jnp.repeat(jnp.arange(n_rows), GRID_W).astype(F32)
    col = jnp.tile(jnp.arange(GRID_W), n_rows).astype(F32)
    axis_dim = HEAD_DIM // 2
    inv_freq = ROPE_THETA ** (-jnp.arange(0, axis_dim, 2, dtype=F32) / axis_dim)
    ang = jnp.concatenate([row[:, None] * inv_freq, col[:, None] * inv_freq], -1)
    cos, sin = jnp.cos(ang), jnp.sin(ang)
    cosf = jnp.concatenate([cos, cos], -1)
    sinf = jnp.concatenate([-sin, sin], -1)
    cosf = jnp.concatenate([jnp.ones((n_ctx, HEAD_DIM), F32), cosf], 0)
    sinf = jnp.concatenate([jnp.zeros((n_ctx, HEAD_DIM), F32), sinf], 0)
    return cosf, sinf


def _seg_sum64(x, e128):
    return _dot_exact_rhs(x, e128)


def _rwkv_feat_kernel(n_ctx_tiles, n_tiles, w_cols,
                      z_ref, zp_ref, zn_ref, mu_ref, kk_ref, ka_ref, w0_ref, a0_ref,
                      wup_ref, aup_ref, gup_ref, e_ref,
                      r_ref, v_ref, kap_ref, g_ref, lwf_ref, bf_ref, ktf_ref, lwb_ref, bb_ref, ktb_ref,
                      scr_ref):
    i = pl.program_id(0)
    tm = z_ref.shape[0]
    first = jnp.logical_or(i == 0, i == n_ctx_tiles)
    last = jnp.logical_or(i == n_ctx_tiles - 1, i == n_tiles - 1)
    scr_ref[8:8 + tm, :] = z_ref[...]
    scr_ref[0:8, :] = jnp.where(first, 0.0, zp_ref[...])
    scr_ref[8 + tm:16 + tm, :] = jnp.where(last, 0.0, zn_ref[...])
    z = z_ref[...]
    prev = scr_ref[7:7 + tm, :]
    nxt = scr_ref[9:9 + tm, :]
    zs = z + mu_ref[0:1, :] * (prev - z) + mu_ref[1:2, :] * (nxt - z)

    r = zs[:, 0:w_cols]
    k = zs[:, w_cols:2 * w_cols]
    v = zs[:, 2 * w_cols:3 * w_cols]
    lora = zs[:, 3 * w_cols:]
    wd = jnp.tanh(lora[:, 0:LANES])
    ad = lora[:, LANES:2 * LANES]
    gd = _sigmoid(lora[:, 2 * LANES:3 * LANES])
    r_ref[...] = r.astype(r_ref.dtype)
    v_ref[...] = v.astype(v_ref.dtype)
    g_ref[...] = _dot(gd.astype(BF16), gup_ref[...]).astype(g_ref.dtype)
    kk = k * kk_ref[...]
    e128 = e_ref[...]
    kap = jnp.concatenate(
        [kk[:, s:s + LANES] * lax.rsqrt(jnp.maximum(_seg_sum64(kk[:, s:s + LANES] * kk[:, s:s + LANES], e128), 1e-24))
         for s in range(0, w_cols, LANES)], axis=1)
    kap_ref[...] = kap.astype(kap_ref.dtype)
    outs = ((lwf_ref, bf_ref, ktf_ref), (lwb_ref, bb_ref, ktb_ref))
    for d in range(2):
        lw_ref, b_ref, kt_ref = outs[d]
        lw_ref[...] = -W_DECAY_SCALE * _sigmoid(w0_ref[d:d + 1, :] + _dot(wd.astype(BF16), wup_ref[d]))
        a = _sigmoid(a0_ref[d:d + 1, :] + _dot(ad.astype(BF16), aup_ref[d]))
        kt_ref[...] = (k * (1.0 + (a - 1.0) * ka_ref[...])).astype(kt_ref.dtype)
        b_ref[...] = (a * kap).astype(b_ref.dtype)


def _seg_ones(width=LANES, seg=RWKV_HEAD):
    i = np.arange(width)
    return jnp.asarray((i[:, None] // seg == i[None, :] // seg).astype(np.float32)).astype(BF16)


def rwkv_features(zr, n_ctx, mu, w0, w_up, a0, a_up, g_up, k_k, k_a, tm=256):
    N, zw = zr.shape
    wc = k_k.shape[0]
    assert zw == 3 * wc + 3 * LANES and W_RANK + W_RANK == LANES and A_RANK + A_RANK == LANES and G_RANK == LANES
    zeros = jnp.zeros((W_RANK, wc), F32)
    wup = jnp.stack([jnp.concatenate([w_up[0], zeros], 0), jnp.concatenate([zeros, w_up[1]], 0)], 0)
    aup = jnp.stack([jnp.concatenate([a_up[0], zeros], 0), jnp.concatenate([zeros, a_up[1]], 0)], 0)
    n_tiles = N // tm
    t8 = tm // 8
    row = lambda i: (i, 0)
    full = lambda i: (0, 0)
    sds = lambda dt: jax.ShapeDtypeStruct((N, wc), dt)
    return pl.pallas_call(
        functools.partial(_rwkv_feat_kernel, n_ctx // tm, n_tiles, wc),
        grid=(n_tiles,),
        in_specs=[pl.BlockSpec((tm, zw), row),
                  pl.BlockSpec((8, zw), lambda i: (jnp.maximum(i * t8 - 1, 0), 0)),
                  pl.BlockSpec((8, zw), lambda i: (jnp.minimum((i + 1) * t8, N // 8 - 1), 0)),
                  pl.BlockSpec((2, zw), full), pl.BlockSpec((1, wc), full), pl.BlockSpec((1, wc), full),
                  pl.BlockSpec((2, wc), full), pl.BlockSpec((2, wc), full),
                  pl.BlockSpec((2, LANES, wc), lambda i: (0, 0, 0)), pl.BlockSpec((2, LANES, wc), lambda i: (0, 0, 0)),
                  pl.BlockSpec((LANES, wc), full), pl.BlockSpec((LANES, LANES), full)],
        out_specs=[pl.BlockSpec((tm, wc), row)] * 10,
        out_shape=[sds(BF16)] * 4 + [sds(F32), sds(BF16), sds(BF16)] * 2,
        scratch_shapes=[pltpu.VMEM((tm + 16, zw), F32)],
        compiler_params=_params(("parallel",)), name="rwkv_features",
    )(zr, zr, zr, mu, k_k.reshape(1, wc), k_a.reshape(1, wc), w0, a0,
      wup.astype(BF16), aup.astype(BF16), g_up.astype(BF16), _seg_ones())


def _rwkv_scan_kernel(rev, pairs, n_sub, r_ref, kap_ref, v_ref, lw_ref, b_ref, kt_ref, y_ref, s_ref):
    L = CHUNK
    L2 = 2 * L

    @pl.when(pl.program_id(1) == 0)
    def _():
        s_ref[...] = jnp.zeros(s_ref.shape, F32)

    t_i = lax.broadcasted_iota(jnp.int32, (L, L), 0)
    s_i = lax.broadcasted_iota(jnp.int32, (L, L), 1)
    m_incl64 = ((s_i >= t_i) if rev else (s_i <= t_i)).astype(F32)
    ri = lax.broadcasted_iota(jnp.int32, (L2, L2), 0)
    qi = lax.broadcasted_iota(jnp.int32, (L2, L2), 1)
    rt, qt = ri % L, qi % L
    same = (ri // L) == (qi // L)
    incl = jnp.logical_and(same, (qt >= rt) if rev else (qt <= rt))
    strict = jnp.logical_and(same, (qt > rt) if rev else (qt < rt))
    eye = (ri == qi).astype(F32)
    lane = lax.broadcasted_iota(jnp.int32, (L, LANES), 1)
    hm = [(lane < RWKV_HEAD).astype(F32), (lane >= RWKV_HEAD).astype(F32)]
    bd = ((lax.broadcasted_iota(jnp.int32, (LANES, LANES), 0) // RWKV_HEAD)
          == (lax.broadcasted_iota(jnp.int32, (LANES, LANES), 1) // RWKV_HEAD)).astype(F32)
    lvl_masks = []
    bsz = 1
    while bsz < L:
        grp = (ri // (2 * bsz)) == (qi // (2 * bsz))
        r_odd = (ri // bsz) % 2 == 1
        q_odd = (qi // bsz) % 2 == 1
        off = jnp.logical_and(jnp.logical_not(r_odd), q_odd) if rev else jnp.logical_and(r_odd, jnp.logical_not(q_odd))
        lvl_masks.append(jnp.logical_and(grp, off))
        bsz *= 2

    def stack2(x):
        return jnp.concatenate([x * hm[0], x * hm[1]], axis=0)

    def bmm(a, b):
        return _dot(a.astype(BF16), b.astype(BF16))

    m_incl_bf = m_incl64.astype(BF16)

    units = [(ci, p) for ci in range(n_sub) for p in range(pairs)]

    def blk(ref, unit):
        ci, p = unit
        cc = (n_sub - 1 - ci) if rev else ci
        return ref[cc * L:(cc + 1) * L, p * LANES:(p + 1) * LANES]

    def each(fn, *dicts):
        return {un: fn(*(d[un] for d in dicts)) for un in units}

    def split3(x):
        hi = x.astype(BF16)
        r1 = x - hi.astype(F32)
        mid = r1.astype(BF16)
        lo = (r1 - mid.astype(F32)).astype(BF16)
        return jnp.concatenate([hi, mid, lo], axis=1)

    lw = {un: blk(lw_ref, un) for un in units}
    c3 = each(lambda x: _dot(m_incl_bf, split3(x)), lw)
    cum = each(lambda c: c[:, 0:LANES] + c[:, LANES:2 * LANES] + c[:, 2 * LANES:], c3)
    e_incl = each(jnp.exp, cum)
    e_inv = each(lambda c: jnp.exp(-c), cum)
    p_tot = each(lambda e: e[0:1, :] if rev else e[L - 1:L, :], e_incl)
    xk = {un: stack2(blk(kap_ref, un) * jnp.exp(cum[un] - lw[un])).astype(BF16) for un in units}
    xr = {un: stack2(blk(r_ref, un) * e_incl[un]).astype(BF16) for un in units}
    yb = {un: stack2(blk(b_ref, un) * e_inv[un]).astype(BF16) for un in units}
    yk = {un: stack2(blk(kt_ref, un) * e_inv[un]).astype(BF16) for un in units}
    vs = {un: stack2(blk(v_ref, un)).astype(BF16) for un in units}
    amat = each(lambda a, b, c, d: _dot_nt(jnp.concatenate([a, b], axis=0), jnp.concatenate([c, d], axis=0)),
                xk, xr, yb, yk)
    a_ub = each(lambda a: jnp.where(strict, a[0:L2, 0:L2], 0.0), amat)
    a_uk = each(lambda a: jnp.where(strict, a[0:L2, L2:], 0.0).astype(BF16), amat)
    a_rb = each(lambda a: jnp.where(incl, a[L2:, 0:L2], 0.0).astype(BF16), amat)
    a_rk = each(lambda a: jnp.where(incl, a[L2:, L2:], 0.0).astype(BF16), amat)
    tinv = each(lambda a: eye - jnp.where(lvl_masks[0], a, 0.0), a_ub)
    for lm in lvl_masks[1:]:
        tb = each(lambda t: t.astype(BF16), tinv)
        x1 = each(lambda a, t: _dot(jnp.where(lm, a, 0.0).astype(BF16), t).astype(BF16), a_ub, tb)
        tinv = each(lambda t, tbf, x: t - _dot(tbf, x), tinv, tb, x1)
    tb = each(lambda t: t.astype(BF16), tinv)
    w1 = each(lambda t, x: _dot(t, x).astype(BF16), tb, xk)
    avs = each(lambda a, v: _dot(a, v).astype(BF16), a_uk, vs)
    y_ind = each(_dot, a_rk, vs)
    k2 = each(_dot_tn, vs, yk)
    w2 = each(_dot, tb, avs)

    state = [s_ref[p] for p in range(pairs)]
    for ci in range(n_sub):
        wx = [_dot_nt(jnp.concatenate([w1[(ci, p)], xr[(ci, p)]], axis=0), state[p].astype(BF16))
              for p in range(pairs)]
        ub = [(-(wx[p][0:L2] + w2[(ci, p)])).astype(BF16) for p in range(pairs)]
        upd = [_dot_tn(ub[p], yb[(ci, p)]) + k2[(ci, p)] for p in range(pairs)]
        state = [(state[p] + upd[p] * bd) * p_tot[(ci, p)] for p in range(pairs)]
        cc = (n_sub - 1 - ci) if rev else ci
        for p in range(pairs):
            ystk = wx[p][L2:] + _dot(a_rb[(ci, p)], ub[p]) + y_ind[(ci, p)]
            y_ref[cc * L:(cc + 1) * L, p * LANES:(p + 1) * LANES] = ystk[0:L] + ystk[L:]
    for p in range(pairs):
        s_ref[p] = state[p]


def rwkv_scan(rev, n_ctx, r, kap, v, lw, b, kt, pairs=4, n_sub=4):
    N, wc = r.shape
    rb = n_sub * CHUNK
    bw = pairs * LANES
    assert N % rb == 0 and n_ctx % rb == 0 and wc % bw == 0
    nb, nbc = N // rb, n_ctx // rb
    if rev:
        rmap = lambda h, i: (jnp.where(i < nbc, nbc - 1 - i, nb - 1 - (i - nbc)), h)
    else:
        rmap = lambda h, i: (i, h)
    spec = pl.BlockSpec((rb, bw), rmap)
    return pl.pallas_call(
        functools.partial(_rwkv_scan_kernel, rev, pairs, n_sub),
        grid=(wc // bw, nb),
        in_specs=[spec] * 6, out_specs=spec,
        out_shape=jax.ShapeDtypeStruct((N, wc), F32),
        scratch_shapes=[pltpu.VMEM((pairs, LANES, LANES), F32)],
        compiler_params=_params(("parallel", "arbitrary")), name="rwkv_scan_rev" if rev else "rwkv_scan_fwd",
    )(r, kap, v, lw, b, kt)


def _rwkv_out_kernel(yf_ref, yb_ref, r_ref, v_ref, ktf_ref, ktb_ref, g_ref, rk_ref, gn_ref, e_ref, o_ref):
    e128 = e_ref[...]
    inv = 1.0 / RWKV_HEAD
    for s in range(0, o_ref.shape[1], LANES):
        sl = slice(s, s + LANES)
        y = yf_ref[:, sl] + yb_ref[:, sl]
        mu = _seg_sum64(y, e128) * inv
        yc = y - mu
        var = _seg_sum64(yc * yc, e128) * inv
        yn = yc * lax.rsqrt(var + GN_EPS) * gn_ref[0:1, sl] + gn_ref[1:2, sl]
        rk = r_ref[:, sl] * rk_ref[0:1, sl]
        bonus = _seg_sum64(rk * (ktf_ref[:, sl].astype(F32) + ktb_ref[:, sl]), e128) * v_ref[:, sl]
        o_ref[:, sl] = ((yn + bonus) * g_ref[:, sl]).astype(o_ref.dtype)


def rwkv_output(y_f, y_b, r, v, kt_f, kt_b, g, r_k, lnx_gain, lnx_bias, tm=256):
    N, wc = r.shape
    row = lambda i: (i, 0)
    full = lambda i: (0, 0)
    gn = jnp.stack([lnx_gain, lnx_bias], 0)
    return pl.pallas_call(
        _rwkv_out_kernel, grid=(N // tm,),
        in_specs=[pl.BlockSpec((tm, wc), row)] * 7 + [pl.BlockSpec((1, wc), full), pl.BlockSpec((2, wc), full),
                                                      pl.BlockSpec((LANES, LANES), full)],
        out_specs=pl.BlockSpec((tm, wc), row),
        out_shape=jax.ShapeDtypeStruct((N, wc), BF16),
        compiler_params=_params(("parallel",)), name="rwkv_output",
    )(y_f, y_b, r, v, kt_f, kt_b, g, r_k.reshape(1, wc), gn, _seg_ones())


def _angles(i, j, period):
    return (2.0 * math.pi / period) * ((i * j) % period).astype(np.float64)


def _chan_dft(gdim, scale):
    i = np.arange(gdim)
    ang = _angles(i[:, None], i[None, :], gdim)
    return jnp.asarray(np.concatenate([np.cos(ang), -np.sin(ang)], 1) * scale, F32)


def _fnet1_kernel(groups, x_ref, cs_ref, m_ref, y_ref):
    r1 = x_ref.shape[0]
    gdim = x_ref.shape[1] // groups
    for g in range(groups):
        sl = slice(g * gdim, (g + 1) * gdim)
        ab = _dot3(x_ref[:, sl], cs_ref[...])
        st = jnp.concatenate([ab[:, :gdim], ab[:, gdim:]], axis=0)
        y = _dot3(m_ref[0], st)
        y_ref[0, 0, :, sl] = y[:r1]
        y_ref[1, 0, :, sl] = y[r1:]


def _fnet3_kernel(cs_ref, y_ref, o_ref):
    o_ref[...] = _dot3(cs_ref[...], y_ref[...])


def fnet_latent(zf, groups=FOURIER_GROUPS):
    T, fw = zf.shape
    gdim = fw // groups
    r2 = GRID_W
    r1 = T // r2
    cs = _chan_dft(gdim, 1.0 / math.sqrt(T * gdim))
    t1p = np.arange(r1)[None, :, None]
    t1 = np.arange(r1)[None, None, :]
    t2 = np.arange(r2)[:, None, None]
    th = _angles(t1p, r2 * t1 + t2, T)
    gr, gi = np.cos(th), -np.sin(th)
    mt = jnp.asarray(np.concatenate([np.concatenate([gr, -gi], 2), np.concatenate([gi, gr], 2)], 1), F32)
    y = pl.pallas_call(
        functools.partial(_fnet1_kernel, groups),
        grid=(r2,),
        in_specs=[pl.BlockSpec((r1, fw), lambda j: (0, j)),
                  pl.BlockSpec((gdim, 2 * gdim), lambda j: (0, 0)),
                  pl.BlockSpec((1, 2 * r1, 2 * r1), lambda j: (j, 0, 0))],
        out_specs=pl.BlockSpec((2, 1, r1, fw), lambda j: (0, j, 0, 0)),
        out_shape=jax.ShapeDtypeStruct((2, r2, r1, fw), F32),
        compiler_params=_params(("parallel",)), name="fnet_rows",
    )(zf.reshape(r1, r2 * fw), cs, mt)
    i2 = np.arange(r2)
    ph = _angles(i2[:, None], i2[None, :], r2)
    cs2 = jnp.asarray(np.concatenate([np.cos(ph), np.sin(ph)], 1), F32)
    tc = 4 * fw if r1 % 4 == 0 else fw
    o = pl.pallas_call(
        _fnet3_kernel, grid=(r1 * fw // tc,),
        in_specs=[pl.BlockSpec((r2, 2 * r2), lambda j: (0, 0)), pl.BlockSpec((2 * r2, tc), lambda j: (0, j))],
        out_specs=pl.BlockSpec((r2, tc), lambda j: (0, j)),
        out_shape=jax.ShapeDtypeStruct((r2, r1 * fw), F32),
        compiler_params=_params(("parallel",)), name="fnet_cols",
    )(cs2, y.reshape(2 * r2, r1 * fw))
    return o.reshape(T, fw)


def _fnet_direct_kernel(x_ref, cs_ref, ct_ref, o_ref):
    gdim = x_ref.shape[1]
    ab = _dot3(x_ref[...], cs_ref[...])
    st = jnp.concatenate([ab[:, :gdim], ab[:, gdim:]], axis=0)
    o_ref[...] = _dot3(ct_ref[...], st)


def fnet_direct(zf, groups=FOURIER_GROUPS):
    C, fw = zf.shape
    gdim = fw // groups
    cs = _chan_dft(gdim, 1.0 / math.sqrt(C * gdim))
    i = np.arange(C)
    ang = _angles(i[:, None], i[None, :], C)
    ct = jnp.asarray(np.concatenate([np.cos(ang), np.sin(ang)], 1), F32)
    return pl.pallas_call(
        _fnet_direct_kernel, grid=(groups,),
        in_specs=[pl.BlockSpec((C, gdim), lambda g: (0, g)), pl.BlockSpec((gdim, 2 * gdim), lambda g: (0, 0)),
                  pl.BlockSpec((C, 2 * C), lambda g: (0, 0))],
        out_specs=pl.BlockSpec((C, gdim), lambda g: (0, g)),
        out_shape=jax.ShapeDtypeStruct((C, fw), F32),
        compiler_params=_params(("parallel",)), name="fnet_direct",
    )(zf, cs, ct)


GATE_LANE0 = N_GROUPS


def _gates_kernel(lg_ref, id_ref, gv_ref):
    lg = lg_ref[...]
    lane = lax.broadcasted_iota(jnp.int32, lg.shape, 1)
    big = jnp.int32(LANES)
    neg = -jnp.inf
    is_g = lane < N_GROUPS
    gl = jnp.where(is_g, lg, neg)
    mg = jnp.max(gl, axis=-1, keepdims=True)
    p_group = 1.0 / jnp.sum(jnp.exp(gl - mg), axis=-1, keepdims=True)
    gsel = jnp.min(jnp.where(gl == mg, lane, big), axis=-1, keepdims=True)
    e_idx = lane - GATE_LANE0
    in_grp = jnp.logical_and(jnp.logical_and(e_idx >= 0, e_idx < N_EXPERTS), e_idx // EXPERTS_PER_GROUP == gsel)
    le = jnp.where(in_grp, lg, neg)
    m1 = jnp.max(le, axis=-1, keepdims=True)
    l1 = jnp.min(jnp.where(le == m1, lane, big), axis=-1, keepdims=True)
    le2 = jnp.where(lane == l1, neg, le)
    m2 = jnp.max(le2, axis=-1, keepdims=True)
    l2 = jnp.min(jnp.where(le2 == m2, lane, big), axis=-1, keepdims=True)
    e2 = jnp.exp(m2 - m1)
    inv = p_group / (1.0 + e2)
    id_ref[...] = jnp.where(lane == 0, l1 - GATE_LANE0, jnp.where(lane == 1, l2 - GATE_LANE0, 0))
    gv_ref[...] = jnp.where(lane == 0, inv, jnp.where(lane == 1, e2 * inv, 0.0))


def moe_gates(logits, tm=256):
    N = logits.shape[0]
    spec = pl.BlockSpec((tm, LANES), lambda i: (i, 0))
    return pl.pallas_call(
        _gates_kernel, grid=(N // tm,),
        in_specs=[spec], out_specs=[spec, spec],
        out_shape=[jax.ShapeDtypeStruct((N, LANES), jnp.int32), jax.ShapeDtypeStruct((N, LANES), F32)],
        compiler_params=_params(("parallel",)), name="moe_gates",
    )(logits)


def moe_dispatch(expert_ids, ts):
    N = expert_ids.shape[0]
    flat = expert_ids.reshape(-1)
    onehot = (flat[:, None] == jnp.arange(N_EXPERTS, dtype=jnp.int32)[None, :]).astype(jnp.int32)
    csum = jnp.cumsum(onehot, axis=0)
    rank = jnp.take_along_axis(csum, flat[:, None], axis=1)[:, 0] - 1
    counts = csum[-1]
    tiles_per = (counts + ts - 1) // ts
    tiles_end = jnp.cumsum(tiles_per)
    pad_start = (tiles_end - tiles_per) * ts
    slot = pad_start[flat] + rank
    n_tiles = -(-2 * N // ts) + N_EXPERTS
    slot_token = jnp.zeros((n_tiles * ts,), jnp.int32).at[slot].set(jnp.arange(2 * N, dtype=jnp.int32) // 2)
    n_used = tiles_end[-1]
    tile_idx = jnp.minimum(jnp.arange(n_tiles, dtype=jnp.int32), n_used - 1)
    tile_expert = jnp.sum((tile_idx[:, None] >= tiles_end[None, :]).astype(jnp.int32), axis=1)
    return slot_token, slot.reshape(N, 2), tile_expert.astype(jnp.int32), n_used.reshape(1).astype(jnp.int32)


def _moe_group_kernel(ts, n_chunks, te_ref, nu_ref, st_ref, u_hbm, w1_ref, w3_ref, w2_ref, o_ref, ubuf, ub16, sem):
    i = pl.program_id(0)
    j = pl.program_id(1)
    used = nu_ref[0]
    share = ts // n_chunks

    def row_copy(slot, b, r):
        return pltpu.make_async_copy(u_hbm.at[pl.ds(st_ref[slot], 1), :], ubuf.at[b, pl.ds(r, 1), :], sem.at[b])

    @pl.when(jnp.logical_and(j == 0, i == 0))
    def _():
        def row(r, carry):
            row_copy(r, 0, r).start()
            return carry
        lax.fori_loop(0, ts, row, 0)

    @pl.when(jnp.logical_and(j == 0, i <= used))
    def _():
        pltpu.make_async_copy(u_hbm.at[pl.ds(0, ts), :], ubuf.at[i % 2], sem.at[i % 2]).wait()

    @pl.when(jnp.logical_and(j == 0, i < used))
    def _():
        ub16[...] = ubuf[i % 2].astype(BF16)

    @pl.when(j == 0)
    def _():
        o_ref[...] = jnp.zeros(o_ref.shape, o_ref.dtype)

    @pl.when(i < used)
    def _():
        nb = (i + 1) % 2
        r0 = j * share
        for r in range(share):
            row_copy((i + 1) * ts + r0 + r, nb, r0 + r).start()
        u = ub16[...]
        a = _dot(u, w1_ref[0, 0])
        g = _dot(u, w3_ref[0, 0])
        h = ((a * _sigmoid(a)) * g).astype(BF16)
        nw = o_ref.shape[1] // 4
        for n in range(0, o_ref.shape[1], nw):
            o_ref[:, n:n + nw] += _dot(h, w2_ref[0, 0, :, n:n + nw])


def moe_grouped(u, slot_token, tile_expert, n_used, w1, w3, w2, layer, ts, tc=256):
    N, D = u.shape
    _, E, _, De = w1.shape
    n_tiles = tile_expert.shape[0]
    grid_spec = pltpu.PrefetchScalarGridSpec(
        num_scalar_prefetch=3,
        grid=(n_tiles, De // tc),
        in_specs=[pl.BlockSpec(memory_space=pl.ANY),
                  pl.BlockSpec((1, 1, D, tc), lambda i, j, te, nu, st: (layer, te[i], 0, j)),
                  pl.BlockSpec((1, 1, D, tc), lambda i, j, te, nu, st: (layer, te[i], 0, j)),
                  pl.BlockSpec((1, 1, tc, D), lambda i, j, te, nu, st: (layer, te[i], j, 0))],
        out_specs=pl.BlockSpec((ts, D), lambda i, j, te, nu, st: (i, 0)),
        scratch_shapes=[pltpu.VMEM((2, ts, D), F32), pltpu.VMEM((ts, D), BF16), pltpu.SemaphoreType.DMA((2,))],
    )
    return pl.pallas_call(
        functools.partial(_moe_group_kernel, ts, De // tc), grid_spec=grid_spec,
        out_shape=jax.ShapeDtypeStruct((n_tiles * ts, D), F32),
        compiler_params=_params(("arbitrary", "arbitrary")), name="moe_grouped",
    )(tile_expert, n_used, slot_token, u, w1, w3, w2)


def _moe_combine_kernel(alpha, n_ctx_tiles, sa_ref, sb_ref, o_hbm, x_ref, gv_ref, g_ref, gb_ref, out_ref,
                        abuf, bbuf, sem):
    i = pl.program_id(0)
    tm = x_ref.shape[0]

    def gather(tile, b):
        def row(r, carry):
            t = tile * tm + r
            pltpu.make_async_copy(o_hbm.at[pl.ds(sa_ref[t], 1), :], abuf.at[b, pl.ds(r, 1), :], sem.at[0, b]).start()
            pltpu.make_async_copy(o_hbm.at[pl.ds(sb_ref[t], 1), :], bbuf.at[b, pl.ds(r, 1), :], sem.at[1, b]).start()
            return carry
        lax.fori_loop(0, tm, row, 0)

    @pl.when(i == 0)
    def _():
        gather(0, 0)

    @pl.when(i + 1 < pl.num_programs(0))
    def _():
        gather(i + 1, (i + 1) % 2)

    b = i % 2
    pltpu.make_async_copy(o_hbm.at[pl.ds(0, tm), :], abuf.at[b], sem.at[0, b]).wait()
    pltpu.make_async_copy(o_hbm.at[pl.ds(0, tm), :], bbuf.at[b], sem.at[1, b]).wait()
    gv = gv_ref[...]
    f = gv[:, 0:1] * abuf[b] + gv[:, 1:2] * bbuf[b]
    g = jnp.where(i < n_ctx_tiles, g_ref[1:2, :], g_ref[0:1, :])
    y = alpha * x_ref[...] + g * f
    out_ref[...] = _ln_rows(y, LN_EPS) * gb_ref[0:1, :] + gb_ref[1:2, :]


def moe_combine_ln(xx, o, slots, gate_vals, gates2, gain_bias, n_ctx, alpha, tm=256):
    N, D = xx.shape
    grid_spec = pltpu.PrefetchScalarGridSpec(
        num_scalar_prefetch=2,
        grid=(N // tm,),
        in_specs=[pl.BlockSpec(memory_space=pl.ANY),
                  pl.BlockSpec((tm, D), lambda i, sa, sb: (i, 0)),
                  pl.BlockSpec((tm, LANES), lambda i, sa, sb: (i, 0)),
                  pl.BlockSpec((2, D), lambda i, sa, sb: (0, 0)), pl.BlockSpec((2, D), lambda i, sa, sb: (0, 0))],
        out_specs=pl.BlockSpec((tm, D), lambda i, sa, sb: (i, 0)),
        scratch_shapes=[pltpu.VMEM((2, tm, D), F32), pltpu.VMEM((2, tm, D), F32), pltpu.SemaphoreType.DMA((2, 2))],
    )
    return pl.pallas_call(
        functools.partial(_moe_combine_kernel, alpha, n_ctx // tm), grid_spec=grid_spec,
        out_shape=jax.ShapeDtypeStruct((N, D), F32),
        compiler_params=_params(("arbitrary",)), name="moe_combine_ln",
    )(slots[:, 0], slots[:, 1], o, xx, gate_vals, gates2, gain_bias)


MOE_SLOT_TILE = 512

def _pick_tile(n, prefer):
    for t in prefer:
        if n % t == 0:
            return t
    raise ValueError(f"no tile for {n}")


def kernel(x, c, ctx, c_ctx, w_mod, b_mod, w_in, q_gain, k_gain, rwkv_mu, w0, w_up, a0, a_up, g_up, k_k, k_a, r_k, lnx_gain, lnx_bias, w_out, ln1_gain, ln1_bias, ln2_gain, ln2_bias, router_group_w, router_group_b, router_expert_w, router_expert_b, w1, w3, w2):
    B, T, D = x.shape
    C = ctx.shape[1]
    assert B == 1
    depth = w_mod.shape[0]
    alpha = (2 * depth) ** 0.25
    N = C + T
    aw = D // 2
    kvw = KV_HEADS * HEAD_DIM
    rw = D // 4
    fw = D // 4
    rcols = 3 * rw + 2 * W_RANK + 2 * A_RANK + G_RANK
    o_zr = aw + 2 * kvw
    o_zf = o_zr + rcols
    tm_big = _pick_tile(N, (768, 384, 256))

    xx = jnp.concatenate([ctx[0], x[0]], 0)
    mods = mod_vectors(c, c_ctx, w_mod, b_mod).reshape(depth, 8, 6, D)
    cosf, sinf = rope_tables(C, T)
    w_qkv = w_in[:, :, :o_zr].astype(BF16)
    w_rwkv = w_in[:, :, o_zr:o_zf].astype(BF16)
    w_four = w_in[:, :, o_zf:].astype(BF16)
    w_out_b = w_out.astype(BF16)
    w1_b, w3_b, w2_b = w1.astype(BF16), w3.astype(BF16), w2.astype(BF16)

    for l in range(depth):
        last = l == depth - 1
        mv = mods[l]
        vec = lambda i: jnp.stack([mv[0, i], mv[1, i]], 0)
        ss1 = jnp.stack([mv[0, 0], mv[0, 1], mv[1, 0], mv[1, 1]], 0)
        ss2 = jnp.stack([mv[0, 3], mv[0, 4], mv[1, 3], mv[1, 4]], 0)

        u = ln_modulate(xx, ss1, C)
        zq = matmul(u, w_qkv, l, F32, tm_big, _pick_tile(o_zr, (768, 512, 256, 128)))
        zr = matmul(u, w_rwkv, l, F32, tm_big, _pick_tile(rcols, (1152, 384, 128)))
        zf = matmul(u, w_four, l, F32, tm_big, _pick_tile(fw, (512, 256, 128)))
        qh, kh, vh = qkv_prep(zq, cosf, sinf, q_gain[l], k_gain[l], aw)
        attn = attention(qh, kh, vh, C)
        r, v, kap, g, lwf, bf, ktf, lwb, bb, ktb = rwkv_features(
            zr, C, rwkv_mu[l], w0[l], w_up[l], a0[l], a_up[l], g_up[l], k_k[l], k_a[l])
        y_f = rwkv_scan(False, C, r, kap, v, lwf, bf, ktf)
        y_b = rwkv_scan(True, C, r, kap, v, lwb, bb, ktb)
        rwkv = rwkv_output(y_f, y_b, r, v, ktf, ktb, g, r_k[l].reshape(-1), lnx_gain[l], lnx_bias[l])
        fl = fnet_latent(zf[C:])
        fc = fnet_direct(zf[:C]) if not last else jnp.zeros((C, fw), F32)
        fn = jnp.concatenate([fc, fl], 0).astype(BF16)
        cat = jnp.concatenate([attn, rwkv, fn], 1)
        m = matmul(cat, w_out_b, l, F32, tm_big, _pick_tile(D, (512, 256, 128)))
        xx = resid_ln(xx, m, vec(2), jnp.stack([ln1_gain[l], ln1_bias[l]], 0), C, alpha)

        wr = jnp.concatenate([router_group_w[l], router_expert_w[l],
                              jnp.zeros((D, LANES - N_GROUPS - N_EXPERTS), F32)], 1)
        br = jnp.concatenate([router_group_b[l], router_expert_b[l],
                              jnp.zeros((LANES - N_GROUPS - N_EXPERTS,), F32)], 0).reshape(1, LANES)
        u2, logits = ln_modulate(xx, ss2, C, router=(wr, br), out_dtype=F32)
        expert_ids, gate_vals = moe_gates(logits)
        slot_token, slots, tile_expert, n_used = moe_dispatch(expert_ids[:, :2], MOE_SLOT_TILE)
        o = moe_grouped(u2, slot_token, tile_expert, n_used, w1_b, w3_b, w2_b, l, MOE_SLOT_TILE)
        xx = moe_combine_ln(xx, o, slots, gate_vals, vec(5), jnp.stack([ln2_gain[l], ln2_bias[l]], 0), C, alpha)
    return xx[C:][None]
```

```python
import functools
import math

import numpy as np
import jax
import jax.numpy as jnp
from jax import lax
from jax.experimental import pallas as pl
from jax.experimental.pallas import tpu as pltpu

F32 = jnp.float32
BF16 = jnp.bfloat16

GRID_W = 64
HEAD_DIM = 128
KV_HEADS = 4
RWKV_HEAD = 64
W_RANK = 64
A_RANK = 64
G_RANK = 128
FOURIER_GROUPS = 4
N_GROUPS = 4
EXPERTS_PER_GROUP = 4
N_EXPERTS = N_GROUPS * EXPERTS_PER_GROUP
ROPE_THETA = 10000.0
W_DECAY_SCALE = math.exp(-0.5)
GN_EPS = 64e-5
LN_EPS = 1e-6
LANES = 128
CHUNK = 64

V7X_VMEM_LIMIT_MB = 56


def _params(sem, vmem_mb=V7X_VMEM_LIMIT_MB):
    return pltpu.CompilerParams(dimension_semantics=sem, vmem_limit_bytes=vmem_mb * 1024 * 1024)


def _dot(a, b, prec=None):
    return jnp.dot(a, b, preferred_element_type=F32, precision=prec)


def _dot_nt(a, b, prec=None):
    return lax.dot_general(a, b, (((1,), (1,)), ((), ())), preferred_element_type=F32, precision=prec)


def _dot_tn(a, b, prec=None):
    return lax.dot_general(a, b, (((0,), (0,)), ((), ())), preferred_element_type=F32, precision=prec)


def _sigmoid(x):
    return 1.0 / (1.0 + jnp.exp(-x))


def _split2(x):
    hi = x.astype(BF16)
    return hi, (x - hi.astype(F32)).astype(BF16)


def _dot3(a, b):
    a_hi, a_lo = _split2(a)
    b_hi, b_lo = _split2(b)
    return _dot(a_hi, b_hi) + (_dot(a_hi, b_lo) + _dot(a_lo, b_hi))


def _dot_exact_rhs(x, e):
    hi = x.astype(BF16)
    r1 = x - hi.astype(F32)
    mid = r1.astype(BF16)
    lo = (r1 - mid.astype(F32)).astype(BF16)
    return _dot(hi, e) + (_dot(mid, e) + _dot(lo, e))


def _mod_kernel(a_ref, w_ref, b_ref, o_ref):
    tn = o_ref.shape[2]
    rows = []
    for r in range(2):
        cols = []
        for j in range(tn // LANES):
            prod = w_ref[0, :, j * LANES:(j + 1) * LANES] * a_ref[r]
            cols.append(jnp.sum(prod, axis=0, keepdims=True))
        rows.append(jnp.concatenate(cols, axis=1) + b_ref[0])
    rows.append(jnp.zeros((6, tn), F32))
    o_ref[0] = jnp.concatenate(rows, axis=0)


def mod_vectors(c, c_ctx, w_mod, b_mod):
    L, D, D6 = w_mod.shape
    acts = jnp.stack([jax.nn.silu(c[0]), jax.nn.silu(c_ctx)], 0)
    a_b = jnp.broadcast_to(acts[:, :, None], (2, D, LANES))
    tn = 512
    return pl.pallas_call(
        _mod_kernel,
        grid=(L, D6 // tn),
        in_specs=[pl.BlockSpec((2, D, LANES), lambda l, j: (0, 0, 0)),
                  pl.BlockSpec((1, D, tn), lambda l, j: (l, 0, j)),
                  pl.BlockSpec((1, 1, tn), lambda l, j: (l, 0, j))],
        out_specs=pl.BlockSpec((1, 8, tn), lambda l, j: (l, 0, j)),
        out_shape=jax.ShapeDtypeStruct((L, 8, D6), F32),
        compiler_params=_params(("parallel", "parallel")),
        name="mod_vectors",
    )(a_b, w_mod, b_mod.reshape(L, 1, D6))


def _ln_rows(x, eps):
    mu = jnp.mean(x, axis=-1, keepdims=True)
    xc = x - mu
    var = jnp.mean(xc * xc, axis=-1, keepdims=True)
    return xc * lax.rsqrt(var + eps)


def _ln_mod_kernel(n_ctx_tiles, with_router, x_ref, ss_ref, *rest):
    if with_router:
        wr_ref, br_ref, u_ref, lg_ref = rest
    else:
        (u_ref,) = rest
    is_ctx = pl.program_id(0) < n_ctx_tiles
    sh = jnp.where(is_ctx, ss_ref[2:3, :], ss_ref[0:1, :])
    sc = jnp.where(is_ctx, ss_ref[3:4, :], ss_ref[1:2, :])
    u = _ln_rows(x_ref[...], LN_EPS) * (1.0 + sc) + sh
    u_ref[...] = u.astype(u_ref.dtype)
    if with_router:
        lg_ref[...] = _dot3(u, wr_ref[...]) + br_ref[...]


def ln_modulate(xx, ss, n_ctx, router=None, out_dtype=BF16, tm=256):
    N, D = xx.shape
    assert N % tm == 0 and n_ctx % tm == 0
    in_specs = [pl.BlockSpec((tm, D), lambda i: (i, 0)), pl.BlockSpec((4, D), lambda i: (0, 0))]
    out_specs = [pl.BlockSpec((tm, D), lambda i: (i, 0))]
    out_shape = [jax.ShapeDtypeStruct((N, D), out_dtype)]
    args = [xx, ss]
    if router is not None:
        in_specs += [pl.BlockSpec((D, LANES), lambda i: (0, 0)), pl.BlockSpec((1, LANES), lambda i: (0, 0))]
        out_specs.append(pl.BlockSpec((tm, LANES), lambda i: (i, 0)))
        out_shape.append(jax.ShapeDtypeStruct((N, LANES), F32))
        args += list(router)
    res = pl.pallas_call(
        functools.partial(_ln_mod_kernel, n_ctx // tm, router is not None),
        grid=(N // tm,), in_specs=in_specs, out_specs=out_specs, out_shape=out_shape,
        compiler_params=_params(("parallel",)), name="ln_modulate",
    )(*args)
    return res if router is not None else res[0]


def _resid_ln_kernel(alpha, n_ctx_tiles, x_ref, m_ref, g_ref, gb_ref, o_ref):
    is_ctx = pl.program_id(0) < n_ctx_tiles
    g = jnp.where(is_ctx, g_ref[1:2, :], g_ref[0:1, :])
    y = alpha * x_ref[...] + g * m_ref[...].astype(F32)
    o_ref[...] = _ln_rows(y, LN_EPS) * gb_ref[0:1, :] + gb_ref[1:2, :]


def resid_ln(xx, m, gates2, gain_bias, n_ctx, alpha, tm=256):
    N, D = xx.shape
    return pl.pallas_call(
        functools.partial(_resid_ln_kernel, alpha, n_ctx // tm),
        grid=(N // tm,),
        in_specs=[pl.BlockSpec((tm, D), lambda i: (i, 0)), pl.BlockSpec((tm, D), lambda i: (i, 0)),
                  pl.BlockSpec((2, D), lambda i: (0, 0)), pl.BlockSpec((2, D), lambda i: (0, 0))],
        out_specs=pl.BlockSpec((tm, D), lambda i: (i, 0)),
        out_shape=jax.ShapeDtypeStruct((N, D), F32),
        compiler_params=_params(("parallel",)), name="resid_ln",
    )(xx, m, gates2, gain_bias)


def _mm_kernel(a_ref, b_ref, o_ref):
    o_ref[...] = _dot(a_ref[...], b_ref[0]).astype(o_ref.dtype)


def matmul(a, b, layer, out_dtype, tm, tn):
    M, K = a.shape
    _, _, Nn = b.shape
    assert M % tm == 0 and Nn % tn == 0
    return pl.pallas_call(
        _mm_kernel, grid=(M // tm, Nn // tn),
        in_specs=[pl.BlockSpec((tm, K), lambda i, j: (i, 0)), pl.BlockSpec((1, K, tn), lambda i, j: (layer, 0, j))],
        out_specs=pl.BlockSpec((tm, tn), lambda i, j: (i, j)),
        out_shape=jax.ShapeDtypeStruct((M, Nn), out_dtype),
        compiler_params=_params(("parallel", "arbitrary")), name="matmul",
    )(a, b)


def _qkv_prep_kernel(n_q_heads, scale, z_ref, cos_ref, sin_ref, qg_ref, kg_ref, q_ref, k_ref, v_ref):
    cosf = cos_ref[...]
    sinf = sin_ref[...]

    def norm_rope(t, gain):
        t = t * lax.rsqrt(jnp.mean(t * t, axis=-1, keepdims=True) + LN_EPS) * gain
        return t * cosf + pltpu.roll(t, HEAD_DIM // 2, 1) * sinf

    for h in range(n_q_heads):
        sl = slice(h * HEAD_DIM, (h + 1) * HEAD_DIM)
        q_ref[:, sl] = (norm_rope(z_ref[:, sl], qg_ref[...]) * scale).astype(q_ref.dtype)
    qw = n_q_heads * HEAD_DIM
    for h in range(KV_HEADS):
        sl = slice(h * HEAD_DIM, (h + 1) * HEAD_DIM)
        zs = slice(qw + h * HEAD_DIM, qw + (h + 1) * HEAD_DIM)
        k_ref[:, sl] = norm_rope(z_ref[:, zs], kg_ref[...]).astype(k_ref.dtype)
    kvw = KV_HEADS * HEAD_DIM
    ones = jnp.ones((z_ref.shape[0], HEAD_DIM), v_ref.dtype)
    for h in range(KV_HEADS):
        zs = slice(qw + kvw + h * HEAD_DIM, qw + kvw + (h + 1) * HEAD_DIM)
        v_ref[:, 2 * h * HEAD_DIM:(2 * h + 1) * HEAD_DIM] = z_ref[:, zs].astype(v_ref.dtype)
        v_ref[:, (2 * h + 1) * HEAD_DIM:(2 * h + 2) * HEAD_DIM] = ones


def qkv_prep(z, cosf, sinf, q_gain, k_gain, attn_width, tm=256):
    N = z.shape[0]
    n_q = attn_width // HEAD_DIM
    kvw = KV_HEADS * HEAD_DIM
    zw = attn_width + 2 * kvw
    return pl.pallas_call(
        functools.partial(_qkv_prep_kernel, n_q, HEAD_DIM ** -0.5 * math.log2(math.e)),
        grid=(N // tm,),
        in_specs=[pl.BlockSpec((tm, zw), lambda i: (i, 0)),
                  pl.BlockSpec((tm, HEAD_DIM), lambda i: (i, 0)), pl.BlockSpec((tm, HEAD_DIM), lambda i: (i, 0)),
                  pl.BlockSpec((1, HEAD_DIM), lambda i: (0, 0)), pl.BlockSpec((1, HEAD_DIM), lambda i: (0, 0))],
        out_specs=[pl.BlockSpec((tm, attn_width), lambda i: (i, 0)),
                   pl.BlockSpec((tm, kvw), lambda i: (i, 0)), pl.BlockSpec((tm, 2 * kvw), lambda i: (i, 0))],
        out_shape=[jax.ShapeDtypeStruct((N, attn_width), BF16),
                   jax.ShapeDtypeStruct((N, kvw), BF16), jax.ShapeDtypeStruct((N, 2 * kvw), BF16)],
        compiler_params=_params(("parallel",)), name="qkv_prep",
    )(z, cosf, sinf, q_gain.reshape(1, HEAD_DIM), k_gain.reshape(1, HEAD_DIM))


def _attn_kernel(n_ctx, n_ctx_tiles, group, n_cast, q_ref, k_ref, v_ref, *rest):
    cast_src, o_ref, cast_dst = rest[:n_cast], rest[n_cast], rest[n_cast + 1:]
    tq = q_ref.shape[0]
    hh = group // 2
    halves = [jnp.concatenate([q_ref[:, h * HEAD_DIM:(h + 1) * HEAD_DIM] for h in range(i * hh, (i + 1) * hh)], axis=0)
              for i in range(2)]

    def attend(k, v):
        s = [_dot_nt(qh, k) for qh in halves]
        p = [jnp.exp2(x - jnp.max(x, axis=-1, keepdims=True)).astype(v.dtype) for x in s]
        acc = [_dot(x, v) for x in p]
        for h in range(group):
            a = acc[h // hh][(h % hh) * tq:(h % hh + 1) * tq]
            o_ref[:, h * HEAD_DIM:(h + 1) * HEAD_DIM] = (a[:, :HEAD_DIM] / a[:, HEAD_DIM:]).astype(o_ref.dtype)

    is_ctx = pl.program_id(1) < n_ctx_tiles

    @pl.when(is_ctx)
    def _():
        attend(k_ref[0:n_ctx, :], v_ref[0:n_ctx, :])

    @pl.when(jnp.logical_not(is_ctx))
    def _():
        attend(k_ref[...], v_ref[...])

    for src, dst in zip(cast_src, cast_dst):
        dst[...] = src[...].astype(dst.dtype)


def attention(q, k, v, n_ctx, cast=(), tq=128):
    N, aw = q.shape
    group = aw // HEAD_DIM // KV_HEADS
    gw = group * HEAD_DIM
    assert N % tq == 0 and n_ctx % tq == 0
    n_q = N // tq
    n_steps = KV_HEADS * n_q
    cast_in, cast_out, cast_shape, cast_args = [], [], [], []
    for a, layer in cast:
        _, rows, cols = a.shape
        rb = -(-(-(-rows // n_steps)) // 16) * 16
        last = -(-rows // rb) - 1
        if rows % rb == 0:
            a2, first = a.reshape(-1, cols), layer * (rows // rb)
        else:
            a2, first = a[layer], 0
        cast_args.append(a2)
        cast_in.append(pl.BlockSpec((rb, cols), lambda g, i, last=last, first=first:
                                    (first + jnp.minimum(g * n_q + i, last), 0)))
        cast_out.append(pl.BlockSpec((rb, cols), lambda g, i, last=last: (jnp.minimum(g * n_q + i, last), 0)))
        cast_shape.append(jax.ShapeDtypeStruct((rows, cols), BF16))
    res = pl.pallas_call(
        functools.partial(_attn_kernel, n_ctx, n_ctx // tq, group, len(cast)),
        grid=(KV_HEADS, n_q),
        in_specs=[pl.BlockSpec((tq, gw), lambda g, i: (i, g)),
                  pl.BlockSpec((N, HEAD_DIM), lambda g, i: (0, g)),
                  pl.BlockSpec((N, 2 * HEAD_DIM), lambda g, i: (0, g))] + cast_in,
        out_specs=[pl.BlockSpec((tq, gw), lambda g, i: (i, g))] + cast_out,
        out_shape=[jax.ShapeDtypeStruct((N, aw), BF16)] + cast_shape,
        compiler_params=_params(("arbitrary", "arbitrary")), name="attention",
    )(q, k, v, *cast_args)
    return res[0], tuple(res[1:])


def rope_tables(n_ctx, n_lat):
    n_rows = n_lat // GRID_W
    row = jnp.repeat(jnp.arange(n_rows), GRID_W).astype(F32)
    col = jnp.tile(jnp.arange(GRID_W), n_rows).astype(F32)
    axis_dim = HEAD_DIM // 2
    inv_freq = ROPE_THETA ** (-jnp.arange(0, axis_dim, 2, dtype=F32) / axis_dim)
    ang = jnp.concatenate([row[:, None] * inv_freq, col[:, None] * inv_freq], -1)
    cos, sin = jnp.cos(ang), jnp.sin(ang)
    cosf = jnp.concatenate([cos, cos], -1)
    sinf = jnp.concatenate([-sin, sin], -1)
    cosf = jnp.concatenate([jnp.ones((n_ctx, HEAD_DIM), F32), cosf], 0)
    sinf = jnp.concatenate([jnp.zeros((n_ctx, HEAD_DIM), F32), sinf], 0)
    return cosf, sinf


def _seg_sum64(x, e128):
    return _dot_exact_rhs(x, e128)


def _rwkv_feat_kernel(n_ctx_tiles, n_tiles, w_cols,
                      z_ref, zp_ref, zn_ref, mu_ref, kk_ref, ka_ref, w0_ref, a0_ref,
                      wup_ref, aup_ref, gup_ref, e_ref,
                      r_ref, v_ref, kap_ref, g_ref, lwf_ref, bf_ref, ktf_ref, lwb_ref, bb_ref, ktb_ref,
                      scr_ref):
    i = pl.program_id(0)
    tm = z_ref.shape[0]
    first = jnp.logical_or(i == 0, i == n_ctx_tiles)
    last = jnp.logical_or(i == n_ctx_tiles - 1, i == n_tiles - 1)
    scr_ref[8:8 + tm, :] = z_ref[...]
    scr_ref[0:8, :] = jnp.where(first, 0.0, zp_ref[...])
    scr_ref[8 + tm:16 + tm, :] = jnp.where(last, 0.0, zn_ref[...])
    z = z_ref[...]
    prev = scr_ref[7:7 + tm, :]
    nxt = scr_ref[9:9 + tm, :]
    zs = z + mu_ref[0:1, :] * (prev - z) + mu_ref[1:2, :] * (nxt - z)

    r = zs[:, 0:w_cols]
    k = zs[:, w_cols:2 * w_cols]
    v = zs[:, 2 * w_cols:3 * w_cols]
    lora = zs[:, 3 * w_cols:]
    wd = jnp.tanh(lora[:, 0:LANES])
    ad = lora[:, LANES:2 * LANES]
    gd = _sigmoid(lora[:, 2 * LANES:3 * LANES])
    r_ref[...] = r.astype(r_ref.dtype)
    v_ref[...] = v.astype(v_ref.dtype)
    g_ref[...] = _dot(gd.astype(BF16), gup_ref[...]).astype(g_ref.dtype)
    kk = k * kk_ref[...]
    e128 = e_ref[...]
    kap = jnp.concatenate(
        [kk[:, s:s + LANES] * lax.rsqrt(jnp.maximum(_seg_sum64(kk[:, s:s + LANES] * kk[:, s:s + LANES], e128), 1e-24))
         for s in range(0, w_cols, LANES)], axis=1)
    kap_ref[...] = kap.astype(kap_ref.dtype)
    outs = ((lwf_ref, bf_ref, ktf_ref), (lwb_ref, bb_ref, ktb_ref))
    for d in range(2):
        lw_ref, b_ref, kt_ref = outs[d]
        lw_ref[...] = -W_DECAY_SCALE * _sigmoid(w0_ref[d:d + 1, :] + _dot(wd.astype(BF16), wup_ref[d]))
        a = _sigmoid(a0_ref[d:d + 1, :] + _dot(ad.astype(BF16), aup_ref[d]))
        kt_ref[...] = (k * (1.0 + (a - 1.0) * ka_ref[...])).astype(kt_ref.dtype)
        b_ref[...] = (a * kap).astype(b_ref.dtype)


def _seg_ones(width=LANES, seg=RWKV_HEAD):
    i = np.arange(width)
    return jnp.asarray((i[:, None] // seg == i[None, :] // seg).astype(np.float32)).astype(BF16)


def rwkv_features(zr, n_ctx, mu, w0, w_up, a0, a_up, g_up, k_k, k_a, tm=256):
    N, zw = zr.shape
    wc = k_k.shape[0]
    assert zw == 3 * wc + 3 * LANES and W_RANK + W_RANK == LANES and A_RANK + A_RANK == LANES and G_RANK == LANES
    zeros = jnp.zeros((W_RANK, wc), F32)
    wup = jnp.stack([jnp.concatenate([w_up[0], zeros], 0), jnp.concatenate([zeros, w_up[1]], 0)], 0)
    aup = jnp.stack([jnp.concatenate([a_up[0], zeros], 0), jnp.concatenate([zeros, a_up[1]], 0)], 0)
    n_tiles = N // tm
    t8 = tm // 8
    row = lambda i: (i, 0)
    full = lambda i: (0, 0)
    sds = lambda dt: jax.ShapeDtypeStruct((N, wc), dt)
    return pl.pallas_call(
        functools.partial(_rwkv_feat_kernel, n_ctx // tm, n_tiles, wc),
        grid=(n_tiles,),
        in_specs=[pl.BlockSpec((tm, zw), row),
                  pl.BlockSpec((8, zw), lambda i: (jnp.maximum(i * t8 - 1, 0), 0)),
                  pl.BlockSpec((8, zw), lambda i: (jnp.minimum((i + 1) * t8, N // 8 - 1), 0)),
                  pl.BlockSpec((2, zw), full), pl.BlockSpec((1, wc), full), pl.BlockSpec((1, wc), full),
                  pl.BlockSpec((2, wc), full), pl.BlockSpec((2, wc), full),
                  pl.BlockSpec((2, LANES, wc), lambda i: (0, 0, 0)), pl.BlockSpec((2, LANES, wc), lambda i: (0, 0, 0)),
                  pl.BlockSpec((LANES, wc), full), pl.BlockSpec((LANES, LANES), full)],
        out_specs=[pl.BlockSpec((tm, wc), row)] * 10,
        out_shape=[sds(BF16)] * 4 + [sds(F32), sds(BF16), sds(BF16)] * 2,
        scratch_shapes=[pltpu.VMEM((tm + 16, zw), F32)],
        compiler_params=_params(("parallel",)), name="rwkv_features",
    )(zr, zr, zr, mu, k_k.reshape(1, wc), k_a.reshape(1, wc), w0, a0,
      wup.astype(BF16), aup.astype(BF16), g_up.astype(BF16), _seg_ones())


def _rwkv_scan_kernel(rev, pairs, n_sub, r_ref, kap_ref, v_ref, lw_ref, b_ref, kt_ref, y_ref, s_ref):
    L = CHUNK
    L2 = 2 * L

    @pl.when(pl.program_id(1) == 0)
    def _():
        s_ref[...] = jnp.zeros(s_ref.shape, F32)

    t_i = lax.broadcasted_iota(jnp.int32, (L, L), 0)
    s_i = lax.broadcasted_iota(jnp.int32, (L, L), 1)
    m_incl64 = ((s_i >= t_i) if rev else (s_i <= t_i)).astype(F32)
    ri = lax.broadcasted_iota(jnp.int32, (L2, L2), 0)
    qi = lax.broadcasted_iota(jnp.int32, (L2, L2), 1)
    rt, qt = ri % L, qi % L
    same = (ri // L) == (qi // L)
    incl = jnp.logical_and(same, (qt >= rt) if rev else (qt <= rt))
    strict = jnp.logical_and(same, (qt > rt) if rev else (qt < rt))
    eye = (ri == qi).astype(F32)
    lane = lax.broadcasted_iota(jnp.int32, (L, LANES), 1)
    hm = [(lane < RWKV_HEAD).astype(F32), (lane >= RWKV_HEAD).astype(F32)]
    bd = ((lax.broadcasted_iota(jnp.int32, (LANES, LANES), 0) // RWKV_HEAD)
          == (lax.broadcasted_iota(jnp.int32, (LANES, LANES), 1) // RWKV_HEAD)).astype(F32)
    lvl_masks = []
    bsz = 1
    while bsz < L:
        grp = (ri // (2 * bsz)) == (qi // (2 * bsz))
        r_odd = (ri // bsz) % 2 == 1
        q_odd = (qi // bsz) % 2 == 1
        off = jnp.logical_and(jnp.logical_not(r_odd), q_odd) if rev else jnp.logical_and(r_odd, jnp.logical_not(q_odd))
        lvl_masks.append(jnp.logical_and(grp, off))
        bsz *= 2

    def stack2(x):
        return jnp.concatenate([x * hm[0], x * hm[1]], axis=0)

    def bmm(a, b):
        return _dot(a.astype(BF16), b.astype(BF16))

    m_incl_bf = m_incl64.astype(BF16)

    units = [(ci, p) for ci in range(n_sub) for p in range(pairs)]

    def blk(ref, unit):
        ci, p = unit
        cc = (n_sub - 1 - ci) if rev else ci
        return ref[cc * L:(cc + 1) * L, p * LANES:(p + 1) * LANES]

    def each(fn, *dicts):
        return {un: fn(*(d[un] for d in dicts)) for un in units}

    def split3(x):
        hi = x.astype(BF16)
        r1 = x - hi.astype(F32)
        mid = r1.astype(BF16)
        lo = (r1 - mid.astype(F32)).astype(BF16)
        return jnp.concatenate([hi, mid, lo], axis=1)

    lw = {un: blk(lw_ref, un) for un in units}
    c3 = each(lambda x: _dot(m_incl_bf, split3(x)), lw)
    cum = each(lambda c: c[:, 0:LANES] + c[:, LANES:2 * LANES] + c[:, 2 * LANES:], c3)
    e_incl = each(jnp.exp, cum)
    e_inv = each(lambda c: jnp.exp(-c), cum)
    p_tot = each(lambda e: e[0:1, :] if rev else e[L - 1:L, :], e_incl)
    xk = {un: stack2(blk(kap_ref, un) * jnp.exp(cum[un] - lw[un])).astype(BF16) for un in units}
    xr = {un: stack2(blk(r_ref, un) * e_incl[un]).astype(BF16) for un in units}
    yb = {un: stack2(blk(b_ref, un) * e_inv[un]).astype(BF16) for un in units}
    yk = {un: stack2(blk(kt_ref, un) * e_inv[un]).astype(BF16) for un in units}
    vs = {un: stack2(blk(v_ref, un)).astype(BF16) for un in units}
    amat = each(lambda a, b, c, d: _dot_nt(jnp.concatenate([a, b], axis=0), jnp.concatenate([c, d], axis=0)),
                xk, xr, yb, yk)
    a_ub = each(lambda a: jnp.where(strict, a[0:L2, 0:L2], 0.0), amat)
    a_uk = each(lambda a: jnp.where(strict, a[0:L2, L2:], 0.0).astype(BF16), amat)
    a_rb = each(lambda a: jnp.where(incl, a[L2:, 0:L2], 0.0).astype(BF16), amat)
    a_rk = each(lambda a: jnp.where(incl, a[L2:, L2:], 0.0).astype(BF16), amat)
    tinv = each(lambda a: eye - jnp.where(lvl_masks[0], a, 0.0), a_ub)
    for lm in lvl_masks[1:]:
        tb = each(lambda t: t.astype(BF16), tinv)
        x1 = each(lambda a, t: _dot(jnp.where(lm, a, 0.0).astype(BF16), t).astype(BF16), a_ub, tb)
        tinv = each(lambda t, tbf, x: t - _dot(tbf, x), tinv, tb, x1)
    tb = each(lambda t: t.astype(BF16), tinv)
    w1 = each(lambda t, x: _dot(t, x).astype(BF16), tb, xk)
    avs = each(lambda a, v: _dot(a, v).astype(BF16), a_uk, vs)
    y_ind = each(_dot, a_rk, vs)
    k2 = each(_dot_tn, vs, yk)
    w2 = each(_dot, tb, avs)

    state = [s_ref[p] for p in range(pairs)]
    for ci in range(n_sub):
        wx = [_dot_nt(jnp.concatenate([w1[(ci, p)], xr[(ci, p)]], axis=0), state[p].astype(BF16))
              for p in range(pairs)]
        ub = [(-(wx[p][0:L2] + w2[(ci, p)])).astype(BF16) for p in range(pairs)]
        upd = [_dot_tn(ub[p], yb[(ci, p)]) + k2[(ci, p)] for p in range(pairs)]
        state = [(state[p] + upd[p] * bd) * p_tot[(ci, p)] for p in range(pairs)]
        cc = (n_sub - 1 - ci) if rev else ci
        for p in range(pairs):
            ystk = wx[p][L2:] + _dot(a_rb[(ci, p)], ub[p]) + y_ind[(ci, p)]
            y_ref[cc * L:(cc + 1) * L, p * LANES:(p + 1) * LANES] = ystk[0:L] + ystk[L:]
    for p in range(pairs):
        s_ref[p] = state[p]


def rwkv_scan(rev, n_ctx, r, kap, v, lw, b, kt, pairs=4, n_sub=4):
    N, wc = r.shape
    rb = n_sub * CHUNK
    bw = pairs * LANES
    assert N % rb == 0 and n_ctx % rb == 0 and wc % bw == 0
    nb, nbc = N // rb, n_ctx // rb
    if rev:
        rmap = lambda h, i: (jnp.where(i < nbc, nbc - 1 - i, nb - 1 - (i - nbc)), h)
    else:
        rmap = lambda h, i: (i, h)
    spec = pl.BlockSpec((rb, bw), rmap)
    return pl.pallas_call(
        functools.partial(_rwkv_scan_kernel, rev, pairs, n_sub),
        grid=(wc // bw, nb),
        in_specs=[spec] * 6, out_specs=spec,
        out_shape=jax.ShapeDtypeStruct((N, wc), F32),
        scratch_shapes=[pltpu.VMEM((pairs, LANES, LANES), F32)],
        compiler_params=_params(("parallel", "arbitrary")), name="rwkv_scan_rev" if rev else "rwkv_scan_fwd",
    )(r, kap, v, lw, b, kt)


def _rwkv_out_kernel(yf_ref, yb_ref, r_ref, v_ref, ktf_ref, ktb_ref, g_ref, rk_ref, gn_ref, e_ref, o_ref):
    e128 = e_ref[...]
    inv = 1.0 / RWKV_HEAD
    for s in range(0, o_ref.shape[1], LANES):
        sl = slice(s, s + LANES)
        y = yf_ref[:, sl] + yb_ref[:, sl]
        mu = _seg_sum64(y, e128) * inv
        yc = y - mu
        var = _seg_sum64(yc * yc, e128) * inv
        yn = yc * lax.rsqrt(var + GN_EPS) * gn_ref[0:1, sl] + gn_ref[1:2, sl]
        rk = r_ref[:, sl] * rk_ref[0:1, sl]
        bonus = _seg_sum64(rk * (ktf_ref[:, sl].astype(F32) + ktb_ref[:, sl]), e128) * v_ref[:, sl]
        o_ref[:, sl] = ((yn + bonus) * g_ref[:, sl]).astype(o_ref.dtype)


def rwkv_output(y_f, y_b, r, v, kt_f, kt_b, g, r_k, lnx_gain, lnx_bias, tm=256):
    N, wc = r.shape
    row = lambda i: (i, 0)
    full = lambda i: (0, 0)
    gn = jnp.stack([lnx_gain, lnx_bias], 0)
    return pl.pallas_call(
        _rwkv_out_kernel, grid=(N // tm,),
        in_specs=[pl.BlockSpec((tm, wc), row)] * 7 + [pl.BlockSpec((1, wc), full), pl.BlockSpec((2, wc), full),
                                                      pl.BlockSpec((LANES, LANES), full)],
        out_specs=pl.BlockSpec((tm, wc), row),
        out_shape=jax.ShapeDtypeStruct((N, wc), BF16),
        compiler_params=_params(("parallel",)), name="rwkv_output",
    )(y_f, y_b, r, v, kt_f, kt_b, g, r_k.reshape(1, wc), gn, _seg_ones())


def _angles(i, j, period):
    return (2.0 * math.pi / period) * ((i * j) % period).astype(np.float64)


def _chan_dft(gdim, scale):
    i = np.arange(gdim)
    ang = _angles(i[:, None], i[None, :], gdim)
    return jnp.asarray(np.concatenate([np.cos(ang), -np.sin(ang)], 1) * scale, F32)


def _fnet1_kernel(groups, x_ref, cs_ref, m_ref, y_ref):
    r1 = x_ref.shape[0]
    gdim = x_ref.shape[1] // groups
    for g in range(groups):
        sl = slice(g * gdim, (g + 1) * gdim)
        ab = _dot3(x_ref[:, sl], cs_ref[...])
        st = jnp.concatenate([ab[:, :gdim], ab[:, gdim:]], axis=0)
        y = _dot3(m_ref[0], st)
        y_ref[0, 0, :, sl] = y[:r1]
        y_ref[1, 0, :, sl] = y[r1:]


def _fnet3_kernel(cs_ref, y_ref, o_ref):
    o_ref[...] = _dot3(cs_ref[...], y_ref[...])


def fnet_latent(zf, groups=FOURIER_GROUPS):
    T, fw = zf.shape
    gdim = fw // groups
    r2 = GRID_W
    r1 = T // r2
    cs = _chan_dft(gdim, 1.0 / math.sqrt(T * gdim))
    t1p = np.arange(r1)[None, :, None]
    t1 = np.arange(r1)[None, None, :]
    t2 = np.arange(r2)[:, None, None]
    th = _angles(t1p, r2 * t1 + t2, T)
    gr, gi = np.cos(th), -np.sin(th)
    mt = jnp.asarray(np.concatenate([np.concatenate([gr, -gi], 2), np.concatenate([gi, gr], 2)], 1), F32)
    y = pl.pallas_call(
        functools.partial(_fnet1_kernel, groups),
        grid=(r2,),
        in_specs=[pl.BlockSpec((r1, fw), lambda j: (0, j)),
                  pl.BlockSpec((gdim, 2 * gdim), lambda j: (0, 0)),
                  pl.BlockSpec((1, 2 * r1, 2 * r1), lambda j: (j, 0, 0))],
        out_specs=pl.BlockSpec((2, 1, r1, fw), lambda j: (0, j, 0, 0)),
        out_shape=jax.ShapeDtypeStruct((2, r2, r1, fw), F32),
        compiler_params=_params(("parallel",)), name="fnet_rows",
    )(zf.reshape(r1, r2 * fw), cs, mt)
    i2 = np.arange(r2)
    ph = _angles(i2[:, None], i2[None, :], r2)
    cs2 = jnp.asarray(np.concatenate([np.cos(ph), np.sin(ph)], 1), F32)
    tc = 4 * fw if r1 % 4 == 0 else fw
    o = pl.pallas_call(
        _fnet3_kernel, grid=(r1 * fw // tc,),
        in_specs=[pl.BlockSpec((r2, 2 * r2), lambda j: (0, 0)), pl.BlockSpec((2 * r2, tc), lambda j: (0, j))],
        out_specs=pl.BlockSpec((r2, tc), lambda j: (0, j)),
        out_shape=jax.ShapeDtypeStruct((r2, r1 * fw), F32),
        compiler_params=_params(("parallel",)), name="fnet_cols",
    )(cs2, y.reshape(2 * r2, r1 * fw))
    return o.reshape(T, fw)


def _fnet_direct_kernel(x_ref, cs_ref, ct_ref, o_ref):
    gdim = x_ref.shape[1]
    ab = _dot3(x_ref[...], cs_ref[...])
    st = jnp.concatenate([ab[:, :gdim], ab[:, gdim:]], axis=0)
    o_ref[...] = _dot3(ct_ref[...], st)


def fnet_direct(zf, groups=FOURIER_GROUPS):
    C, fw = zf.shape
    gdim = fw // groups
    cs = _chan_dft(gdim, 1.0 / math.sqrt(C * gdim))
    i = np.arange(C)
    ang = _angles(i[:, None], i[None, :], C)
    ct = jnp.asarray(np.concatenate([np.cos(ang), np.sin(ang)], 1), F32)
    return pl.pallas_call(
        _fnet_direct_kernel, grid=(groups,),
        in_specs=[pl.BlockSpec((C, gdim), lambda g: (0, g)), pl.BlockSpec((gdim, 2 * gdim), lambda g: (0, 0)),
                  pl.BlockSpec((C, 2 * C), lambda g: (0, 0))],
        out_specs=pl.BlockSpec((C, gdim), lambda g: (0, g)),
        out_shape=jax.ShapeDtypeStruct((C, fw), F32),
        compiler_params=_params(("parallel",)), name="fnet_direct",
    )(zf, cs, ct)


GATE_LANE0 = N_GROUPS


def _gates_kernel(lg_ref, id_ref, gv_ref):
    lg = lg_ref[...]
    lane = lax.broadcasted_iota(jnp.int32, lg.shape, 1)
    big = jnp.int32(LANES)
    neg = -jnp.inf
    is_g = lane < N_GROUPS
    gl = jnp.where(is_g, lg, neg)
    mg = jnp.max(gl, axis=-1, keepdims=True)
    p_group = 1.0 / jnp.sum(jnp.exp(gl - mg), axis=-1, keepdims=True)
    gsel = jnp.min(jnp.where(gl == mg, lane, big), axis=-1, keepdims=True)
    e_idx = lane - GATE_LANE0
    in_grp = jnp.logical_and(jnp.logical_and(e_idx >= 0, e_idx < N_EXPERTS), e_idx // EXPERTS_PER_GROUP == gsel)
    le = jnp.where(in_grp, lg, neg)
    m1 = jnp.max(le, axis=-1, keepdims=True)
    l1 = jnp.min(jnp.where(le == m1, lane, big), axis=-1, keepdims=True)
    le2 = jnp.where(lane == l1, neg, le)
    m2 = jnp.max(le2, axis=-1, keepdims=True)
    l2 = jnp.min(jnp.where(le2 == m2, lane, big), axis=-1, keepdims=True)
    e2 = jnp.exp(m2 - m1)
    inv = p_group / (1.0 + e2)
    id_ref[...] = jnp.where(lane == 0, l1 - GATE_LANE0, jnp.where(lane == 1, l2 - GATE_LANE0, 0))
    gv_ref[...] = jnp.where(lane == 0, inv, jnp.where(lane == 1, e2 * inv, 0.0))


def moe_gates(logits, tm=256):
    N = logits.shape[0]
    spec = pl.BlockSpec((tm, LANES), lambda i: (i, 0))
    return pl.pallas_call(
        _gates_kernel, grid=(N // tm,),
        in_specs=[spec], out_specs=[spec, spec],
        out_shape=[jax.ShapeDtypeStruct((N, LANES), jnp.int32), jax.ShapeDtypeStruct((N, LANES), F32)],
        compiler_params=_params(("parallel",)), name="moe_gates",
    )(logits)


def moe_dispatch(expert_ids, ts):
    N = expert_ids.shape[0]
    flat = expert_ids.reshape(-1)
    onehot = (flat[:, None] == jnp.arange(N_EXPERTS, dtype=jnp.int32)[None, :]).astype(jnp.int32)
    csum = jnp.cumsum(onehot, axis=0)
    rank = jnp.take_along_axis(csum, flat[:, None], axis=1)[:, 0] - 1
    counts = csum[-1]
    tiles_per = (counts + ts - 1) // ts
    tiles_end = jnp.cumsum(tiles_per)
    pad_start = (tiles_end - tiles_per) * ts
    slot = pad_start[flat] + rank
    n_tiles = -(-2 * N // ts) + N_EXPERTS
    slot_token = jnp.zeros((n_tiles * ts,), jnp.int32).at[slot].set(jnp.arange(2 * N, dtype=jnp.int32) // 2)
    n_used = tiles_end[-1]
    tile_idx = jnp.minimum(jnp.arange(n_tiles, dtype=jnp.int32), n_used - 1)
    tile_expert = jnp.sum((tile_idx[:, None] >= tiles_end[None, :]).astype(jnp.int32), axis=1)
    return slot_token, slot.reshape(N, 2), tile_expert.astype(jnp.int32), n_used.reshape(1).astype(jnp.int32)


def _moe_group_kernel(ts, n_chunks, te_ref, nu_ref, st_ref, u_hbm, w1_ref, w3_ref, w2_ref, o_ref, ubuf, ub16, sem):
    i = pl.program_id(0)
    j = pl.program_id(1)
    used = nu_ref[0]
    share = ts // n_chunks

    def row_copy(slot, b, r):
        return pltpu.make_async_copy(u_hbm.at[pl.ds(st_ref[slot], 1), :], ubuf.at[b, pl.ds(r, 1), :], sem.at[b])

    @pl.when(jnp.logical_and(j == 0, i == 0))
    def _():
        def row(r, carry):
            row_copy(r, 0, r).start()
            return carry
        lax.fori_loop(0, ts, row, 0)

    @pl.when(jnp.logical_and(j == 0, i <= used))
    def _():
        pltpu.make_async_copy(u_hbm.at[pl.ds(0, ts), :], ubuf.at[i % 2], sem.at[i % 2]).wait()

    @pl.when(jnp.logical_and(j == 0, i < used))
    def _():
        ub16[...] = ubuf[i % 2].astype(BF16)

    @pl.when(j == 0)
    def _():
        o_ref[...] = jnp.zeros(o_ref.shape, o_ref.dtype)

    @pl.when(i < used)
    def _():
        nb = (i + 1) % 2
        r0 = j * share
        for r in range(share):
            row_copy((i + 1) * ts + r0 + r, nb, r0 + r).start()
        u = ub16[...]
        a = _dot(u, w1_ref[0, 0])
        g = _dot(u, w3_ref[0, 0])
        h = ((a * _sigmoid(a)) * g).astype(BF16)
        nw = o_ref.shape[1] // 4
        for n in range(0, o_ref.shape[1], nw):
            o_ref[:, n:n + nw] += _dot(h, w2_ref[0, 0, :, n:n + nw])


def moe_grouped(u, slot_token, tile_expert, n_used, w1, w3, w2, layer, ts, tc=256):
    N, D = u.shape
    _, E, _, De = w1.shape
    n_tiles = tile_expert.shape[0]
    grid_spec = pltpu.PrefetchScalarGridSpec(
        num_scalar_prefetch=3,
        grid=(n_tiles, De // tc),
        in_specs=[pl.BlockSpec(memory_space=pl.ANY),
                  pl.BlockSpec((1, 1, D, tc), lambda i, j, te, nu, st: (layer, te[i], 0, j)),
                  pl.BlockSpec((1, 1, D, tc), lambda i, j, te, nu, st: (layer, te[i], 0, j)),
                  pl.BlockSpec((1, 1, tc, D), lambda i, j, te, nu, st: (layer, te[i], j, 0))],
        out_specs=pl.BlockSpec((ts, D), lambda i, j, te, nu, st: (i, 0)),
        scratch_shapes=[pltpu.VMEM((2, ts, D), F32), pltpu.VMEM((ts, D), BF16), pltpu.SemaphoreType.DMA((2,))],
    )
    return pl.pallas_call(
        functools.partial(_moe_group_kernel, ts, De // tc), grid_spec=grid_spec,
        out_shape=jax.ShapeDtypeStruct((n_tiles * ts, D), F32),
        compiler_params=_params(("arbitrary", "arbitrary")), name="moe_grouped",
    )(tile_expert, n_used, slot_token, u, w1, w3, w2)


def _moe_combine_kernel(alpha, n_ctx_tiles, sa_ref, sb_ref, o_hbm, x_ref, gv_ref, g_ref, gb_ref, out_ref,
                        abuf, bbuf, sem):
    i = pl.program_id(0)
    tm = x_ref.shape[0]

    def gather(tile, b):
        def row(r, carry):
            t = tile * tm + r
            pltpu.make_async_copy(o_hbm.at[pl.ds(sa_ref[t], 1), :], abuf.at[b, pl.ds(r, 1), :], sem.at[0, b]).start()
            pltpu.make_async_copy(o_hbm.at[pl.ds(sb_ref[t], 1), :], bbuf.at[b, pl.ds(r, 1), :], sem.at[1, b]).start()
            return carry
        lax.fori_loop(0, tm, row, 0)

    @pl.when(i == 0)
    def _():
        gather(0, 0)

    @pl.when(i + 1 < pl.num_programs(0))
    def _():
        gather(i + 1, (i + 1) % 2)

    b = i % 2
    pltpu.make_async_copy(o_hbm.at[pl.ds(0, tm), :], abuf.at[b], sem.at[0, b]).wait()
    pltpu.make_async_copy(o_hbm.at[pl.ds(0, tm), :], bbuf.at[b], sem.at[1, b]).wait()
    gv = gv_ref[...]
    f = gv[:, 0:1] * abuf[b] + gv[:, 1:2] * bbuf[b]
    g = jnp.where(i < n_ctx_tiles, g_ref[1:2, :], g_ref[0:1, :])
    y = alpha * x_ref[...] + g * f
    out_ref[...] = _ln_rows(y, LN_EPS) * gb_ref[0:1, :] + gb_ref[1:2, :]


def moe_combine_ln(xx, o, slots, gate_vals, gates2, gain_bias, n_ctx, alpha, tm=256):
    N, D = xx.shape
    grid_spec = pltpu.PrefetchScalarGridSpec(
        num_scalar_prefetch=2,
        grid=(N // tm,),
        in_specs=[pl.BlockSpec(memory_space=pl.ANY),
                  pl.BlockSpec((tm, D), lambda i, sa, sb: (i, 0)),
                  pl.BlockSpec((tm, LANES), lambda i, sa, sb: (i, 0)),
                  pl.BlockSpec((2, D), lambda i, sa, sb: (0, 0)), pl.BlockSpec((2, D), lambda i, sa, sb: (0, 0))],
        out_specs=pl.BlockSpec((tm, D), lambda i, sa, sb: (i, 0)),
        scratch_shapes=[pltpu.VMEM((2, tm, D), F32), pltpu.VMEM((2, tm, D), F32), pltpu.SemaphoreType.DMA((2, 2))],
    )
    return pl.pallas_call(
        functools.partial(_moe_combine_kernel, alpha, n_ctx // tm), grid_spec=grid_spec,
        out_shape=jax.ShapeDtypeStruct((N, D), F32),
        compiler_params=_params(("arbitrary",)), name="moe_combine_ln",
    )(slots[:, 0], slots[:, 1], o, xx, gate_vals, gates2, gain_bias)


MOE_SLOT_TILE = 512

def _pick_tile(n, prefer):
    for t in prefer:
        if n % t == 0:
            return t
    raise ValueError(f"no tile for {n}")


def kernel(x, c, ctx, c_ctx, w_mod, b_mod, w_in, q_gain, k_gain, rwkv_mu, w0, w_up, a0, a_up, g_up, k_k, k_a, r_k, lnx_gain, lnx_bias, w_out, ln1_gain, ln1_bias, ln2_gain, ln2_bias, router_group_w, router_group_b, router_expert_w, router_expert_b, w1, w3, w2):
    B, T, D = x.shape
    C = ctx.shape[1]
    assert B == 1
    depth = w_mod.shape[0]
    alpha = (2 * depth) ** 0.25
    N = C + T
    aw = D // 2
    kvw = KV_HEADS * HEAD_DIM
    rw = D // 4
    fw = D // 4
    rcols = 3 * rw + 2 * W_RANK + 2 * A_RANK + G_RANK
    o_zr = aw + 2 * kvw
    o_zf = o_zr + rcols
    tm_big = _pick_tile(N, (768, 384, 256))

    xx = jnp.concatenate([ctx[0], x[0]], 0)
    mods = mod_vectors(c, c_ctx, w_mod, b_mod).reshape(depth, 8, 6, D)
    cosf, sinf = rope_tables(C, T)
    w_qkv = w_in[:, :, :o_zr].astype(BF16)
    w_rwkv = w_in[:, :, o_zr:o_zf].astype(BF16)
    w_four = w_in[:, :, o_zf:].astype(BF16)
    w_out_b = w_out.astype(BF16)
    n_exp, _, d_exp = w1.shape[1:]
    ew_flat = (w1.reshape(depth, n_exp * D, d_exp), w3.reshape(depth, n_exp * D, d_exp),
               w2.reshape(depth, n_exp * d_exp, D))

    for l in range(depth):
        last = l == depth - 1
        mv = mods[l]
        vec = lambda i: jnp.stack([mv[0, i], mv[1, i]], 0)
        ss1 = jnp.stack([mv[0, 0], mv[0, 1], mv[1, 0], mv[1, 1]], 0)
        ss2 = jnp.stack([mv[0, 3], mv[0, 4], mv[1, 3], mv[1, 4]], 0)

        u = ln_modulate(xx, ss1, C)
        zq = matmul(u, w_qkv, l, F32, tm_big, _pick_tile(o_zr, (768, 512, 256, 128)))
        zr = matmul(u, w_rwkv, l, F32, tm_big, _pick_tile(rcols, (1152, 384, 128)))
        zf = matmul(u, w_four, l, F32, tm_big, _pick_tile(fw, (512, 256, 128)))
        qh, kh, vh = qkv_prep(zq, cosf, sinf, q_gain[l], k_gain[l], aw)
        attn, (w1_b, w3_b, w2_b) = attention(qh, kh, vh, C, cast=[(a, l) for a in ew_flat])
        w1_b = w1_b.reshape(1, n_exp, D, d_exp)
        w3_b = w3_b.reshape(1, n_exp, D, d_exp)
        w2_b = w2_b.reshape(1, n_exp, d_exp, D)
        r, v, kap, g, lwf, bf, ktf, lwb, bb, ktb = rwkv_features(
            zr, C, rwkv_mu[l], w0[l], w_up[l], a0[l], a_up[l], g_up[l], k_k[l], k_a[l])
        y_f = rwkv_scan(False, C, r, kap, v, lwf, bf, ktf)
        y_b = rwkv_scan(True, C, r, kap, v, lwb, bb, ktb)
        rwkv = rwkv_output(y_f, y_b, r, v, ktf, ktb, g, r_k[l].reshape(-1), lnx_gain[l], lnx_bias[l])
        fl = fnet_latent(zf[C:])
        fc = fnet_direct(zf[:C]) if not last else jnp.zeros((C, fw), F32)
        fn = jnp.concatenate([fc, fl], 0).astype(BF16)
        cat = jnp.concatenate([attn, rwkv, fn], 1)
        m = matmul(cat, w_out_b, l, F32, tm_big, _pick_tile(D, (512, 256, 128)))
        xx = resid_ln(xx, m, vec(2), jnp.stack([ln1_gain[l], ln1_bias[l]], 0), C, alpha)

        wr = jnp.concatenate([router_group_w[l], router_expert_w[l],
                              jnp.zeros((D, LANES - N_GROUPS - N_EXPERTS), F32)], 1)
        br = jnp.concatenate([router_group_b[l], router_expert_b[l],
                              jnp.zeros((LANES - N_GROUPS - N_EXPERTS,), F32)], 0).reshape(1, LANES)
        u2, logits = ln_modulate(xx, ss2, C, router=(wr, br), out_dtype=F32)
        expert_ids, gate_vals = moe_gates(logits)
        slot_token, slots, tile_expert, n_used = moe_dispatch(expert_ids[:, :2], MOE_SLOT_TILE)
        o = moe_grouped(u2, slot_token, tile_expert, n_used, w1_b, w3_b, w2_b, 0, MOE_SLOT_TILE)
        xx = moe_combine_ln(xx, o, slots, gate_vals, vec(5), jnp.stack([ln2_gain[l], ln2_bias[l]], 0), C, alpha)
    return xx[C:][None]
```

```python
import functools
import math

import numpy as np
import jax
import jax.numpy as jnp
from jax import lax
from jax.experimental import pallas as pl
from jax.experimental.pallas import tpu as pltpu

F32 = jnp.float32
BF16 = jnp.bfloat16

GRID_W = 64
HEAD_DIM = 128
KV_HEADS = 4
RWKV_HEAD = 64
W_RANK = 64
A_RANK = 64
G_RANK = 128
FOURIER_GROUPS = 4
N_GROUPS = 4
EXPERTS_PER_GROUP = 4
N_EXPERTS = N_GROUPS * EXPERTS_PER_GROUP
ROPE_THETA = 10000.0
W_DECAY_SCALE = math.exp(-0.5)
GN_EPS = 64e-5
LN_EPS = 1e-6
LANES = 128
CHUNK = 64

V7X_VMEM_LIMIT_MB = 56


def _params(sem, vmem_mb=V7X_VMEM_LIMIT_MB):
    return pltpu.CompilerParams(dimension_semantics=sem, vmem_limit_bytes=vmem_mb * 1024 * 1024)


def _dot(a, b, prec=None):
    return jnp.dot(a, b, preferred_element_type=F32, precision=prec)


def _dot_nt(a, b, prec=None):
    return lax.dot_general(a, b, (((1,), (1,)), ((), ())), preferred_element_type=F32, precision=prec)


def _dot_tn(a, b, prec=None):
    return lax.dot_general(a, b, (((0,), (0,)), ((), ())), preferred_element_type=F32, precision=prec)


def _sigmoid(x):
    return 1.0 / (1.0 + jnp.exp(-x))


def _split2(x):
    hi = x.astype(BF16)
    return hi, (x - hi.astype(F32)).astype(BF16)


def _dot3(a, b):
    a_hi, a_lo = _split2(a)
    b_hi, b_lo = _split2(b)
    return _dot(a_hi, b_hi) + (_dot(a_hi, b_lo) + _dot(a_lo, b_hi))


def _dot_exact_rhs(x, e):
    hi = x.astype(BF16)
    r1 = x - hi.astype(F32)
    mid = r1.astype(BF16)
    lo = (r1 - mid.astype(F32)).astype(BF16)
    return _dot(hi, e) + (_dot(mid, e) + _dot(lo, e))


def _mod_kernel(a_ref, w_ref, b_ref, o_ref):
    tn = o_ref.shape[2]
    rows = []
    for r in range(2):
        cols = []
        for j in range(tn // LANES):
            prod = w_ref[0, :, j * LANES:(j + 1) * LANES] * a_ref[r]
            cols.append(jnp.sum(prod, axis=0, keepdims=True))
        rows.append(jnp.concatenate(cols, axis=1) + b_ref[0])
    rows.append(jnp.zeros((6, tn), F32))
    o_ref[0] = jnp.concatenate(rows, axis=0)


def mod_vectors(c, c_ctx, w_mod, b_mod):
    L, D, D6 = w_mod.shape
    acts = jnp.stack([jax.nn.silu(c[0]), jax.nn.silu(c_ctx)], 0)
    a_b = jnp.broadcast_to(acts[:, :, None], (2, D, LANES))
    tn = 512
    return pl.pallas_call(
        _mod_kernel,
        grid=(L, D6 // tn),
        in_specs=[pl.BlockSpec((2, D, LANES), lambda l, j: (0, 0, 0)),
                  pl.BlockSpec((1, D, tn), lambda l, j: (l, 0, j)),
                  pl.BlockSpec((1, 1, tn), lambda l, j: (l, 0, j))],
        out_specs=pl.BlockSpec((1, 8, tn), lambda l, j: (l, 0, j)),
        out_shape=jax.ShapeDtypeStruct((L, 8, D6), F32),
        compiler_params=_params(("parallel", "parallel")),
        name="mod_vectors",
    )(a_b, w_mod, b_mod.reshape(L, 1, D6))


def _ln_rows(x, eps):
    mu = jnp.mean(x, axis=-1, keepdims=True)
    xc = x - mu
    var = jnp.mean(xc * xc, axis=-1, keepdims=True)
    return xc * lax.rsqrt(var + eps)


def _ln_mod_kernel(n_ctx_tiles, with_router, x_ref, ss_ref, *rest):
    if with_router:
        wr_ref, br_ref, u_ref, lg_ref = rest
    else:
        (u_ref,) = rest
    is_ctx = pl.program_id(0) < n_ctx_tiles
    sh = jnp.where(is_ctx, ss_ref[2:3, :], ss_ref[0:1, :])
    sc = jnp.where(is_ctx, ss_ref[3:4, :], ss_ref[1:2, :])
    u = _ln_rows(x_ref[...], LN_EPS) * (1.0 + sc) + sh
    u_ref[...] = u.astype(u_ref.dtype)
    if with_router:
        lg_ref[...] = _dot3(u, wr_ref[...]) + br_ref[...]


def ln_modulate(xx, ss, n_ctx, router=None, out_dtype=BF16, tm=256):
    N, D = xx.shape
    assert N % tm == 0 and n_ctx % tm == 0
    in_specs = [pl.BlockSpec((tm, D), lambda i: (i, 0)), pl.BlockSpec((4, D), lambda i: (0, 0))]
    out_specs = [pl.BlockSpec((tm, D), lambda i: (i, 0))]
    out_shape = [jax.ShapeDtypeStruct((N, D), out_dtype)]
    args = [xx, ss]
    if router is not None:
        in_specs += [pl.BlockSpec((D, LANES), lambda i: (0, 0)), pl.BlockSpec((1, LANES), lambda i: (0, 0))]
        out_specs.append(pl.BlockSpec((tm, LANES), lambda i: (i, 0)))
        out_shape.append(jax.ShapeDtypeStruct((N, LANES), F32))
        args += list(router)
    res = pl.pallas_call(
        functools.partial(_ln_mod_kernel, n_ctx // tm, router is not None),
        grid=(N // tm,), in_specs=in_specs, out_specs=out_specs, out_shape=out_shape,
        compiler_params=_params(("parallel",)), name="ln_modulate",
    )(*args)
    return res if router is not None else res[0]


def _resid_ln_kernel(alpha, n_ctx_tiles, x_ref, m_ref, g_ref, gb_ref, o_ref):
    is_ctx = pl.program_id(0) < n_ctx_tiles
    g = jnp.where(is_ctx, g_ref[1:2, :], g_ref[0:1, :])
    y = alpha * x_ref[...] + g * m_ref[...].astype(F32)
    o_ref[...] = _ln_rows(y, LN_EPS) * gb_ref[0:1, :] + gb_ref[1:2, :]


def resid_ln(xx, m, gates2, gain_bias, n_ctx, alpha, tm=256):
    N, D = xx.shape
    return pl.pallas_call(
        functools.partial(_resid_ln_kernel, alpha, n_ctx // tm),
        grid=(N // tm,),
        in_specs=[pl.BlockSpec((tm, D), lambda i: (i, 0)), pl.BlockSpec((tm, D), lambda i: (i, 0)),
                  pl.BlockSpec((2, D), lambda i: (0, 0)), pl.BlockSpec((2, D), lambda i: (0, 0))],
        out_specs=pl.BlockSpec((tm, D), lambda i: (i, 0)),
        out_shape=jax.ShapeDtypeStruct((N, D), F32),
        compiler_params=_params(("parallel",)), name="resid_ln",
    )(xx, m, gates2, gain_bias)


def _mm_kernel(a_ref, b_ref, o_ref):
    o_ref[...] = _dot(a_ref[...], b_ref[0]).astype(o_ref.dtype)


def matmul(a, b, layer, out_dtype, tm, tn):
    M, K = a.shape
    _, _, Nn = b.shape
    assert M % tm == 0 and Nn % tn == 0
    return pl.pallas_call(
        _mm_kernel, grid=(M // tm, Nn // tn),
        in_specs=[pl.BlockSpec((tm, K), lambda i, j: (i, 0)), pl.BlockSpec((1, K, tn), lambda i, j: (layer, 0, j))],
        out_specs=pl.BlockSpec((tm, tn), lambda i, j: (i, j)),
        out_shape=jax.ShapeDtypeStruct((M, Nn), out_dtype),
        compiler_params=_params(("parallel", "arbitrary")), name="matmul",
    )(a, b)


def _mm_parts_kernel(n_parts, *refs):
    a_refs, b_refs, o_ref = refs[:n_parts], refs[n_parts:2 * n_parts], refs[2 * n_parts]
    acc = _dot(a_refs[0][...], b_refs[0][0])
    for a_ref, b_ref in zip(a_refs[1:], b_refs[1:]):
        acc = acc + _dot(a_ref[...], b_ref[0])
    o_ref[...] = acc.astype(o_ref.dtype)


def matmul_parts(parts, b, layer, out_dtype, tm, tn):
    M = parts[0].shape[0]
    Nn = b.shape[2]
    a_specs, b_specs, off = [], [], 0
    for p in parts:
        kp = p.shape[1]
        assert off % kp == 0
        a_specs.append(pl.BlockSpec((tm, kp), lambda i, j: (i, 0)))
        b_specs.append(pl.BlockSpec((1, kp, tn), lambda i, j, r=off // kp: (layer, r, j)))
        off += kp
    assert off == b.shape[1] and M % tm == 0 and Nn % tn == 0
    return pl.pallas_call(
        functools.partial(_mm_parts_kernel, len(parts)), grid=(M // tm, Nn // tn),
        in_specs=a_specs + b_specs,
        out_specs=pl.BlockSpec((tm, tn), lambda i, j: (i, j)),
        out_shape=jax.ShapeDtypeStruct((M, Nn), out_dtype),
        compiler_params=_params(("parallel", "arbitrary")), name="matmul_parts",
    )(*parts, *([b] * len(parts)))


def _qkv_prep_kernel(n_q_heads, scale, z_ref, cos_ref, sin_ref, qg_ref, kg_ref, q_ref, k_ref, v_ref):
    cosf = cos_ref[...]
    sinf = sin_ref[...]

    def norm_rope(t, gain):
        t = t * lax.rsqrt(jnp.mean(t * t, axis=-1, keepdims=True) + LN_EPS) * gain
        return t * cosf + pltpu.roll(t, HEAD_DIM // 2, 1) * sinf

    for h in range(n_q_heads):
        sl = slice(h * HEAD_DIM, (h + 1) * HEAD_DIM)
        q_ref[:, sl] = (norm_rope(z_ref[:, sl], qg_ref[...]) * scale).astype(q_ref.dtype)
    qw = n_q_heads * HEAD_DIM
    for h in range(KV_HEADS):
        sl = slice(h * HEAD_DIM, (h + 1) * HEAD_DIM)
        zs = slice(qw + h * HEAD_DIM, qw + (h + 1) * HEAD_DIM)
        k_ref[:, sl] = norm_rope(z_ref[:, zs], kg_ref[...]).astype(k_ref.dtype)
    kvw = KV_HEADS * HEAD_DIM
    ones = jnp.ones((z_ref.shape[0], HEAD_DIM), v_ref.dtype)
    for h in range(KV_HEADS):
        zs = slice(qw + kvw + h * HEAD_DIM, qw + kvw + (h + 1) * HEAD_DIM)
        v_ref[:, 2 * h * HEAD_DIM:(2 * h + 1) * HEAD_DIM] = z_ref[:, zs].astype(v_ref.dtype)
        v_ref[:, (2 * h + 1) * HEAD_DIM:(2 * h + 2) * HEAD_DIM] = ones


def qkv_prep(z, cosf, sinf, q_gain, k_gain, attn_width, tm=256):
    N = z.shape[0]
    n_q = attn_width // HEAD_DIM
    kvw = KV_HEADS * HEAD_DIM
    zw = attn_width + 2 * kvw
    return pl.pallas_call(
        functools.partial(_qkv_prep_kernel, n_q, HEAD_DIM ** -0.5 * math.log2(math.e)),
        grid=(N // tm,),
        in_specs=[pl.BlockSpec((tm, zw), lambda i: (i, 0)),
                  pl.BlockSpec((tm, HEAD_DIM), lambda i: (i, 0)), pl.BlockSpec((tm, HEAD_DIM), lambda i: (i, 0)),
                  pl.BlockSpec((1, HEAD_DIM), lambda i: (0, 0)), pl.BlockSpec((1, HEAD_DIM), lambda i: (0, 0))],
        out_specs=[pl.BlockSpec((tm, attn_width), lambda i: (i, 0)),
                   pl.BlockSpec((tm, kvw), lambda i: (i, 0)), pl.BlockSpec((tm, 2 * kvw), lambda i: (i, 0))],
        out_shape=[jax.ShapeDtypeStruct((N, attn_width), BF16),
                   jax.ShapeDtypeStruct((N, kvw), BF16), jax.ShapeDtypeStruct((N, 2 * kvw), BF16)],
        compiler_params=_params(("parallel",)), name="qkv_prep",
    )(z, cosf, sinf, q_gain.reshape(1, HEAD_DIM), k_gain.reshape(1, HEAD_DIM))


def _attn_kernel(n_ctx, n_ctx_tiles, group, n_cast, q_ref, k_ref, v_ref, *rest):
    cast_src, o_ref, cast_dst = rest[:n_cast], rest[n_cast], rest[n_cast + 1:]
    tq = q_ref.shape[0]
    hh = group // 2
    halves = [jnp.concatenate([q_ref[:, h * HEAD_DIM:(h + 1) * HEAD_DIM] for h in range(i * hh, (i + 1) * hh)], axis=0)
              for i in range(2)]

    def attend(k, v):
        s = [_dot_nt(qh, k) for qh in halves]
        p = [jnp.exp2(x - jnp.max(x, axis=-1, keepdims=True)).astype(v.dtype) for x in s]
        acc = [_dot(x, v) for x in p]
        for h in range(group):
            a = acc[h // hh][(h % hh) * tq:(h % hh + 1) * tq]
            o_ref[:, h * HEAD_DIM:(h + 1) * HEAD_DIM] = (a[:, :HEAD_DIM] / a[:, HEAD_DIM:]).astype(o_ref.dtype)

    is_ctx = pl.program_id(1) < n_ctx_tiles

    @pl.when(is_ctx)
    def _():
        attend(k_ref[0:n_ctx, :], v_ref[0:n_ctx, :])

    @pl.when(jnp.logical_not(is_ctx))
    def _():
        attend(k_ref[...], v_ref[...])

    for src, dst in zip(cast_src, cast_dst):
        dst[...] = src[...].astype(dst.dtype)


def attention(q, k, v, n_ctx, cast=(), tq=128):
    N, aw = q.shape
    group = aw // HEAD_DIM // KV_HEADS
    gw = group * HEAD_DIM
    assert N % tq == 0 and n_ctx % tq == 0
    n_q = N // tq
    n_steps = KV_HEADS * n_q
    cast_in, cast_out, cast_shape, cast_args = [], [], [], []
    for a, layer in cast:
        _, rows, cols = a.shape
        rb = -(-(-(-rows // n_steps)) // 16) * 16
        last = -(-rows // rb) - 1
        if rows % rb == 0:
            a2, first = a.reshape(-1, cols), layer * (rows // rb)
        else:
            a2, first = a[layer], 0
        cast_args.append(a2)
        cast_in.append(pl.BlockSpec((rb, cols), lambda g, i, last=last, first=first:
                                    (first + jnp.minimum(g * n_q + i, last), 0)))
        cast_out.append(pl.BlockSpec((rb, cols), lambda g, i, last=last: (jnp.minimum(g * n_q + i, last), 0)))
        cast_shape.append(jax.ShapeDtypeStruct((rows, cols), BF16))
    res = pl.pallas_call(
        functools.partial(_attn_kernel, n_ctx, n_ctx // tq, group, len(cast)),
        grid=(KV_HEADS, n_q),
        in_specs=[pl.BlockSpec((tq, gw), lambda g, i: (i, g)),
                  pl.BlockSpec((N, HEAD_DIM), lambda g, i: (0, g)),
                  pl.BlockSpec((N, 2 * HEAD_DIM), lambda g, i: (0, g))] + cast_in,
        out_specs=[pl.BlockSpec((tq, gw), lambda g, i: (i, g))] + cast_out,
        out_shape=[jax.ShapeDtypeStruct((N, aw), BF16)] + cast_shape,
        compiler_params=_params(("arbitrary", "arbitrary")), name="attention",
    )(q, k, v, *cast_args)
    return res[0], tuple(res[1:])


def rope_tables(n_ctx, n_lat):
    n_rows = n_lat // GRID_W
    row = jnp.repeat(jnp.arange(n_rows), GRID_W).astype(F32)
    col = jnp.tile(jnp.arange(GRID_W), n_rows).astype(F32)
    axis_dim = HEAD_DIM // 2
    inv_freq = ROPE_THETA ** (-jnp.arange(0, axis_dim, 2, dtype=F32) / axis_dim)
    ang = jnp.concatenate([row[:, None] * inv_freq, col[:, None] * inv_freq], -1)
    cos, sin = jnp.cos(ang), jnp.sin(ang)
    cosf = jnp.concatenate([cos, cos], -1)
    sinf = jnp.concatenate([-sin, sin], -1)
    cosf = jnp.concatenate([jnp.ones((n_ctx, HEAD_DIM), F32), cosf], 0)
    sinf = jnp.concatenate([jnp.zeros((n_ctx, HEAD_DIM), F32), sinf], 0)
    return cosf, sinf


def _seg_sum64(x, e128):
    return _dot_exact_rhs(x, e128)


def _rwkv_feat_kernel(n_ctx_tiles, n_tiles, w_cols,
                      z_ref, zp_ref, zn_ref, mu_ref, kk_ref, ka_ref, w0_ref, a0_ref,
                      wup_ref, aup_ref, gup_ref, e_ref,
                      r_ref, v_ref, kap_ref, g_ref, lwf_ref, bf_ref, ktf_ref, lwb_ref, bb_ref, ktb_ref,
                      scr_ref):
    i = pl.program_id(0)
    tm = z_ref.shape[0]
    first = jnp.logical_or(i == 0, i == n_ctx_tiles)
    last = jnp.logical_or(i == n_ctx_tiles - 1, i == n_tiles - 1)
    scr_ref[8:8 + tm, :] = z_ref[...]
    scr_ref[0:8, :] = jnp.where(first, 0.0, zp_ref[...])
    scr_ref[8 + tm:16 + tm, :] = jnp.where(last, 0.0, zn_ref[...])
    z = z_ref[...]
    prev = scr_ref[7:7 + tm, :]
    nxt = scr_ref[9:9 + tm, :]
    zs = z + mu_ref[0:1, :] * (prev - z) + mu_ref[1:2, :] * (nxt - z)

    r = zs[:, 0:w_cols]
    k = zs[:, w_cols:2 * w_cols]
    v = zs[:, 2 * w_cols:3 * w_cols]
    lora = zs[:, 3 * w_cols:]
    wd = jnp.tanh(lora[:, 0:LANES])
    ad = lora[:, LANES:2 * LANES]
    gd = _sigmoid(lora[:, 2 * LANES:3 * LANES])
    r_ref[...] = r.astype(r_ref.dtype)
    v_ref[...] = v.astype(v_ref.dtype)
    g_ref[...] = _dot(gd.astype(BF16), gup_ref[...]).astype(g_ref.dtype)
    kk = k * kk_ref[...]
    e128 = e_ref[...]
    kap = jnp.concatenate(
        [kk[:, s:s + LANES] * lax.rsqrt(jnp.maximum(_seg_sum64(kk[:, s:s + LANES] * kk[:, s:s + LANES], e128), 1e-24))
         for s in range(0, w_cols, LANES)], axis=1)
    kap_ref[...] = kap.astype(kap_ref.dtype)
    outs = ((lwf_ref, bf_ref, ktf_ref), (lwb_ref, bb_ref, ktb_ref))
    for d in range(2):
        lw_ref, b_ref, kt_ref = outs[d]
        lw_ref[...] = -W_DECAY_SCALE * _sigmoid(w0_ref[d:d + 1, :] + _dot(wd.astype(BF16), wup_ref[d]))
        a = _sigmoid(a0_ref[d:d + 1, :] + _dot(ad.astype(BF16), aup_ref[d]))
        kt_ref[...] = (k * (1.0 + (a - 1.0) * ka_ref[...])).astype(kt_ref.dtype)
        b_ref[...] = (a * kap).astype(b_ref.dtype)


def _seg_ones(width=LANES, seg=RWKV_HEAD):
    i = np.arange(width)
    return jnp.asarray((i[:, None] // seg == i[None, :] // seg).astype(np.float32)).astype(BF16)


def rwkv_features(zr, n_ctx, mu, w0, w_up, a0, a_up, g_up, k_k, k_a, tm=256):
    N, zw = zr.shape
    wc = k_k.shape[0]
    assert zw == 3 * wc + 3 * LANES and W_RANK + W_RANK == LANES and A_RANK + A_RANK == LANES and G_RANK == LANES
    zeros = jnp.zeros((W_RANK, wc), F32)
    wup = jnp.stack([jnp.concatenate([w_up[0], zeros], 0), jnp.concatenate([zeros, w_up[1]], 0)], 0)
    aup = jnp.stack([jnp.concatenate([a_up[0], zeros], 0), jnp.concatenate([zeros, a_up[1]], 0)], 0)
    n_tiles = N // tm
    t8 = tm // 8
    row = lambda i: (i, 0)
    full = lambda i: (0, 0)
    sds = lambda dt: jax.ShapeDtypeStruct((N, wc), dt)
    return pl.pallas_call(
        functools.partial(_rwkv_feat_kernel, n_ctx // tm, n_tiles, wc),
        grid=(n_tiles,),
        in_specs=[pl.BlockSpec((tm, zw), row),
                  pl.BlockSpec((8, zw), lambda i: (jnp.maximum(i * t8 - 1, 0), 0)),
                  pl.BlockSpec((8, zw), lambda i: (jnp.minimum((i + 1) * t8, N // 8 - 1), 0)),
                  pl.BlockSpec((2, zw), full), pl.BlockSpec((1, wc), full), pl.BlockSpec((1, wc), full),
                  pl.BlockSpec((2, wc), full), pl.BlockSpec((2, wc), full),
                  pl.BlockSpec((2, LANES, wc), lambda i: (0, 0, 0)), pl.BlockSpec((2, LANES, wc), lambda i: (0, 0, 0)),
                  pl.BlockSpec((LANES, wc), full), pl.BlockSpec((LANES, LANES), full)],
        out_specs=[pl.BlockSpec((tm, wc), row)] * 10,
        out_shape=[sds(BF16)] * 4 + [sds(F32), sds(BF16), sds(BF16)] * 2,
        scratch_shapes=[pltpu.VMEM((tm + 16, zw), F32)],
        compiler_params=_params(("parallel",)), name="rwkv_features",
    )(zr, zr, zr, mu, k_k.reshape(1, wc), k_a.reshape(1, wc), w0, a0,
      wup.astype(BF16), aup.astype(BF16), g_up.astype(BF16), _seg_ones())


def _rwkv_scan_kernel(rev, pairs, n_sub, r_ref, kap_ref, v_ref, lw_ref, b_ref, kt_ref, y_ref, s_ref):
    L = CHUNK
    L2 = 2 * L

    @pl.when(pl.program_id(1) == 0)
    def _():
        s_ref[...] = jnp.zeros(s_ref.shape, F32)

    t_i = lax.broadcasted_iota(jnp.int32, (L, L), 0)
    s_i = lax.broadcasted_iota(jnp.int32, (L, L), 1)
    m_incl64 = ((s_i >= t_i) if rev else (s_i <= t_i)).astype(F32)
    ri = lax.broadcasted_iota(jnp.int32, (L2, L2), 0)
    qi = lax.broadcasted_iota(jnp.int32, (L2, L2), 1)
    rt, qt = ri % L, qi % L
    same = (ri // L) == (qi // L)
    incl = jnp.logical_and(same, (qt >= rt) if rev else (qt <= rt))
    strict = jnp.logical_and(same, (qt > rt) if rev else (qt < rt))
    eye = (ri == qi).astype(F32)
    lane = lax.broadcasted_iota(jnp.int32, (L, LANES), 1)
    hm = [(lane < RWKV_HEAD).astype(F32), (lane >= RWKV_HEAD).astype(F32)]
    bd = ((lax.broadcasted_iota(jnp.int32, (LANES, LANES), 0) // RWKV_HEAD)
          == (lax.broadcasted_iota(jnp.int32, (LANES, LANES), 1) // RWKV_HEAD)).astype(F32)
    lvl_masks = []
    bsz = 1
    while bsz < L:
        grp = (ri // (2 * bsz)) == (qi // (2 * bsz))
        r_odd = (ri // bsz) % 2 == 1
        q_odd = (qi // bsz) % 2 == 1
        off = jnp.logical_and(jnp.logical_not(r_odd), q_odd) if rev else jnp.logical_and(r_odd, jnp.logical_not(q_odd))
        lvl_masks.append(jnp.logical_and(grp, off))
        bsz *= 2

    def stack2(x):
        return jnp.concatenate([x * hm[0], x * hm[1]], axis=0)

    def bmm(a, b):
        return _dot(a.astype(BF16), b.astype(BF16))

    m_incl_bf = m_incl64.astype(BF16)

    units = [(ci, p) for ci in range(n_sub) for p in range(pairs)]

    def blk(ref, unit):
        ci, p = unit
        cc = (n_sub - 1 - ci) if rev else ci
        return ref[cc * L:(cc + 1) * L, p * LANES:(p + 1) * LANES]

    def each(fn, *dicts):
        return {un: fn(*(d[un] for d in dicts)) for un in units}

    def split3(x):
        hi = x.astype(BF16)
        r1 = x - hi.astype(F32)
        mid = r1.astype(BF16)
        lo = (r1 - mid.astype(F32)).astype(BF16)
        return jnp.concatenate([hi, mid, lo], axis=1)

    lw = {un: blk(lw_ref, un) for un in units}
    c3 = each(lambda x: _dot(m_incl_bf, split3(x)), lw)
    cum = each(lambda c: c[:, 0:LANES] + c[:, LANES:2 * LANES] + c[:, 2 * LANES:], c3)
    e_incl = each(jnp.exp, cum)
    e_inv = each(lambda c: jnp.exp(-c), cum)
    p_tot = each(lambda e: e[0:1, :] if rev else e[L - 1:L, :], e_incl)
    xk = {un: stack2(blk(kap_ref, un) * jnp.exp(cum[un] - lw[un])).astype(BF16) for un in units}
    xr = {un: stack2(blk(r_ref, un) * e_incl[un]).astype(BF16) for un in units}
    yb = {un: stack2(blk(b_ref, un) * e_inv[un]).astype(BF16) for un in units}
    yk = {un: stack2(blk(kt_ref, un) * e_inv[un]).astype(BF16) for un in units}
    vs = {un: stack2(blk(v_ref, un)).astype(BF16) for un in units}
    amat = each(lambda a, b, c, d: _dot_nt(jnp.concatenate([a, b], axis=0), jnp.concatenate([c, d], axis=0)),
                xk, xr, yb, yk)
    a_ub = each(lambda a: jnp.where(strict, a[0:L2, 0:L2], 0.0), amat)
    a_uk = each(lambda a: jnp.where(strict, a[0:L2, L2:], 0.0).astype(BF16), amat)
    a_rb = each(lambda a: jnp.where(incl, a[L2:, 0:L2], 0.0).astype(BF16), amat)
    a_rk = each(lambda a: jnp.where(incl, a[L2:, L2:], 0.0).astype(BF16), amat)
    tinv = each(lambda a: eye - jnp.where(lvl_masks[0], a, 0.0), a_ub)
    for lm in lvl_masks[1:]:
        tb = each(lambda t: t.astype(BF16), tinv)
        x1 = each(lambda a, t: _dot(jnp.where(lm, a, 0.0).astype(BF16), t).astype(BF16), a_ub, tb)
        tinv = each(lambda t, tbf, x: t - _dot(tbf, x), tinv, tb, x1)
    tb = each(lambda t: t.astype(BF16), tinv)
    w1 = each(lambda t, x: _dot(t, x).astype(BF16), tb, xk)
    avs = each(lambda a, v: _dot(a, v).astype(BF16), a_uk, vs)
    y_ind = each(_dot, a_rk, vs)
    k2 = each(_dot_tn, vs, yk)
    w2 = each(_dot, tb, avs)

    state = [s_ref[p] for p in range(pairs)]
    for ci in range(n_sub):
        wx = [_dot_nt(jnp.concatenate([w1[(ci, p)], xr[(ci, p)]], axis=0), state[p].astype(BF16))
              for p in range(pairs)]
        ub = [(-(wx[p][0:L2] + w2[(ci, p)])).astype(BF16) for p in range(pairs)]
        upd = [_dot_tn(ub[p], yb[(ci, p)]) + k2[(ci, p)] for p in range(pairs)]
        state = [(state[p] + upd[p] * bd) * p_tot[(ci, p)] for p in range(pairs)]
        cc = (n_sub - 1 - ci) if rev else ci
        for p in range(pairs):
            ystk = wx[p][L2:] + _dot(a_rb[(ci, p)], ub[p]) + y_ind[(ci, p)]
            y_ref[cc * L:(cc + 1) * L, p * LANES:(p + 1) * LANES] = ystk[0:L] + ystk[L:]
    for p in range(pairs):
        s_ref[p] = state[p]


def rwkv_scan(rev, n_ctx, r, kap, v, lw, b, kt, pairs=4, n_sub=4):
    N, wc = r.shape
    rb = n_sub * CHUNK
    bw = pairs * LANES
    assert N % rb == 0 and n_ctx % rb == 0 and wc % bw == 0
    nb, nbc = N // rb, n_ctx // rb
    if rev:
        rmap = lambda h, i: (jnp.where(i < nbc, nbc - 1 - i, nb - 1 - (i - nbc)), h)
    else:
        rmap = lambda h, i: (i, h)
    spec = pl.BlockSpec((rb, bw), rmap)
    return pl.pallas_call(
        functools.partial(_rwkv_scan_kernel, rev, pairs, n_sub),
        grid=(wc // bw, nb),
        in_specs=[spec] * 6, out_specs=spec,
        out_shape=jax.ShapeDtypeStruct((N, wc), F32),
        scratch_shapes=[pltpu.VMEM((pairs, LANES, LANES), F32)],
        compiler_params=_params(("parallel", "arbitrary")), name="rwkv_scan_rev" if rev else "rwkv_scan_fwd",
    )(r, kap, v, lw, b, kt)


def _rwkv_out_kernel(yf_ref, yb_ref, r_ref, v_ref, ktf_ref, ktb_ref, g_ref, rk_ref, gn_ref, e_ref, o_ref):
    e128 = e_ref[...]
    inv = 1.0 / RWKV_HEAD
    for s in range(0, o_ref.shape[1], LANES):
        sl = slice(s, s + LANES)
        y = yf_ref[:, sl] + yb_ref[:, sl]
        mu = _seg_sum64(y, e128) * inv
        yc = y - mu
        var = _seg_sum64(yc * yc, e128) * inv
        yn = yc * lax.rsqrt(var + GN_EPS) * gn_ref[0:1, sl] + gn_ref[1:2, sl]
        rk = r_ref[:, sl] * rk_ref[0:1, sl]
        bonus = _seg_sum64(rk * (ktf_ref[:, sl].astype(F32) + ktb_ref[:, sl]), e128) * v_ref[:, sl]
        o_ref[:, sl] = ((yn + bonus) * g_ref[:, sl]).astype(o_ref.dtype)


def rwkv_output(y_f, y_b, r, v, kt_f, kt_b, g, r_k, lnx_gain, lnx_bias, tm=256):
    N, wc = r.shape
    row = lambda i: (i, 0)
    full = lambda i: (0, 0)
    gn = jnp.stack([lnx_gain, lnx_bias], 0)
    return pl.pallas_call(
        _rwkv_out_kernel, grid=(N // tm,),
        in_specs=[pl.BlockSpec((tm, wc), row)] * 7 + [pl.BlockSpec((1, wc), full), pl.BlockSpec((2, wc), full),
                                                      pl.BlockSpec((LANES, LANES), full)],
        out_specs=pl.BlockSpec((tm, wc), row),
        out_shape=jax.ShapeDtypeStruct((N, wc), BF16),
        compiler_params=_params(("parallel",)), name="rwkv_output",
    )(y_f, y_b, r, v, kt_f, kt_b, g, r_k.reshape(1, wc), gn, _seg_ones())


def _angles(i, j, period):
    return (2.0 * math.pi / period) * ((i * j) % period).astype(np.float64)


def _chan_dft(gdim, scale):
    i = np.arange(gdim)
    ang = _angles(i[:, None], i[None, :], gdim)
    return jnp.asarray(np.concatenate([np.cos(ang), -np.sin(ang)], 1) * scale, F32)


def _fnet1_kernel(groups, x_ref, cs_ref, m_ref, y_ref):
    r1 = x_ref.shape[0]
    gdim = x_ref.shape[1] // groups
    for g in range(groups):
        sl = slice(g * gdim, (g + 1) * gdim)
        ab = _dot3(x_ref[:, sl], cs_ref[...])
        st = jnp.concatenate([ab[:, :gdim], ab[:, gdim:]], axis=0)
        y = _dot3(m_ref[0], st)
        y_ref[0, 0, :, sl] = y[:r1]
        y_ref[1, 0, :, sl] = y[r1:]


def _fnet3_kernel(cs_ref, y_ref, o_ref):
    o_ref[...] = _dot3(cs_ref[...], y_ref[...])


def fnet_latent(zf, groups=FOURIER_GROUPS):
    T, fw = zf.shape
    gdim = fw // groups
    r2 = GRID_W
    r1 = T // r2
    cs = _chan_dft(gdim, 1.0 / math.sqrt(T * gdim))
    t1p = np.arange(r1)[None, :, None]
    t1 = np.arange(r1)[None, None, :]
    t2 = np.arange(r2)[:, None, None]
    th = _angles(t1p, r2 * t1 + t2, T)
    gr, gi = np.cos(th), -np.sin(th)
    mt = jnp.asarray(np.concatenate([np.concatenate([gr, -gi], 2), np.concatenate([gi, gr], 2)], 1), F32)
    y = pl.pallas_call(
        functools.partial(_fnet1_kernel, groups),
        grid=(r2,),
        in_specs=[pl.BlockSpec((r1, fw), lambda j: (0, j)),
                  pl.BlockSpec((gdim, 2 * gdim), lambda j: (0, 0)),
                  pl.BlockSpec((1, 2 * r1, 2 * r1), lambda j: (j, 0, 0))],
        out_specs=pl.BlockSpec((2, 1, r1, fw), lambda j: (0, j, 0, 0)),
        out_shape=jax.ShapeDtypeStruct((2, r2, r1, fw), F32),
        compiler_params=_params(("parallel",)), name="fnet_rows",
    )(zf.reshape(r1, r2 * fw), cs, mt)
    i2 = np.arange(r2)
    ph = _angles(i2[:, None], i2[None, :], r2)
    cs2 = jnp.asarray(np.concatenate([np.cos(ph), np.sin(ph)], 1), F32)
    tc = 4 * fw if r1 % 4 == 0 else fw
    o = pl.pallas_call(
        _fnet3_kernel, grid=(r1 * fw // tc,),
        in_specs=[pl.BlockSpec((r2, 2 * r2), lambda j: (0, 0)), pl.BlockSpec((2 * r2, tc), lambda j: (0, j))],
        out_specs=pl.BlockSpec((r2, tc), lambda j: (0, j)),
        out_shape=jax.ShapeDtypeStruct((r2, r1 * fw), F32),
        compiler_params=_params(("parallel",)), name="fnet_cols",
    )(cs2, y.reshape(2 * r2, r1 * fw))
    return o.reshape(T, fw)


def _fnet_direct_kernel(x_ref, cs_ref, ct_ref, o_ref):
    gdim = x_ref.shape[1]
    ab = _dot3(x_ref[...], cs_ref[...])
    st = jnp.concatenate([ab[:, :gdim], ab[:, gdim:]], axis=0)
    o_ref[...] = _dot3(ct_ref[...], st)


def fnet_direct(zf, groups=FOURIER_GROUPS):
    C, fw = zf.shape
    gdim = fw // groups
    cs = _chan_dft(gdim, 1.0 / math.sqrt(C * gdim))
    i = np.arange(C)
    ang = _angles(i[:, None], i[None, :], C)
    ct = jnp.asarray(np.concatenate([np.cos(ang), np.sin(ang)], 1), F32)
    return pl.pallas_call(
        _fnet_direct_kernel, grid=(groups,),
        in_specs=[pl.BlockSpec((C, gdim), lambda g: (0, g)), pl.BlockSpec((gdim, 2 * gdim), lambda g: (0, 0)),
                  pl.BlockSpec((C, 2 * C), lambda g: (0, 0))],
        out_specs=pl.BlockSpec((C, gdim), lambda g: (0, g)),
        out_shape=jax.ShapeDtypeStruct((C, fw), F32),
        compiler_params=_params(("parallel",)), name="fnet_direct",
    )(zf, cs, ct)


GATHER_DMA_PRIORITY = 1
GATE_LANE0 = N_GROUPS


def _gates_kernel(lg_ref, id_ref, gv_ref):
    lg = lg_ref[...]
    lane = lax.broadcasted_iota(jnp.int32, lg.shape, 1)
    big = jnp.int32(LANES)
    neg = -jnp.inf
    is_g = lane < N_GROUPS
    gl = jnp.where(is_g, lg, neg)
    mg = jnp.max(gl, axis=-1, keepdims=True)
    p_group = 1.0 / jnp.sum(jnp.exp(gl - mg), axis=-1, keepdims=True)
    gsel = jnp.min(jnp.where(gl == mg, lane, big), axis=-1, keepdims=True)
    e_idx = lane - GATE_LANE0
    in_grp = jnp.logical_and(jnp.logical_and(e_idx >= 0, e_idx < N_EXPERTS), e_idx // EXPERTS_PER_GROUP == gsel)
    le = jnp.where(in_grp, lg, neg)
    m1 = jnp.max(le, axis=-1, keepdims=True)
    l1 = jnp.min(jnp.where(le == m1, lane, big), axis=-1, keepdims=True)
    le2 = jnp.where(lane == l1, neg, le)
    m2 = jnp.max(le2, axis=-1, keepdims=True)
    l2 = jnp.min(jnp.where(le2 == m2, lane, big), axis=-1, keepdims=True)
    e2 = jnp.exp(m2 - m1)
    inv = p_group / (1.0 + e2)
    id_ref[...] = jnp.where(lane == 0, l1 - GATE_LANE0, jnp.where(lane == 1, l2 - GATE_LANE0, 0))
    gv_ref[...] = jnp.where(lane == 0, inv, jnp.where(lane == 1, e2 * inv, 0.0))


def moe_gates(logits, tm=256):
    N = logits.shape[0]
    spec = pl.BlockSpec((tm, LANES), lambda i: (i, 0))
    return pl.pallas_call(
        _gates_kernel, grid=(N // tm,),
        in_specs=[spec], out_specs=[spec, spec],
        out_shape=[jax.ShapeDtypeStruct((N, LANES), jnp.int32), jax.ShapeDtypeStruct((N, LANES), F32)],
        compiler_params=_params(("parallel",)), name="moe_gates",
    )(logits)


def moe_dispatch(expert_ids, ts):
    N = expert_ids.shape[0]
    flat = expert_ids.reshape(-1)
    onehot = (flat[:, None] == jnp.arange(N_EXPERTS, dtype=jnp.int32)[None, :]).astype(jnp.int32)
    csum = jnp.cumsum(onehot, axis=0)
    rank = jnp.take_along_axis(csum, flat[:, None], axis=1)[:, 0] - 1
    counts = csum[-1]
    tiles_per = (counts + ts - 1) // ts
    tiles_end = jnp.cumsum(tiles_per)
    pad_start = (tiles_end - tiles_per) * ts
    slot = pad_start[flat] + rank
    n_tiles = -(-2 * N // ts) + N_EXPERTS
    slot_token = jnp.zeros((n_tiles * ts,), jnp.int32).at[slot].set(jnp.arange(2 * N, dtype=jnp.int32) // 2)
    n_used = tiles_end[-1]
    tile_idx = jnp.minimum(jnp.arange(n_tiles, dtype=jnp.int32), n_used - 1)
    tile_expert = jnp.sum((tile_idx[:, None] >= tiles_end[None, :]).astype(jnp.int32), axis=1)
    return slot_token, slot.reshape(N, 2), tile_expert.astype(jnp.int32), n_used.reshape(1).astype(jnp.int32)


def _moe_group_kernel(ts, n_chunks, te_ref, nu_ref, st_ref, u_hbm, w1_ref, w3_ref, w2_ref, o_ref, ubuf, ub16, sem):
    i = pl.program_id(0)
    j = pl.program_id(1)
    used = nu_ref[0]
    share = ts // n_chunks

    def row_copy(slot, b, r):
        return pltpu.make_async_copy(u_hbm.at[pl.ds(st_ref[slot], 1), :], ubuf.at[b, pl.ds(r, 1), :], sem.at[b])

    @pl.when(jnp.logical_and(j == 0, i == 0))
    def _():
        def row(r, carry):
            row_copy(r, 0, r).start(priority=GATHER_DMA_PRIORITY)
            return carry
        lax.fori_loop(0, ts, row, 0)

    @pl.when(jnp.logical_and(j == 0, i <= used))
    def _():
        pltpu.make_async_copy(u_hbm.at[pl.ds(0, ts), :], ubuf.at[i % 2], sem.at[i % 2]).wait()

    @pl.when(jnp.logical_and(j == 0, i < used))
    def _():
        ub16[...] = ubuf[i % 2].astype(BF16)

    @pl.when(j == 0)
    def _():
        o_ref[...] = jnp.zeros(o_ref.shape, o_ref.dtype)

    @pl.when(i < used)
    def _():
        nb = (i + 1) % 2
        r0 = j * share
        for r in range(share):
            row_copy((i + 1) * ts + r0 + r, nb, r0 + r).start(priority=GATHER_DMA_PRIORITY)
        u = ub16[...]
        a = _dot(u, w1_ref[0, 0])
        g = _dot(u, w3_ref[0, 0])
        h = ((a * _sigmoid(a)) * g).astype(BF16)
        nw = o_ref.shape[1] // 4
        for n in range(0, o_ref.shape[1], nw):
            o_ref[:, n:n + nw] += _dot(h, w2_ref[0, 0, :, n:n + nw])


def moe_grouped(u, slot_token, tile_expert, n_used, w1, w3, w2, layer, ts, tc=256):
    N, D = u.shape
    _, E, _, De = w1.shape
    n_tiles = tile_expert.shape[0]
    grid_spec = pltpu.PrefetchScalarGridSpec(
        num_scalar_prefetch=3,
        grid=(n_tiles, De // tc),
        in_specs=[pl.BlockSpec(memory_space=pl.ANY),
                  pl.BlockSpec((1, 1, D, tc), lambda i, j, te, nu, st: (layer, te[i], 0, j)),
                  pl.BlockSpec((1, 1, D, tc), lambda i, j, te, nu, st: (layer, te[i], 0, j)),
                  pl.BlockSpec((1, 1, tc, D), lambda i, j, te, nu, st: (layer, te[i], j, 0))],
        out_specs=pl.BlockSpec((ts, D), lambda i, j, te, nu, st: (i, 0)),
        scratch_shapes=[pltpu.VMEM((2, ts, D), F32), pltpu.VMEM((ts, D), BF16), pltpu.SemaphoreType.DMA((2,))],
    )
    return pl.pallas_call(
        functools.partial(_moe_group_kernel, ts, De // tc), grid_spec=grid_spec,
        out_shape=jax.ShapeDtypeStruct((n_tiles * ts, D), F32),
        compiler_params=_params(("arbitrary", "arbitrary")), name="moe_grouped",
    )(tile_expert, n_used, slot_token, u, w1, w3, w2)


def _moe_combine_kernel(alpha, n_ctx_tiles, first_tile, sa_ref, sb_ref, o_hbm, x_ref, gv_ref, g_ref, gb_ref, out_ref,
                        abuf, bbuf, sem):
    i = pl.program_id(0)
    tm = x_ref.shape[0]

    def gather(tile, b):
        def row(r, carry):
            t = (first_tile + tile) * tm + r
            pltpu.make_async_copy(o_hbm.at[pl.ds(sa_ref[t], 1), :], abuf.at[b, pl.ds(r, 1), :], sem.at[0, b]).start()
            pltpu.make_async_copy(o_hbm.at[pl.ds(sb_ref[t], 1), :], bbuf.at[b, pl.ds(r, 1), :], sem.at[1, b]).start(
                priority=GATHER_DMA_PRIORITY)
            return carry
        lax.fori_loop(0, tm, row, 0)

    @pl.when(i == 0)
    def _():
        gather(0, 0)

    @pl.when(i + 1 < pl.num_programs(0))
    def _():
        gather(i + 1, (i + 1) % 2)

    b = i % 2
    pltpu.make_async_copy(o_hbm.at[pl.ds(0, tm), :], abuf.at[b], sem.at[0, b]).wait()
    pltpu.make_async_copy(o_hbm.at[pl.ds(0, tm), :], bbuf.at[b], sem.at[1, b]).wait()
    gv = gv_ref[...]
    f = gv[:, 0:1] * abuf[b] + gv[:, 1:2] * bbuf[b]
    g = jnp.where(first_tile + i < n_ctx_tiles, g_ref[1:2, :], g_ref[0:1, :])
    y = alpha * x_ref[...] + g * f
    out_ref[...] = _ln_rows(y, LN_EPS) * gb_ref[0:1, :] + gb_ref[1:2, :]


def moe_combine_ln(xx, o, slots, gate_vals, gates2, gain_bias, n_ctx, alpha, latent_only=False, tm=256):
    N, D = xx.shape
    first = n_ctx // tm if latent_only else 0
    grid_spec = pltpu.PrefetchScalarGridSpec(
        num_scalar_prefetch=2,
        grid=(N // tm - first,),
        in_specs=[pl.BlockSpec(memory_space=pl.ANY),
                  pl.BlockSpec((tm, D), lambda i, sa, sb: (first + i, 0)),
                  pl.BlockSpec((tm, LANES), lambda i, sa, sb: (first + i, 0)),
                  pl.BlockSpec((2, D), lambda i, sa, sb: (0, 0)), pl.BlockSpec((2, D), lambda i, sa, sb: (0, 0))],
        out_specs=pl.BlockSpec((tm, D), lambda i, sa, sb: (i, 0)),
        scratch_shapes=[pltpu.VMEM((2, tm, D), F32), pltpu.VMEM((2, tm, D), F32), pltpu.SemaphoreType.DMA((2, 2))],
    )
    return pl.pallas_call(
        functools.partial(_moe_combine_kernel, alpha, n_ctx // tm, first), grid_spec=grid_spec,
        out_shape=jax.ShapeDtypeStruct((N - first * tm, D), F32),
        compiler_params=_params(("arbitrary",)), name="moe_combine_ln",
    )(slots[:, 0], slots[:, 1], o, xx, gate_vals, gates2, gain_bias)


MOE_SLOT_TILE = 512

def _pick_tile(n, prefer):
    for t in prefer:
        if n % t == 0:
            return t
    raise ValueError(f"no tile for {n}")


def kernel(x, c, ctx, c_ctx, w_mod, b_mod, w_in, q_gain, k_gain, rwkv_mu, w0, w_up, a0, a_up, g_up, k_k, k_a, r_k, lnx_gain, lnx_bias, w_out, ln1_gain, ln1_bias, ln2_gain, ln2_bias, router_group_w, router_group_b, router_expert_w, router_expert_b, w1, w3, w2):
    B, T, D = x.shape
    C = ctx.shape[1]
    assert B == 1
    depth = w_mod.shape[0]
    alpha = (2 * depth) ** 0.25
    N = C + T
    aw = D // 2
    kvw = KV_HEADS * HEAD_DIM
    rw = D // 4
    fw = D // 4
    rcols = 3 * rw + 2 * W_RANK + 2 * A_RANK + G_RANK
    o_zr = aw + 2 * kvw
    o_zf = o_zr + rcols
    tm_big = _pick_tile(N, (768, 384, 256))

    xx = jnp.concatenate([ctx[0], x[0]], 0)
    mods = mod_vectors(c, c_ctx, w_mod, b_mod).reshape(depth, 8, 6, D)
    cosf, sinf = rope_tables(C, T)
    w_qkv = w_in[:, :, :o_zr].astype(BF16)
    w_rwkv = w_in[:, :, o_zr:o_zf].astype(BF16)
    w_four = w_in[:, :, o_zf:].astype(BF16)
    n_exp, _, d_exp = w1.shape[1:]
    side_cast = (w1.reshape(depth, n_exp * D, d_exp), w3.reshape(depth, n_exp * D, d_exp),
                 w2.reshape(depth, n_exp * d_exp, D), w_out)

    for l in range(depth):
        last = l == depth - 1
        mv = mods[l]
        vec = lambda i: jnp.stack([mv[0, i], mv[1, i]], 0)
        ss1 = jnp.stack([mv[0, 0], mv[0, 1], mv[1, 0], mv[1, 1]], 0)
        ss2 = jnp.stack([mv[0, 3], mv[0, 4], mv[1, 3], mv[1, 4]], 0)

        u = ln_modulate(xx, ss1, C)
        zq = matmul(u, w_qkv, l, F32, tm_big, _pick_tile(o_zr, (768, 512, 256, 128)))
        zr = matmul(u, w_rwkv, l, F32, tm_big, _pick_tile(rcols, (1152, 384, 128)))
        zf = matmul(u, w_four, l, F32, tm_big, _pick_tile(fw, (512, 256, 128)))
        qh, kh, vh = qkv_prep(zq, cosf, sinf, q_gain[l], k_gain[l], aw)
        attn, (w1_b, w3_b, w2_b, w_out_b) = attention(qh, kh, vh, C, cast=[(a, l) for a in side_cast])
        w1_b = w1_b.reshape(1, n_exp, D, d_exp)
        w3_b = w3_b.reshape(1, n_exp, D, d_exp)
        w2_b = w2_b.reshape(1, n_exp, d_exp, D)
        w_out_b = w_out_b[None]
        r, v, kap, g, lwf, bf, ktf, lwb, bb, ktb = rwkv_features(
            zr, C, rwkv_mu[l], w0[l], w_up[l], a0[l], a_up[l], g_up[l], k_k[l], k_a[l])
        y_f = rwkv_scan(False, C, r, kap, v, lwf, bf, ktf)
        y_b = rwkv_scan(True, C, r, kap, v, lwb, bb, ktb)
        rwkv = rwkv_output(y_f, y_b, r, v, ktf, ktb, g, r_k[l].reshape(-1), lnx_gain[l], lnx_bias[l])
        fl = fnet_latent(zf[C:])
        fc = fnet_direct(zf[:C]) if not last else jnp.zeros((C, fw), F32)
        fn = jnp.concatenate([fc, fl], 0).astype(BF16)
        m = matmul_parts([attn, rwkv, fn], w_out_b, 0, F32, tm_big, _pick_tile(D, (512, 256, 128)))
        xx = resid_ln(xx, m, vec(2), jnp.stack([ln1_gain[l], ln1_bias[l]], 0), C, alpha)

        wr = jnp.concatenate([router_group_w[l], router_expert_w[l],
                              jnp.zeros((D, LANES - N_GROUPS - N_EXPERTS), F32)], 1)
        br = jnp.concatenate([router_group_b[l], router_expert_b[l],
                              jnp.zeros((LANES - N_GROUPS - N_EXPERTS,), F32)], 0).reshape(1, LANES)
        u2, logits = ln_modulate(xx, ss2, C, router=(wr, br), out_dtype=F32)
        expert_ids, gate_vals = moe_gates(logits)
        slot_token, slots, tile_expert, n_used = moe_dispatch(expert_ids[:, :2], MOE_SLOT_TILE)
        o = moe_grouped(u2, slot_token, tile_expert, n_used, w1_b, w3_b, w2_b, 0, MOE_SLOT_TILE)
        xx = moe_combine_ln(xx, o, slots, gate_vals, vec(5), jnp.stack([ln2_gain[l], ln2_bias[l]], 0), C, alpha,
                            latent_only=last)
    return xx[None]
```

```python
import functools
import math

import numpy as np
import jax
import jax.numpy as jnp
from jax import lax
from jax.experimental import pallas as pl
from jax.experimental.pallas import tpu as pltpu

F32 = jnp.float32
BF16 = jnp.bfloat16

GRID_W = 64
HEAD_DIM = 128
KV_HEADS = 4
RWKV_HEAD = 64
W_RANK = 64
A_RANK = 64
G_RANK = 128
FOURIER_GROUPS = 4
N_GROUPS = 4
EXPERTS_PER_GROUP = 4
N_EXPERTS = N_GROUPS * EXPERTS_PER_GROUP
ROPE_THETA = 10000.0
W_DECAY_SCALE = math.exp(-0.5)
GN_EPS = 64e-5
LN_EPS = 1e-6
LANES = 128
SLAB_PITCH = 40
CHUNK = 64

V7X_VMEM_LIMIT_MB = 56


def _params(sem, vmem_mb=V7X_VMEM_LIMIT_MB):
    return pltpu.CompilerParams(dimension_semantics=sem, vmem_limit_bytes=vmem_mb * 1024 * 1024)


def _dot(a, b, prec=None):
    return jnp.dot(a, b, preferred_element_type=F32, precision=prec)


def _dot_nt(a, b, prec=None):
    return lax.dot_general(a, b, (((1,), (1,)), ((), ())), preferred_element_type=F32, precision=prec)


def _dot_tn(a, b, prec=None):
    return lax.dot_general(a, b, (((0,), (0,)), ((), ())), preferred_element_type=F32, precision=prec)


def _sigmoid(x):
    return 1.0 / (1.0 + jnp.exp(-x))


def _split2(x):
    hi = x.astype(BF16)
    return hi, (x - hi.astype(F32)).astype(BF16)


def _dot3(a, b):
    a_hi, a_lo = _split2(a)
    b_hi, b_lo = _split2(b)
    return _dot(a_hi, b_hi) + (_dot(a_hi, b_lo) + _dot(a_lo, b_hi))


def _dot_exact_rhs(x, e):
    hi = x.astype(BF16)
    r1 = x - hi.astype(F32)
    mid = r1.astype(BF16)
    lo = (r1 - mid.astype(F32)).astype(BF16)
    return _dot(hi, e) + (_dot(mid, e) + _dot(lo, e))


def _mod_kernel(a_ref, w_ref, b_ref, o_ref):
    tn = o_ref.shape[2]
    rows = []
    for r in range(2):
        cols = []
        for j in range(tn // LANES):
            prod = w_ref[0, :, j * LANES:(j + 1) * LANES] * a_ref[r]
            cols.append(jnp.sum(prod, axis=0, keepdims=True))
        rows.append(jnp.concatenate(cols, axis=1) + b_ref[0])
    rows.append(jnp.zeros((6, tn), F32))
    o_ref[0] = jnp.concatenate(rows, axis=0)


def mod_vectors(c, c_ctx, w_mod, b_mod):
    L, D, D6 = w_mod.shape
    acts = jnp.stack([jax.nn.silu(c[0]), jax.nn.silu(c_ctx)], 0)
    a_b = jnp.broadcast_to(acts[:, :, None], (2, D, LANES))
    tn = 512
    return pl.pallas_call(
        _mod_kernel,
        grid=(L, D6 // tn),
        in_specs=[pl.BlockSpec((2, D, LANES), lambda l, j: (0, 0, 0)),
                  pl.BlockSpec((1, D, tn), lambda l, j: (l, 0, j)),
                  pl.BlockSpec((1, 1, tn), lambda l, j: (l, 0, j))],
        out_specs=pl.BlockSpec((1, 8, tn), lambda l, j: (l, 0, j)),
        out_shape=jax.ShapeDtypeStruct((L, 8, D6), F32),
        compiler_params=_params(("parallel", "parallel")),
        name="mod_vectors",
    )(a_b, w_mod, b_mod.reshape(L, 1, D6))


def _ln_rows(x, eps):
    mu = jnp.mean(x, axis=-1, keepdims=True)
    xc = x - mu
    var = jnp.mean(xc * xc, axis=-1, keepdims=True)
    return xc * lax.rsqrt(var + eps)


def _ln_mod_kernel(n_ctx_tiles, with_router, x_ref, ss_ref, *rest):
    if with_router:
        wr_ref, br_ref, u_ref, lg_ref = rest
    else:
        (u_ref,) = rest
    is_ctx = pl.program_id(0) < n_ctx_tiles
    sh = jnp.where(is_ctx, ss_ref[2:3, :], ss_ref[0:1, :])
    sc = jnp.where(is_ctx, ss_ref[3:4, :], ss_ref[1:2, :])
    u = _ln_rows(x_ref[...], LN_EPS) * (1.0 + sc) + sh
    if with_router:
        tm = x_ref.shape[0]
        for c in range(SLAB_PITCH):
            chunk = u[:, c * LANES:(c + 1) * LANES] if c < u.shape[1] // LANES else jnp.zeros((tm, LANES), F32)
            u_ref[pl.ds(c, tm, stride=SLAB_PITCH), :] = chunk
        lg_ref[...] = _dot3(u, wr_ref[...]) + br_ref[...]
    else:
        u_ref[...] = u.astype(u_ref.dtype)


def ln_modulate(xx, ss, n_ctx, router=None, tm=256):
    N, D = xx.shape
    assert N % tm == 0 and n_ctx % tm == 0 and D // LANES <= SLAB_PITCH
    in_specs = [pl.BlockSpec((tm, D), lambda i: (i, 0)), pl.BlockSpec((4, D), lambda i: (0, 0))]
    if router is None:
        out_specs = [pl.BlockSpec((tm, D), lambda i: (i, 0))]
        out_shape = [jax.ShapeDtypeStruct((N, D), BF16)]
    else:
        out_specs = [pl.BlockSpec((tm * SLAB_PITCH, LANES), lambda i: (i, 0))]
        out_shape = [jax.ShapeDtypeStruct((N * SLAB_PITCH, LANES), F32)]
    args = [xx, ss]
    if router is not None:
        in_specs += [pl.BlockSpec((D, LANES), lambda i: (0, 0)), pl.BlockSpec((1, LANES), lambda i: (0, 0))]
        out_specs.append(pl.BlockSpec((tm, LANES), lambda i: (i, 0)))
        out_shape.append(jax.ShapeDtypeStruct((N, LANES), F32))
        args += list(router)
    res = pl.pallas_call(
        functools.partial(_ln_mod_kernel, n_ctx // tm, router is not None),
        grid=(N // tm,), in_specs=in_specs, out_specs=out_specs, out_shape=out_shape,
        compiler_params=_params(("parallel",)), name="ln_modulate",
    )(*args)
    return res if router is not None else res[0]


def _resid_ln_kernel(alpha, n_ctx_tiles, x_ref, m_ref, g_ref, gb_ref, o_ref):
    is_ctx = pl.program_id(0) < n_ctx_tiles
    g = jnp.where(is_ctx, g_ref[1:2, :], g_ref[0:1, :])
    y = alpha * x_ref[...] + g * m_ref[...].astype(F32)
    o_ref[...] = _ln_rows(y, LN_EPS) * gb_ref[0:1, :] + gb_ref[1:2, :]


def resid_ln(xx, m, gates2, gain_bias, n_ctx, alpha, tm=256):
    N, D = xx.shape
    return pl.pallas_call(
        functools.partial(_resid_ln_kernel, alpha, n_ctx // tm),
        grid=(N // tm,),
        in_specs=[pl.BlockSpec((tm, D), lambda i: (i, 0)), pl.BlockSpec((tm, D), lambda i: (i, 0)),
                  pl.BlockSpec((2, D), lambda i: (0, 0)), pl.BlockSpec((2, D), lambda i: (0, 0))],
        out_specs=pl.BlockSpec((tm, D), lambda i: (i, 0)),
        out_shape=jax.ShapeDtypeStruct((N, D), F32),
        compiler_params=_params(("parallel",)), name="resid_ln",
    )(xx, m, gates2, gain_bias)


def _mm_kernel(a_ref, b_ref, o_ref):
    o_ref[...] = _dot(a_ref[...], b_ref[0]).astype(o_ref.dtype)


def matmul(a, b, layer, out_dtype, tm, tn):
    M, K = a.shape
    _, _, Nn = b.shape
    assert M % tm == 0 and Nn % tn == 0
    return pl.pallas_call(
        _mm_kernel, grid=(M // tm, Nn // tn),
        in_specs=[pl.BlockSpec((tm, K), lambda i, j: (i, 0)), pl.BlockSpec((1, K, tn), lambda i, j: (layer, 0, j))],
        out_specs=pl.BlockSpec((tm, tn), lambda i, j: (i, j)),
        out_shape=jax.ShapeDtypeStruct((M, Nn), out_dtype),
        compiler_params=_params(("parallel", "arbitrary")), name="matmul",
    )(a, b)


def _mm_parts_kernel(n_parts, *refs):
    a_refs, b_refs, o_ref = refs[:n_parts], refs[n_parts:2 * n_parts], refs[2 * n_parts]
    acc = _dot(a_refs[0][...], b_refs[0][0])
    for a_ref, b_ref in zip(a_refs[1:], b_refs[1:]):
        acc = acc + _dot(a_ref[...], b_ref[0])
    o_ref[...] = acc.astype(o_ref.dtype)


def matmul_parts(parts, b, layer, out_dtype, tm, tn):
    M = parts[0].shape[0]
    Nn = b.shape[2]
    a_specs, b_specs, off = [], [], 0
    for p in parts:
        kp = p.shape[1]
        assert off % kp == 0
        a_specs.append(pl.BlockSpec((tm, kp), lambda i, j: (i, 0)))
        b_specs.append(pl.BlockSpec((1, kp, tn), lambda i, j, r=off // kp: (layer, r, j)))
        off += kp
    assert off == b.shape[1] and M % tm == 0 and Nn % tn == 0
    return pl.pallas_call(
        functools.partial(_mm_parts_kernel, len(parts)), grid=(M // tm, Nn // tn),
        in_specs=a_specs + b_specs,
        out_specs=pl.BlockSpec((tm, tn), lambda i, j: (i, j)),
        out_shape=jax.ShapeDtypeStruct((M, Nn), out_dtype),
        compiler_params=_params(("parallel", "arbitrary")), name="matmul_parts",
    )(*parts, *([b] * len(parts)))


def _qkv_prep_kernel(n_q_heads, scale, z_ref, cos_ref, sin_ref, qg_ref, kg_ref, q_ref, k_ref, v_ref):
    cosf = cos_ref[...]
    sinf = sin_ref[...]

    def norm_rope(t, gain):
        t = t * lax.rsqrt(jnp.mean(t * t, axis=-1, keepdims=True) + LN_EPS) * gain
        return t * cosf + pltpu.roll(t, HEAD_DIM // 2, 1) * sinf

    for h in range(n_q_heads):
        sl = slice(h * HEAD_DIM, (h + 1) * HEAD_DIM)
        q_ref[:, sl] = (norm_rope(z_ref[:, sl], qg_ref[...]) * scale).astype(q_ref.dtype)
    qw = n_q_heads * HEAD_DIM
    for h in range(KV_HEADS):
        sl = slice(h * HEAD_DIM, (h + 1) * HEAD_DIM)
        zs = slice(qw + h * HEAD_DIM, qw + (h + 1) * HEAD_DIM)
        k_ref[:, sl] = norm_rope(z_ref[:, zs], kg_ref[...]).astype(k_ref.dtype)
    kvw = KV_HEADS * HEAD_DIM
    ones = jnp.ones((z_ref.shape[0], HEAD_DIM), v_ref.dtype)
    for h in range(KV_HEADS):
        zs = slice(qw + kvw + h * HEAD_DIM, qw + kvw + (h + 1) * HEAD_DIM)
        v_ref[:, 2 * h * HEAD_DIM:(2 * h + 1) * HEAD_DIM] = z_ref[:, zs].astype(v_ref.dtype)
        v_ref[:, (2 * h + 1) * HEAD_DIM:(2 * h + 2) * HEAD_DIM] = ones


def qkv_prep(z, cosf, sinf, q_gain, k_gain, attn_width, tm=256):
    N = z.shape[0]
    n_q = attn_width // HEAD_DIM
    kvw = KV_HEADS * HEAD_DIM
    zw = attn_width + 2 * kvw
    return pl.pallas_call(
        functools.partial(_qkv_prep_kernel, n_q, HEAD_DIM ** -0.5 * math.log2(math.e)),
        grid=(N // tm,),
        in_specs=[pl.BlockSpec((tm, zw), lambda i: (i, 0)),
                  pl.BlockSpec((tm, HEAD_DIM), lambda i: (i, 0)), pl.BlockSpec((tm, HEAD_DIM), lambda i: (i, 0)),
                  pl.BlockSpec((1, HEAD_DIM), lambda i: (0, 0)), pl.BlockSpec((1, HEAD_DIM), lambda i: (0, 0))],
        out_specs=[pl.BlockSpec((tm, attn_width), lambda i: (i, 0)),
                   pl.BlockSpec((tm, kvw), lambda i: (i, 0)), pl.BlockSpec((tm, 2 * kvw), lambda i: (i, 0))],
        out_shape=[jax.ShapeDtypeStruct((N, attn_width), BF16),
                   jax.ShapeDtypeStruct((N, kvw), BF16), jax.ShapeDtypeStruct((N, 2 * kvw), BF16)],
        compiler_params=_params(("parallel",)), name="qkv_prep",
    )(z, cosf, sinf, q_gain.reshape(1, HEAD_DIM), k_gain.reshape(1, HEAD_DIM))


def _attn_kernel(n_ctx, n_ctx_tiles, group, n_cast, q_ref, k_ref, v_ref, *rest):
    cast_src, o_ref, cast_dst = rest[:n_cast], rest[n_cast], rest[n_cast + 1:]
    tq = q_ref.shape[0]
    hh = group // 2
    halves = [jnp.concatenate([q_ref[:, h * HEAD_DIM:(h + 1) * HEAD_DIM] for h in range(i * hh, (i + 1) * hh)], axis=0)
              for i in range(2)]

    def attend(k, v):
        s = [_dot_nt(qh, k) for qh in halves]
        p = [jnp.exp2(x - jnp.max(x, axis=-1, keepdims=True)).astype(v.dtype) for x in s]
        acc = [_dot(x, v) for x in p]
        for h in range(group):
            a = acc[h // hh][(h % hh) * tq:(h % hh + 1) * tq]
            o_ref[:, h * HEAD_DIM:(h + 1) * HEAD_DIM] = (a[:, :HEAD_DIM] / a[:, HEAD_DIM:]).astype(o_ref.dtype)

    is_ctx = pl.program_id(1) < n_ctx_tiles

    @pl.when(is_ctx)
    def _():
        attend(k_ref[0:n_ctx, :], v_ref[0:n_ctx, :])

    @pl.when(jnp.logical_not(is_ctx))
    def _():
        attend(k_ref[...], v_ref[...])

    for src, dst in zip(cast_src, cast_dst):
        dst[...] = src[...].astype(dst.dtype)


def attention(q, k, v, n_ctx, cast=(), tq=128):
    N, aw = q.shape
    group = aw // HEAD_DIM // KV_HEADS
    gw = group * HEAD_DIM
    assert N % tq == 0 and n_ctx % tq == 0
    n_q = N // tq
    n_steps = KV_HEADS * n_q
    cast_in, cast_out, cast_shape, cast_args = [], [], [], []
    for a, layer in cast:
        _, rows, cols = a.shape
        rb = -(-(-(-rows // n_steps)) // 16) * 16
        last = -(-rows // rb) - 1
        if rows % rb == 0:
            a2, first = a.reshape(-1, cols), layer * (rows // rb)
        else:
            a2, first = a[layer], 0
        cast_args.append(a2)
        cast_in.append(pl.BlockSpec((rb, cols), lambda g, i, last=last, first=first:
                                    (first + jnp.minimum(g * n_q + i, last), 0)))
        cast_out.append(pl.BlockSpec((rb, cols), lambda g, i, last=last: (jnp.minimum(g * n_q + i, last), 0)))
        cast_shape.append(jax.ShapeDtypeStruct((rows, cols), BF16))
    res = pl.pallas_call(
        functools.partial(_attn_kernel, n_ctx, n_ctx // tq, group, len(cast)),
        grid=(KV_HEADS, n_q),
        in_specs=[pl.BlockSpec((tq, gw), lambda g, i: (i, g)),
                  pl.BlockSpec((N, HEAD_DIM), lambda g, i: (0, g)),
                  pl.BlockSpec((N, 2 * HEAD_DIM), lambda g, i: (0, g))] + cast_in,
        out_specs=[pl.BlockSpec((tq, gw), lambda g, i: (i, g))] + cast_out,
        out_shape=[jax.ShapeDtypeStruct((N, aw), BF16)] + cast_shape,
        compiler_params=_params(("arbitrary", "arbitrary")), name="attention",
    )(q, k, v, *cast_args)
    return res[0], tuple(res[1:])


def rope_tables(n_ctx, n_lat):
    n_rows = n_lat // GRID_W
    row = jnp.repeat(jnp.arange(n_rows), GRID_W).astype(F32)
    col = jnp.tile(jnp.arange(GRID_W), n_rows).astype(F32)
    axis_dim = HEAD_DIM // 2
    inv_freq = ROPE_THETA ** (-jnp.arange(0, axis_dim, 2, dtype=F32) / axis_dim)
    ang = jnp.concatenate([row[:, None] * inv_freq, col[:, None] * inv_freq], -1)
    cos, sin = jnp.cos(ang), jnp.sin(ang)
    cosf = jnp.concatenate([cos, cos], -1)
    sinf = jnp.concatenate([-sin, sin], -1)
    cosf = jnp.concatenate([jnp.ones((n_ctx, HEAD_DIM), F32), cosf], 0)
    sinf = jnp.concatenate([jnp.zeros((n_ctx, HEAD_DIM), F32), sinf], 0)
    return cosf, sinf


def _seg_sum64(x, e128):
    return _dot_exact_rhs(x, e128)


def _rwkv_feat_kernel(n_ctx_tiles, n_tiles, w_cols,
                      z_ref, zp_ref, zn_ref, mu_ref, kk_ref, ka_ref, w0_ref, a0_ref,
                      wup_ref, aup_ref, gup_ref, e_ref,
                      r_ref, v_ref, kap_ref, g_ref, lwf_ref, bf_ref, ktf_ref, lwb_ref, bb_ref, ktb_ref,
                      scr_ref):
    i = pl.program_id(0)
    tm = z_ref.shape[0]
    first = jnp.logical_or(i == 0, i == n_ctx_tiles)
    last = jnp.logical_or(i == n_ctx_tiles - 1, i == n_tiles - 1)
    scr_ref[8:8 + tm, :] = z_ref[...]
    scr_ref[0:8, :] = jnp.where(first, 0.0, zp_ref[...])
    scr_ref[8 + tm:16 + tm, :] = jnp.where(last, 0.0, zn_ref[...])
    z = z_ref[...]
    prev = scr_ref[7:7 + tm, :]
    nxt = scr_ref[9:9 + tm, :]
    zs = z + mu_ref[0:1, :] * (prev - z) + mu_ref[1:2, :] * (nxt - z)

    r = zs[:, 0:w_cols]
    k = zs[:, w_cols:2 * w_cols]
    v = zs[:, 2 * w_cols:3 * w_cols]
    lora = zs[:, 3 * w_cols:]
    wd = jnp.tanh(lora[:, 0:LANES])
    ad = lora[:, LANES:2 * LANES]
    gd = _sigmoid(lora[:, 2 * LANES:3 * LANES])
    r_ref[...] = r.astype(r_ref.dtype)
    v_ref[...] = v.astype(v_ref.dtype)
    g_ref[...] = _dot(gd.astype(BF16), gup_ref[...]).astype(g_ref.dtype)
    kk = k * kk_ref[...]
    e128 = e_ref[...]
    kap = jnp.concatenate(
        [kk[:, s:s + LANES] * lax.rsqrt(jnp.maximum(_seg_sum64(kk[:, s:s + LANES] * kk[:, s:s + LANES], e128), 1e-24))
         for s in range(0, w_cols, LANES)], axis=1)
    kap_ref[...] = kap.astype(kap_ref.dtype)
    outs = ((lwf_ref, bf_ref, ktf_ref), (lwb_ref, bb_ref, ktb_ref))
    for d in range(2):
        lw_ref, b_ref, kt_ref = outs[d]
        lw_ref[...] = -W_DECAY_SCALE * _sigmoid(w0_ref[d:d + 1, :] + _dot(wd.astype(BF16), wup_ref[d]))
        a = _sigmoid(a0_ref[d:d + 1, :] + _dot(ad.astype(BF16), aup_ref[d]))
        kt_ref[...] = (k * (1.0 + (a - 1.0) * ka_ref[...])).astype(kt_ref.dtype)
        b_ref[...] = (a * kap).astype(b_ref.dtype)


def _seg_ones(width=LANES, seg=RWKV_HEAD):
    i = np.arange(width)
    return jnp.asarray((i[:, None] // seg == i[None, :] // seg).astype(np.float32)).astype(BF16)


def rwkv_features(zr, n_ctx, mu, w0, w_up, a0, a_up, g_up, k_k, k_a, tm=256):
    N, zw = zr.shape
    wc = k_k.shape[0]
    assert zw == 3 * wc + 3 * LANES and W_RANK + W_RANK == LANES and A_RANK + A_RANK == LANES and G_RANK == LANES
    zeros = jnp.zeros((W_RANK, wc), F32)
    wup = jnp.stack([jnp.concatenate([w_up[0], zeros], 0), jnp.concatenate([zeros, w_up[1]], 0)], 0)
    aup = jnp.stack([jnp.concatenate([a_up[0], zeros], 0), jnp.concatenate([zeros, a_up[1]], 0)], 0)
    n_tiles = N // tm
    t8 = tm // 8
    row = lambda i: (i, 0)
    full = lambda i: (0, 0)
    sds = lambda dt: jax.ShapeDtypeStruct((N, wc), dt)
    return pl.pallas_call(
        functools.partial(_rwkv_feat_kernel, n_ctx // tm, n_tiles, wc),
        grid=(n_tiles,),
        in_specs=[pl.BlockSpec((tm, zw), row),
                  pl.BlockSpec((8, zw), lambda i: (jnp.maximum(i * t8 - 1, 0), 0)),
                  pl.BlockSpec((8, zw), lambda i: (jnp.minimum((i + 1) * t8, N // 8 - 1), 0)),
                  pl.BlockSpec((2, zw), full), pl.BlockSpec((1, wc), full), pl.BlockSpec((1, wc), full),
                  pl.BlockSpec((2, wc), full), pl.BlockSpec((2, wc), full),
                  pl.BlockSpec((2, LANES, wc), lambda i: (0, 0, 0)), pl.BlockSpec((2, LANES, wc), lambda i: (0, 0, 0)),
                  pl.BlockSpec((LANES, wc), full), pl.BlockSpec((LANES, LANES), full)],
        out_specs=[pl.BlockSpec((tm, wc), row)] * 10,
        out_shape=[sds(BF16)] * 4 + [sds(F32), sds(BF16), sds(BF16)] * 2,
        scratch_shapes=[pltpu.VMEM((tm + 16, zw), F32)],
        compiler_params=_params(("parallel",)), name="rwkv_features",
    )(zr, zr, zr, mu, k_k.reshape(1, wc), k_a.reshape(1, wc), w0, a0,
      wup.astype(BF16), aup.astype(BF16), g_up.astype(BF16), _seg_ones())


def _rwkv_scan_kernel(rev, pairs, n_sub, r_ref, kap_ref, v_ref, lw_ref, b_ref, kt_ref, y_ref, s_ref):
    L = CHUNK
    L2 = 2 * L

    @pl.when(pl.program_id(1) == 0)
    def _():
        s_ref[...] = jnp.zeros(s_ref.shape, F32)

    t_i = lax.broadcasted_iota(jnp.int32, (L, L), 0)
    s_i = lax.broadcasted_iota(jnp.int32, (L, L), 1)
    m_incl64 = ((s_i >= t_i) if rev else (s_i <= t_i)).astype(F32)
    ri = lax.broadcasted_iota(jnp.int32, (L2, L2), 0)
    qi = lax.broadcasted_iota(jnp.int32, (L2, L2), 1)
    rt, qt = ri % L, qi % L
    same = (ri // L) == (qi // L)
    incl = jnp.logical_and(same, (qt >= rt) if rev else (qt <= rt))
    strict = jnp.logical_and(same, (qt > rt) if rev else (qt < rt))
    eye = (ri == qi).astype(F32)
    lane = lax.broadcasted_iota(jnp.int32, (L, LANES), 1)
    hm = [(lane < RWKV_HEAD).astype(F32), (lane >= RWKV_HEAD).astype(F32)]
    bd = ((lax.broadcasted_iota(jnp.int32, (LANES, LANES), 0) // RWKV_HEAD)
          == (lax.broadcasted_iota(jnp.int32, (LANES, LANES), 1) // RWKV_HEAD)).astype(F32)
    lvl_masks = []
    bsz = 1
    while bsz < L:
        grp = (ri // (2 * bsz)) == (qi // (2 * bsz))
        r_odd = (ri // bsz) % 2 == 1
        q_odd = (qi // bsz) % 2 == 1
        off = jnp.logical_and(jnp.logical_not(r_odd), q_odd) if rev else jnp.logical_and(r_odd, jnp.logical_not(q_odd))
        lvl_masks.append(jnp.logical_and(grp, off))
        bsz *= 2

    def stack2(x):
        return jnp.concatenate([x * hm[0], x * hm[1]], axis=0)

    def bmm(a, b):
        return _dot(a.astype(BF16), b.astype(BF16))

    m_incl_bf = m_incl64.astype(BF16)

    units = [(ci, p) for ci in range(n_sub) for p in range(pairs)]

    def blk(ref, unit):
        ci, p = unit
        cc = (n_sub - 1 - ci) if rev else ci
        return ref[cc * L:(cc + 1) * L, p * LANES:(p + 1) * LANES]

    def each(fn, *dicts):
        return {un: fn(*(d[un] for d in dicts)) for un in units}

    def split3(x):
        hi = x.astype(BF16)
        r1 = x - hi.astype(F32)
        mid = r1.astype(BF16)
        lo = (r1 - mid.astype(F32)).astype(BF16)
        return jnp.concatenate([hi, mid, lo], axis=1)

    lw = {un: blk(lw_ref, un) for un in units}
    c3 = each(lambda x: _dot(m_incl_bf, split3(x)), lw)
    cum = each(lambda c: c[:, 0:LANES] + c[:, LANES:2 * LANES] + c[:, 2 * LANES:], c3)
    e_incl = each(jnp.exp, cum)
    e_inv = each(lambda c: jnp.exp(-c), cum)
    p_tot = each(lambda e: e[0:1, :] if rev else e[L - 1:L, :], e_incl)
    xk = {un: stack2(blk(kap_ref, un) * jnp.exp(cum[un] - lw[un])).astype(BF16) for un in units}
    xr = {un: stack2(blk(r_ref, un) * e_incl[un]).astype(BF16) for un in units}
    yb = {un: stack2(blk(b_ref, un) * e_inv[un]).astype(BF16) for un in units}
    yk = {un: stack2(blk(kt_ref, un) * e_inv[un]).astype(BF16) for un in units}
    vs = {un: stack2(blk(v_ref, un)).astype(BF16) for un in units}
    amat = each(lambda a, b, c, d: _dot_nt(jnp.concatenate([a, b], axis=0), jnp.concatenate([c, d], axis=0)),
                xk, xr, yb, yk)
    a_ub = each(lambda a: jnp.where(strict, a[0:L2, 0:L2], 0.0), amat)
    a_uk = each(lambda a: jnp.where(strict, a[0:L2, L2:], 0.0).astype(BF16), amat)
    a_rb = each(lambda a: jnp.where(incl, a[L2:, 0:L2], 0.0).astype(BF16), amat)
    a_rk = each(lambda a: jnp.where(incl, a[L2:, L2:], 0.0).astype(BF16), amat)
    tinv = each(lambda a: eye - jnp.where(lvl_masks[0], a, 0.0), a_ub)
    for lm in lvl_masks[1:]:
        tb = each(lambda t: t.astype(BF16), tinv)
        x1 = each(lambda a, t: _dot(jnp.where(lm, a, 0.0).astype(BF16), t).astype(BF16), a_ub, tb)
        tinv = each(lambda t, tbf, x: t - _dot(tbf, x), tinv, tb, x1)
    tb = each(lambda t: t.astype(BF16), tinv)
    w1 = each(lambda t, x: _dot(t, x).astype(BF16), tb, xk)
    avs = each(lambda a, v: _dot(a, v).astype(BF16), a_uk, vs)
    y_ind = each(_dot, a_rk, vs)
    k2 = each(_dot_tn, vs, yk)
    w2 = each(_dot, tb, avs)

    state = [s_ref[p] for p in range(pairs)]
    for ci in range(n_sub):
        wx = [_dot_nt(jnp.concatenate([w1[(ci, p)], xr[(ci, p)]], axis=0), state[p].astype(BF16))
              for p in range(pairs)]
        ub = [(-(wx[p][0:L2] + w2[(ci, p)])).astype(BF16) for p in range(pairs)]
        upd = [_dot_tn(ub[p], yb[(ci, p)]) + k2[(ci, p)] for p in range(pairs)]
        state = [(state[p] + upd[p] * bd) * p_tot[(ci, p)] for p in range(pairs)]
        cc = (n_sub - 1 - ci) if rev else ci
        for p in range(pairs):
            ystk = wx[p][L2:] + _dot(a_rb[(ci, p)], ub[p]) + y_ind[(ci, p)]
            y_ref[cc * L:(cc + 1) * L, p * LANES:(p + 1) * LANES] = ystk[0:L] + ystk[L:]
    for p in range(pairs):
        s_ref[p] = state[p]


def rwkv_scan(rev, n_ctx, r, kap, v, lw, b, kt, pairs=4, n_sub=4):
    N, wc = r.shape
    rb = n_sub * CHUNK
    bw = pairs * LANES
    assert N % rb == 0 and n_ctx % rb == 0 and wc % bw == 0
    nb, nbc = N // rb, n_ctx // rb
    if rev:
        rmap = lambda h, i: (jnp.where(i < nbc, nbc - 1 - i, nb - 1 - (i - nbc)), h)
    else:
        rmap = lambda h, i: (i, h)
    spec = pl.BlockSpec((rb, bw), rmap)
    return pl.pallas_call(
        functools.partial(_rwkv_scan_kernel, rev, pairs, n_sub),
        grid=(wc // bw, nb),
        in_specs=[spec] * 6, out_specs=spec,
        out_shape=jax.ShapeDtypeStruct((N, wc), F32),
        scratch_shapes=[pltpu.VMEM((pairs, LANES, LANES), F32)],
        compiler_params=_params(("parallel", "arbitrary")), name="rwkv_scan_rev" if rev else "rwkv_scan_fwd",
    )(r, kap, v, lw, b, kt)


def _rwkv_out_kernel(yf_ref, yb_ref, r_ref, v_ref, ktf_ref, ktb_ref, g_ref, rk_ref, gn_ref, e_ref, o_ref):
    e128 = e_ref[...]
    inv = 1.0 / RWKV_HEAD
    for s in range(0, o_ref.shape[1], LANES):
        sl = slice(s, s + LANES)
        y = yf_ref[:, sl] + yb_ref[:, sl]
        mu = _seg_sum64(y, e128) * inv
        yc = y - mu
        var = _seg_sum64(yc * yc, e128) * inv
        yn = yc * lax.rsqrt(var + GN_EPS) * gn_ref[0:1, sl] + gn_ref[1:2, sl]
        rk = r_ref[:, sl] * rk_ref[0:1, sl]
        bonus = _seg_sum64(rk * (ktf_ref[:, sl].astype(F32) + ktb_ref[:, sl]), e128) * v_ref[:, sl]
        o_ref[:, sl] = ((yn + bonus) * g_ref[:, sl]).astype(o_ref.dtype)


def rwkv_output(y_f, y_b, r, v, kt_f, kt_b, g, r_k, lnx_gain, lnx_bias, tm=256):
    N, wc = r.shape
    row = lambda i: (i, 0)
    full = lambda i: (0, 0)
    gn = jnp.stack([lnx_gain, lnx_bias], 0)
    return pl.pallas_call(
        _rwkv_out_kernel, grid=(N // tm,),
        in_specs=[pl.BlockSpec((tm, wc), row)] * 7 + [pl.BlockSpec((1, wc), full), pl.BlockSpec((2, wc), full),
                                                      pl.BlockSpec((LANES, LANES), full)],
        out_specs=pl.BlockSpec((tm, wc), row),
        out_shape=jax.ShapeDtypeStruct((N, wc), BF16),
        compiler_params=_params(("parallel",)), name="rwkv_output",
    )(y_f, y_b, r, v, kt_f, kt_b, g, r_k.reshape(1, wc), gn, _seg_ones())


def _angles(i, j, period):
    return (2.0 * math.pi / period) * ((i * j) % period).astype(np.float64)


def _chan_dft(gdim, scale):
    i = np.arange(gdim)
    ang = _angles(i[:, None], i[None, :], gdim)
    return jnp.asarray(np.concatenate([np.cos(ang), -np.sin(ang)], 1) * scale, F32)


def _fnet1_kernel(groups, x_ref, cs_ref, m_ref, y_ref):
    r1 = x_ref.shape[0]
    gdim = x_ref.shape[1] // groups
    for g in range(groups):
        sl = slice(g * gdim, (g + 1) * gdim)
        ab = _dot3(x_ref[:, sl], cs_ref[...])
        st = jnp.concatenate([ab[:, :gdim], ab[:, gdim:]], axis=0)
        y = _dot3(m_ref[0], st)
        y_ref[0, 0, :, sl] = y[:r1]
        y_ref[1, 0, :, sl] = y[r1:]


def _fnet3_kernel(cs_ref, y_ref, o_ref):
    o_ref[...] = _dot3(cs_ref[...], y_ref[...])


def fnet_latent(zf, groups=FOURIER_GROUPS):
    T, fw = zf.shape
    gdim = fw // groups
    r2 = GRID_W
    r1 = T // r2
    cs = _chan_dft(gdim, 1.0 / math.sqrt(T * gdim))
    t1p = np.arange(r1)[None, :, None]
    t1 = np.arange(r1)[None, None, :]
    t2 = np.arange(r2)[:, None, None]
    th = _angles(t1p, r2 * t1 + t2, T)
    gr, gi = np.cos(th), -np.sin(th)
    mt = jnp.asarray(np.concatenate([np.concatenate([gr, -gi], 2), np.concatenate([gi, gr], 2)], 1), F32)
    y = pl.pallas_call(
        functools.partial(_fnet1_kernel, groups),
        grid=(r2,),
        in_specs=[pl.BlockSpec((r1, fw), lambda j: (0, j)),
                  pl.BlockSpec((gdim, 2 * gdim), lambda j: (0, 0)),
                  pl.BlockSpec((1, 2 * r1, 2 * r1), lambda j: (j, 0, 0))],
        out_specs=pl.BlockSpec((2, 1, r1, fw), lambda j: (0, j, 0, 0)),
        out_shape=jax.ShapeDtypeStruct((2, r2, r1, fw), F32),
        compiler_params=_params(("parallel",)), name="fnet_rows",
    )(zf.reshape(r1, r2 * fw), cs, mt)
    i2 = np.arange(r2)
    ph = _angles(i2[:, None], i2[None, :], r2)
    cs2 = jnp.asarray(np.concatenate([np.cos(ph), np.sin(ph)], 1), F32)
    tc = 4 * fw if r1 % 4 == 0 else fw
    o = pl.pallas_call(
        _fnet3_kernel, grid=(r1 * fw // tc,),
        in_specs=[pl.BlockSpec((r2, 2 * r2), lambda j: (0, 0)), pl.BlockSpec((2 * r2, tc), lambda j: (0, j))],
        out_specs=pl.BlockSpec((r2, tc), lambda j: (0, j)),
        out_shape=jax.ShapeDtypeStruct((r2, r1 * fw), F32),
        compiler_params=_params(("parallel",)), name="fnet_cols",
    )(cs2, y.reshape(2 * r2, r1 * fw))
    return o.reshape(T, fw)


def _fnet_direct_kernel(x_ref, cs_ref, ct_ref, o_ref):
    gdim = x_ref.shape[1]
    ab = _dot3(x_ref[...], cs_ref[...])
    st = jnp.concatenate([ab[:, :gdim], ab[:, gdim:]], axis=0)
    o_ref[...] = _dot3(ct_ref[...], st)


def fnet_direct(zf, groups=FOURIER_GROUPS):
    C, fw = zf.shape
    gdim = fw // groups
    cs = _chan_dft(gdim, 1.0 / math.sqrt(C * gdim))
    i = np.arange(C)
    ang = _angles(i[:, None], i[None, :], C)
    ct = jnp.asarray(np.concatenate([np.cos(ang), np.sin(ang)], 1), F32)
    return pl.pallas_call(
        _fnet_direct_kernel, grid=(groups,),
        in_specs=[pl.BlockSpec((C, gdim), lambda g: (0, g)), pl.BlockSpec((gdim, 2 * gdim), lambda g: (0, 0)),
                  pl.BlockSpec((C, 2 * C), lambda g: (0, 0))],
        out_specs=pl.BlockSpec((C, gdim), lambda g: (0, g)),
        out_shape=jax.ShapeDtypeStruct((C, fw), F32),
        compiler_params=_params(("parallel",)), name="fnet_direct",
    )(zf, cs, ct)


GATHER_DMA_PRIORITY = 1
GATE_LANE0 = N_GROUPS


def _gates_kernel(lg_ref, id_ref, gv_ref):
    lg = lg_ref[...]
    lane = lax.broadcasted_iota(jnp.int32, lg.shape, 1)
    big = jnp.int32(LANES)
    neg = -jnp.inf
    is_g = lane < N_GROUPS
    gl = jnp.where(is_g, lg, neg)
    mg = jnp.max(gl, axis=-1, keepdims=True)
    p_group = 1.0 / jnp.sum(jnp.exp(gl - mg), axis=-1, keepdims=True)
    gsel = jnp.min(jnp.where(gl == mg, lane, big), axis=-1, keepdims=True)
    e_idx = lane - GATE_LANE0
    in_grp = jnp.logical_and(jnp.logical_and(e_idx >= 0, e_idx < N_EXPERTS), e_idx // EXPERTS_PER_GROUP == gsel)
    le = jnp.where(in_grp, lg, neg)
    m1 = jnp.max(le, axis=-1, keepdims=True)
    l1 = jnp.min(jnp.where(le == m1, lane, big), axis=-1, keepdims=True)
    le2 = jnp.where(lane == l1, neg, le)
    m2 = jnp.max(le2, axis=-1, keepdims=True)
    l2 = jnp.min(jnp.where(le2 == m2, lane, big), axis=-1, keepdims=True)
    e2 = jnp.exp(m2 - m1)
    inv = p_group / (1.0 + e2)
    id_ref[...] = jnp.where(lane == 0, l1 - GATE_LANE0, jnp.where(lane == 1, l2 - GATE_LANE0, 0))
    gv_ref[...] = jnp.where(lane == 0, inv, jnp.where(lane == 1, e2 * inv, 0.0))


def moe_gates(logits, tm=256):
    N = logits.shape[0]
    spec = pl.BlockSpec((tm, LANES), lambda i: (i, 0))
    return pl.pallas_call(
        _gates_kernel, grid=(N // tm,),
        in_specs=[spec], out_specs=[spec, spec],
        out_shape=[jax.ShapeDtypeStruct((N, LANES), jnp.int32), jax.ShapeDtypeStruct((N, LANES), F32)],
        compiler_params=_params(("parallel",)), name="moe_gates",
    )(logits)


def moe_dispatch(expert_ids, ts):
    N = expert_ids.shape[0]
    flat = expert_ids.reshape(-1)
    onehot = (flat[:, None] == jnp.arange(N_EXPERTS, dtype=jnp.int32)[None, :]).astype(jnp.int32)
    csum = jnp.cumsum(onehot, axis=0)
    rank = jnp.take_along_axis(csum, flat[:, None], axis=1)[:, 0] - 1
    counts = csum[-1]
    tiles_per = (counts + ts - 1) // ts
    tiles_end = jnp.cumsum(tiles_per)
    pad_start = (tiles_end - tiles_per) * ts
    slot = pad_start[flat] + rank
    n_tiles = -(-2 * N // ts) + N_EXPERTS
    slot_token = jnp.zeros((n_tiles * ts,), jnp.int32).at[slot].set(jnp.arange(2 * N, dtype=jnp.int32) // 2)
    n_used = tiles_end[-1]
    tile_idx = jnp.minimum(jnp.arange(n_tiles, dtype=jnp.int32), n_used - 1)
    tile_expert = jnp.sum((tile_idx[:, None] >= tiles_end[None, :]).astype(jnp.int32), axis=1)
    return slot_token, slot.reshape(N, 2), tile_expert.astype(jnp.int32), n_used.reshape(1).astype(jnp.int32)


def _moe_group_kernel(ts, n_chunks, te_ref, nu_ref, st_ref, u_hbm, w1_ref, w3_ref, w2_ref, o_ref, ubuf, ub16, sem):
    i = pl.program_id(0)
    j = pl.program_id(1)
    used = nu_ref[0]
    share = ts // n_chunks

    n_lane_chunks = ub16.shape[1] // LANES
    slab_rows = n_lane_chunks
    buf_rows = ts * SLAB_PITCH

    def row_copy(slot, b, r):
        src = pl.multiple_of(st_ref[slot] * SLAB_PITCH, 8)
        dst = pl.multiple_of(b * buf_rows + r * SLAB_PITCH, 8)
        return pltpu.make_async_copy(u_hbm.at[pl.ds(src, slab_rows), :], ubuf.at[pl.ds(dst, slab_rows), :], sem.at[b])

    @pl.when(jnp.logical_and(j == 0, i == 0))
    def _():
        def row(r, carry):
            row_copy(r, 0, r).start(priority=GATHER_DMA_PRIORITY)
            return carry
        lax.fori_loop(0, ts, row, 0)

    @pl.when(jnp.logical_and(j == 0, i <= used))
    def _():
        n = ts * slab_rows
        pltpu.make_async_copy(u_hbm.at[pl.ds(0, n), :], ubuf.at[pl.ds(0, n), :], sem.at[i % 2]).wait()

    @pl.when(jnp.logical_and(j == 0, i < used))
    def _():
        base = (i % 2) * buf_rows
        for c in range(n_lane_chunks):
            ub16[:, c * LANES:(c + 1) * LANES] = ubuf[pl.ds(base + c, ts, stride=SLAB_PITCH), :].astype(BF16)

    @pl.when(j == 0)
    def _():
        o_ref[...] = jnp.zeros(o_ref.shape, o_ref.dtype)

    @pl.when(i < used)
    def _():
        nb = (i + 1) % 2
        r0 = j * share
        for r in range(share):
            row_copy((i + 1) * ts + r0 + r, nb, r0 + r).start(priority=GATHER_DMA_PRIORITY)
        u = ub16[...]
        a = _dot(u, w1_ref[0, 0])
        g = _dot(u, w3_ref[0, 0])
        h = ((a * _sigmoid(a)) * g).astype(BF16)
        nw = o_ref.shape[1] // 4
        for n in range(0, o_ref.shape[1], nw):
            o_ref[:, n:n + nw] += _dot(h, w2_ref[0, 0, :, n:n + nw])


def moe_grouped(u, slot_token, tile_expert, n_used, w1, w3, w2, layer, ts, tc=256):
    _, E, D, De = w1.shape
    n_tiles = tile_expert.shape[0]
    grid_spec = pltpu.PrefetchScalarGridSpec(
        num_scalar_prefetch=3,
        grid=(n_tiles, De // tc),
        in_specs=[pl.BlockSpec(memory_space=pl.ANY),
                  pl.BlockSpec((1, 1, D, tc), lambda i, j, te, nu, st: (layer, te[i], 0, j)),
                  pl.BlockSpec((1, 1, D, tc), lambda i, j, te, nu, st: (layer, te[i], 0, j)),
                  pl.BlockSpec((1, 1, tc, D), lambda i, j, te, nu, st: (layer, te[i], j, 0))],
        out_specs=pl.BlockSpec((ts, D), lambda i, j, te, nu, st: (i, 0)),
        scratch_shapes=[pltpu.VMEM((2 * ts * SLAB_PITCH, LANES), F32), pltpu.VMEM((ts, D), BF16),
                        pltpu.SemaphoreType.DMA((2,))],
    )
    return pl.pallas_call(
        functools.partial(_moe_group_kernel, ts, De // tc), grid_spec=grid_spec,
        out_shape=jax.ShapeDtypeStruct((n_tiles * ts, D), F32),
        compiler_params=_params(("arbitrary", "arbitrary")), name="moe_grouped",
    )(tile_expert, n_used, slot_token, u, w1, w3, w2)


def _moe_combine_kernel(alpha, n_ctx_tiles, first_tile, sa_ref, sb_ref, o_hbm, x_ref, gv_ref, g_ref, gb_ref, out_ref,
                        abuf, bbuf, sem):
    i = pl.program_id(0)
    tm = x_ref.shape[0]

    def gather(tile, b):
        def row(r, carry):
            t = (first_tile + tile) * tm + r
            pltpu.make_async_copy(o_hbm.at[pl.ds(sa_ref[t], 1), :], abuf.at[b, pl.ds(r, 1), :], sem.at[0, b]).start()
            pltpu.make_async_copy(o_hbm.at[pl.ds(sb_ref[t], 1), :], bbuf.at[b, pl.ds(r, 1), :], sem.at[1, b]).start(
                priority=GATHER_DMA_PRIORITY)
            return carry
        lax.fori_loop(0, tm, row, 0)

    @pl.when(i == 0)
    def _():
        gather(0, 0)

    @pl.when(i + 1 < pl.num_programs(0))
    def _():
        gather(i + 1, (i + 1) % 2)

    b = i % 2
    pltpu.make_async_copy(o_hbm.at[pl.ds(0, tm), :], abuf.at[b], sem.at[0, b]).wait()
    pltpu.make_async_copy(o_hbm.at[pl.ds(0, tm), :], bbuf.at[b], sem.at[1, b]).wait()
    gv = gv_ref[...]
    f = gv[:, 0:1] * abuf[b] + gv[:, 1:2] * bbuf[b]
    g = jnp.where(first_tile + i < n_ctx_tiles, g_ref[1:2, :], g_ref[0:1, :])
    y = alpha * x_ref[...] + g * f
    out_ref[...] = _ln_rows(y, LN_EPS) * gb_ref[0:1, :] + gb_ref[1:2, :]


def moe_combine_ln(xx, o, slots, gate_vals, gates2, gain_bias, n_ctx, alpha, latent_only=False, tm=256):
    N, D = xx.shape
    first = n_ctx // tm if latent_only else 0
    grid_spec = pltpu.PrefetchScalarGridSpec(
        num_scalar_prefetch=2,
        grid=(N // tm - first,),
        in_specs=[pl.BlockSpec(memory_space=pl.ANY),
                  pl.BlockSpec((tm, D), lambda i, sa, sb: (first + i, 0)),
                  pl.BlockSpec((tm, LANES), lambda i, sa, sb: (first + i, 0)),
                  pl.BlockSpec((2, D), lambda i, sa, sb: (0, 0)), pl.BlockSpec((2, D), lambda i, sa, sb: (0, 0))],
        out_specs=pl.BlockSpec((tm, D), lambda i, sa, sb: (i, 0)),
        scratch_shapes=[pltpu.VMEM((2, tm, D), F32), pltpu.VMEM((2, tm, D), F32), pltpu.SemaphoreType.DMA((2, 2))],
    )
    return pl.pallas_call(
        functools.partial(_moe_combine_kernel, alpha, n_ctx // tm, first), grid_spec=grid_spec,
        out_shape=jax.ShapeDtypeStruct((N - first * tm, D), F32),
        compiler_params=_params(("arbitrary",)), name="moe_combine_ln",
    )(slots[:, 0], slots[:, 1], o, xx, gate_vals, gates2, gain_bias)


MOE_SLOT_TILE = 512

def _pick_tile(n, prefer):
    for t in prefer:
        if n % t == 0:
            return t
    raise ValueError(f"no tile for {n}")


def kernel(x, c, ctx, c_ctx, w_mod, b_mod, w_in, q_gain, k_gain, rwkv_mu, w0, w_up, a0, a_up, g_up, k_k, k_a, r_k, lnx_gain, lnx_bias, w_out, ln1_gain, ln1_bias, ln2_gain, ln2_bias, router_group_w, router_group_b, router_expert_w, router_expert_b, w1, w3, w2):
    B, T, D = x.shape
    C = ctx.shape[1]
    assert B == 1
    depth = w_mod.shape[0]
    alpha = (2 * depth) ** 0.25
    N = C + T
    aw = D // 2
    kvw = KV_HEADS * HEAD_DIM
    rw = D // 4
    fw = D // 4
    rcols = 3 * rw + 2 * W_RANK + 2 * A_RANK + G_RANK
    o_zr = aw + 2 * kvw
    o_zf = o_zr + rcols
    tm_big = _pick_tile(N, (768, 384, 256))

    xx = jnp.concatenate([ctx[0], x[0]], 0)
    mods = mod_vectors(c, c_ctx, w_mod, b_mod).reshape(depth, 8, 6, D)
    cosf, sinf = rope_tables(C, T)
    w_qkv = w_in[:, :, :o_zr].astype(BF16)
    w_rwkv = w_in[:, :, o_zr:o_zf].astype(BF16)
    w_four = w_in[:, :, o_zf:].astype(BF16)
    n_exp, _, d_exp = w1.shape[1:]
    side_cast = (w1.reshape(depth, n_exp * D, d_exp), w3.reshape(depth, n_exp * D, d_exp),
                 w2.reshape(depth, n_exp * d_exp, D), w_out)

    for l in range(depth):
        last = l == depth - 1
        mv = mods[l]
        vec = lambda i: jnp.stack([mv[0, i], mv[1, i]], 0)
        ss1 = jnp.stack([mv[0, 0], mv[0, 1], mv[1, 0], mv[1, 1]], 0)
        ss2 = jnp.stack([mv[0, 3], mv[0, 4], mv[1, 3], mv[1, 4]], 0)

        u = ln_modulate(xx, ss1, C)
        zq = matmul(u, w_qkv, l, F32, tm_big, _pick_tile(o_zr, (768, 512, 256, 128)))
        zr = matmul(u, w_rwkv, l, F32, tm_big, _pick_tile(rcols, (1152, 384, 128)))
        zf = matmul(u, w_four, l, F32, tm_big, _pick_tile(fw, (512, 256, 128)))
        qh, kh, vh = qkv_prep(zq, cosf, sinf, q_gain[l], k_gain[l], aw)
        attn, (w1_b, w3_b, w2_b, w_out_b) = attention(qh, kh, vh, C, cast=[(a, l) for a in side_cast])
        w1_b = w1_b.reshape(1, n_exp, D, d_exp)
        w3_b = w3_b.reshape(1, n_exp, D, d_exp)
        w2_b = w2_b.reshape(1, n_exp, d_exp, D)
        w_out_b = w_out_b[None]
        r, v, kap, g, lwf, bf, ktf, lwb, bb, ktb = rwkv_features(
            zr, C, rwkv_mu[l], w0[l], w_up[l], a0[l], a_up[l], g_up[l], k_k[l], k_a[l])
        y_f = rwkv_scan(False, C, r, kap, v, lwf, bf, ktf)
        y_b = rwkv_scan(True, C, r, kap, v, lwb, bb, ktb)
        rwkv = rwkv_output(y_f, y_b, r, v, ktf, ktb, g, r_k[l].reshape(-1), lnx_gain[l], lnx_bias[l])
        fl = fnet_latent(zf[C:])
        fc = fnet_direct(zf[:C]) if not last else jnp.zeros((C, fw), F32)
        fn = jnp.concatenate([fc, fl], 0).astype(BF16)
        m = matmul_parts([attn, rwkv, fn], w_out_b, 0, F32, tm_big, _pick_tile(D, (512, 256, 128)))
        xx = resid_ln(xx, m, vec(2), jnp.stack([ln1_gain[l], ln1_bias[l]], 0), C, alpha)

        wr = jnp.concatenate([router_group_w[l], router_expert_w[l],
                              jnp.zeros((D, LANES - N_GROUPS - N_EXPERTS), F32)], 1)
        br = jnp.concatenate([router_group_b[l], router_expert_b[l],
                              jnp.zeros((LANES - N_GROUPS - N_EXPERTS,), F32)], 0).reshape(1, LANES)
        u2, logits = ln_modulate(xx, ss2, C, router=(wr, br))
        expert_ids, gate_vals = moe_gates(logits)
        slot_token, slots, tile_expert, n_used = moe_dispatch(expert_ids[:, :2], MOE_SLOT_TILE)
        o = moe_grouped(u2, slot_token, tile_expert, n_used, w1_b, w3_b, w2_b, 0, MOE_SLOT_TILE)
        xx = moe_combine_ln(xx, o, slots, gate_vals, vec(5), jnp.stack([ln2_gain[l], ln2_bias[l]], 0), C, alpha,
                            latent_only=last)
    return xx[None]
```

```python
import functools
import math

import numpy as np
import jax
import jax.numpy as jnp
from jax import lax
from jax.experimental import pallas as pl
from jax.experimental.pallas import tpu as pltpu

F32 = jnp.float32
BF16 = jnp.bfloat16

GRID_W = 64
HEAD_DIM = 128
KV_HEADS = 4
RWKV_HEAD = 64
W_RANK = 64
A_RANK = 64
G_RANK = 128
FOURIER_GROUPS = 4
N_GROUPS = 4
EXPERTS_PER_GROUP = 4
N_EXPERTS = N_GROUPS * EXPERTS_PER_GROUP
ROPE_THETA = 10000.0
W_DECAY_SCALE = math.exp(-0.5)
GN_EPS = 64e-5
LN_EPS = 1e-6
LANES = 128
CHUNK = 64

V7X_VMEM_LIMIT_MB = 56


def _params(sem, vmem_mb=V7X_VMEM_LIMIT_MB):
    return pltpu.CompilerParams(dimension_semantics=sem, vmem_limit_bytes=vmem_mb * 1024 * 1024)


def _dot(a, b, prec=None):
    return jnp.dot(a, b, preferred_element_type=F32, precision=prec)


def _dot_nt(a, b, prec=None):
    return lax.dot_general(a, b, (((1,), (1,)), ((), ())), preferred_element_type=F32, precision=prec)


def _dot_tn(a, b, prec=None):
    return lax.dot_general(a, b, (((0,), (0,)), ((), ())), preferred_element_type=F32, precision=prec)


def _sigmoid(x):
    return 1.0 / (1.0 + jnp.exp(-x))


def _split2(x):
    hi = x.astype(BF16)
    return hi, (x - hi.astype(F32)).astype(BF16)


def _dot3(a, b):
    a_hi, a_lo = _split2(a)
    b_hi, b_lo = _split2(b)
    return _dot(a_hi, b_hi) + (_dot(a_hi, b_lo) + _dot(a_lo, b_hi))


def _bf16_bits(x):
    return lax.bitcast_convert_type(x.astype(BF16).astype(F32), jnp.uint32)


def _pack_bf16_halves(x):
    h = x.shape[1] // 2
    return (_bf16_bits(x[:, :h]) >> 16) | _bf16_bits(x[:, h:])


def _unpack_bf16_halves(p):
    lo = lax.bitcast_convert_type(p << 16, F32)
    hi = lax.bitcast_convert_type(p & jnp.uint32(0xFFFF0000), F32)
    return lo, hi


def _dot_exact_rhs(x, e):
    hi = x.astype(BF16)
    r1 = x - hi.astype(F32)
    mid = r1.astype(BF16)
    lo = (r1 - mid.astype(F32)).astype(BF16)
    return _dot(hi, e) + (_dot(mid, e) + _dot(lo, e))


def _mod_kernel(a_ref, w_ref, b_ref, o_ref):
    tn = o_ref.shape[2]
    rows = []
    for r in range(2):
        cols = []
        for j in range(tn // LANES):
            prod = w_ref[0, :, j * LANES:(j + 1) * LANES] * a_ref[r]
            cols.append(jnp.sum(prod, axis=0, keepdims=True))
        rows.append(jnp.concatenate(cols, axis=1) + b_ref[0])
    rows.append(jnp.zeros((6, tn), F32))
    o_ref[0] = jnp.concatenate(rows, axis=0)


def mod_vectors(c, c_ctx, w_mod, b_mod):
    L, D, D6 = w_mod.shape
    acts = jnp.stack([jax.nn.silu(c[0]), jax.nn.silu(c_ctx)], 0)
    a_b = jnp.broadcast_to(acts[:, :, None], (2, D, LANES))
    tn = 512
    return pl.pallas_call(
        _mod_kernel,
        grid=(L, D6 // tn),
        in_specs=[pl.BlockSpec((2, D, LANES), lambda l, j: (0, 0, 0)),
                  pl.BlockSpec((1, D, tn), lambda l, j: (l, 0, j)),
                  pl.BlockSpec((1, 1, tn), lambda l, j: (l, 0, j))],
        out_specs=pl.BlockSpec((1, 8, tn), lambda l, j: (l, 0, j)),
        out_shape=jax.ShapeDtypeStruct((L, 8, D6), F32),
        compiler_params=_params(("parallel", "parallel")),
        name="mod_vectors",
    )(a_b, w_mod, b_mod.reshape(L, 1, D6))


def _ln_rows(x, eps):
    mu = jnp.mean(x, axis=-1, keepdims=True)
    xc = x - mu
    var = jnp.mean(xc * xc, axis=-1, keepdims=True)
    return xc * lax.rsqrt(var + eps)


def _ln_mod_kernel(n_ctx_tiles, with_router, x_ref, ss_ref, *rest):
    if with_router:
        wr_ref, br_ref, u_ref, lg_ref = rest
    else:
        (u_ref,) = rest
    is_ctx = pl.program_id(0) < n_ctx_tiles
    sh = jnp.where(is_ctx, ss_ref[2:3, :], ss_ref[0:1, :])
    sc = jnp.where(is_ctx, ss_ref[3:4, :], ss_ref[1:2, :])
    u = _ln_rows(x_ref[...], LN_EPS) * (1.0 + sc) + sh
    if with_router:
        u_ref[...] = _pack_bf16_halves(u)
        lg_ref[...] = _dot3(u, wr_ref[...]) + br_ref[...]
    else:
        u_ref[...] = u.astype(u_ref.dtype)


def ln_modulate(xx, ss, n_ctx, router=None, tm=256):
    N, D = xx.shape
    assert N % tm == 0 and n_ctx % tm == 0
    in_specs = [pl.BlockSpec((tm, D), lambda i: (i, 0)), pl.BlockSpec((4, D), lambda i: (0, 0))]
    if router is None:
        out_specs = [pl.BlockSpec((tm, D), lambda i: (i, 0))]
        out_shape = [jax.ShapeDtypeStruct((N, D), BF16)]
    else:
        out_specs = [pl.BlockSpec((tm, D // 2), lambda i: (i, 0))]
        out_shape = [jax.ShapeDtypeStruct((N, D // 2), jnp.uint32)]
    args = [xx, ss]
    if router is not None:
        in_specs += [pl.BlockSpec((D, LANES), lambda i: (0, 0)), pl.BlockSpec((1, LANES), lambda i: (0, 0))]
        out_specs.append(pl.BlockSpec((tm, LANES), lambda i: (i, 0)))
        out_shape.append(jax.ShapeDtypeStruct((N, LANES), F32))
        args += list(router)
    res = pl.pallas_call(
        functools.partial(_ln_mod_kernel, n_ctx // tm, router is not None),
        grid=(N // tm,), in_specs=in_specs, out_specs=out_specs, out_shape=out_shape,
        compiler_params=_params(("parallel",)), name="ln_modulate",
    )(*args)
    return res if router is not None else res[0]


def _resid_ln_kernel(alpha, n_ctx_tiles, x_ref, m_ref, g_ref, gb_ref, o_ref):
    is_ctx = pl.program_id(0) < n_ctx_tiles
    g = jnp.where(is_ctx, g_ref[1:2, :], g_ref[0:1, :])
    y = alpha * x_ref[...] + g * m_ref[...].astype(F32)
    o_ref[...] = _ln_rows(y, LN_EPS) * gb_ref[0:1, :] + gb_ref[1:2, :]


def resid_ln(xx, m, gates2, gain_bias, n_ctx, alpha, tm=256):
    N, D = xx.shape
    return pl.pallas_call(
        functools.partial(_resid_ln_kernel, alpha, n_ctx // tm),
        grid=(N // tm,),
        in_specs=[pl.BlockSpec((tm, D), lambda i: (i, 0)), pl.BlockSpec((tm, D), lambda i: (i, 0)),
                  pl.BlockSpec((2, D), lambda i: (0, 0)), pl.BlockSpec((2, D), lambda i: (0, 0))],
        out_specs=pl.BlockSpec((tm, D), lambda i: (i, 0)),
        out_shape=jax.ShapeDtypeStruct((N, D), F32),
        compiler_params=_params(("parallel",)), name="resid_ln",
    )(xx, m, gates2, gain_bias)


def _mm_kernel(a_ref, b_ref, o_ref):
    o_ref[...] = _dot(a_ref[...], b_ref[0]).astype(o_ref.dtype)


def matmul(a, b, layer, out_dtype, tm, tn):
    M, K = a.shape
    _, _, Nn = b.shape
    assert M % tm == 0 and Nn % tn == 0
    return pl.pallas_call(
        _mm_kernel, grid=(M // tm, Nn // tn),
        in_specs=[pl.BlockSpec((tm, K), lambda i, j: (i, 0)), pl.BlockSpec((1, K, tn), lambda i, j: (layer, 0, j))],
        out_specs=pl.BlockSpec((tm, tn), lambda i, j: (i, j)),
        out_shape=jax.ShapeDtypeStruct((M, Nn), out_dtype),
        compiler_params=_params(("parallel", "arbitrary")), name="matmul",
    )(a, b)


def _mm_parts_kernel(n_parts, *refs):
    a_refs, b_refs, o_ref = refs[:n_parts], refs[n_parts:2 * n_parts], refs[2 * n_parts]
    acc = _dot(a_refs[0][...], b_refs[0][0])
    for a_ref, b_ref in zip(a_refs[1:], b_refs[1:]):
        acc = acc + _dot(a_ref[...], b_ref[0])
    o_ref[...] = acc.astype(o_ref.dtype)


def matmul_parts(parts, b, layer, out_dtype, tm, tn):
    M = parts[0].shape[0]
    Nn = b.shape[2]
    a_specs, b_specs, off = [], [], 0
    for p in parts:
        kp = p.shape[1]
        assert off % kp == 0
        a_specs.append(pl.BlockSpec((tm, kp), lambda i, j: (i, 0)))
        b_specs.append(pl.BlockSpec((1, kp, tn), lambda i, j, r=off // kp: (layer, r, j)))
        off += kp
    assert off == b.shape[1] and M % tm == 0 and Nn % tn == 0
    return pl.pallas_call(
        functools.partial(_mm_parts_kernel, len(parts)), grid=(M // tm, Nn // tn),
        in_specs=a_specs + b_specs,
        out_specs=pl.BlockSpec((tm, tn), lambda i, j: (i, j)),
        out_shape=jax.ShapeDtypeStruct((M, Nn), out_dtype),
        compiler_params=_params(("parallel", "arbitrary")), name="matmul_parts",
    )(*parts, *([b] * len(parts)))


def _qkv_prep_kernel(n_q_heads, scale, z_ref, cos_ref, sin_ref, qg_ref, kg_ref, q_ref, k_ref, v_ref):
    cosf = cos_ref[...]
    sinf = sin_ref[...]

    def norm_rope(t, gain):
        t = t * lax.rsqrt(jnp.mean(t * t, axis=-1, keepdims=True) + LN_EPS) * gain
        return t * cosf + pltpu.roll(t, HEAD_DIM // 2, 1) * sinf

    for h in range(n_q_heads):
        sl = slice(h * HEAD_DIM, (h + 1) * HEAD_DIM)
        q_ref[:, sl] = (norm_rope(z_ref[:, sl], qg_ref[...]) * scale).astype(q_ref.dtype)
    qw = n_q_heads * HEAD_DIM
    for h in range(KV_HEADS):
        sl = slice(h * HEAD_DIM, (h + 1) * HEAD_DIM)
        zs = slice(qw + h * HEAD_DIM, qw + (h + 1) * HEAD_DIM)
        k_ref[:, sl] = norm_rope(z_ref[:, zs], kg_ref[...]).astype(k_ref.dtype)
    kvw = KV_HEADS * HEAD_DIM
    ones = jnp.ones((z_ref.shape[0], HEAD_DIM), v_ref.dtype)
    for h in range(KV_HEADS):
        zs = slice(qw + kvw + h * HEAD_DIM, qw + kvw + (h + 1) * HEAD_DIM)
        v_ref[:, 2 * h * HEAD_DIM:(2 * h + 1) * HEAD_DIM] = z_ref[:, zs].astype(v_ref.dtype)
        v_ref[:, (2 * h + 1) * HEAD_DIM:(2 * h + 2) * HEAD_DIM] = ones


def qkv_prep(z, cosf, sinf, q_gain, k_gain, attn_width, tm=256):
    N = z.shape[0]
    n_q = attn_width // HEAD_DIM
    kvw = KV_HEADS * HEAD_DIM
    zw = attn_width + 2 * kvw
    return pl.pallas_call(
        functools.partial(_qkv_prep_kernel, n_q, HEAD_DIM ** -0.5 * math.log2(math.e)),
        grid=(N // tm,),
        in_specs=[pl.BlockSpec((tm, zw), lambda i: (i, 0)),
                  pl.BlockSpec((tm, HEAD_DIM), lambda i: (i, 0)), pl.BlockSpec((tm, HEAD_DIM), lambda i: (i, 0)),
                  pl.BlockSpec((1, HEAD_DIM), lambda i: (0, 0)), pl.BlockSpec((1, HEAD_DIM), lambda i: (0, 0))],
        out_specs=[pl.BlockSpec((tm, attn_width), lambda i: (i, 0)),
                   pl.BlockSpec((tm, kvw), lambda i: (i, 0)), pl.BlockSpec((tm, 2 * kvw), lambda i: (i, 0))],
        out_shape=[jax.ShapeDtypeStruct((N, attn_width), BF16),
                   jax.ShapeDtypeStruct((N, kvw), BF16), jax.ShapeDtypeStruct((N, 2 * kvw), BF16)],
        compiler_params=_params(("parallel",)), name="qkv_prep",
    )(z, cosf, sinf, q_gain.reshape(1, HEAD_DIM), k_gain.reshape(1, HEAD_DIM))


def _attn_kernel(n_ctx, n_ctx_tiles, group, n_cast, q_ref, k_ref, v_ref, *rest):
    cast_src, o_ref, cast_dst = rest[:n_cast], rest[n_cast], rest[n_cast + 1:]
    tq = q_ref.shape[0]
    hh = group // 2
    halves = [jnp.concatenate([q_ref[:, h * HEAD_DIM:(h + 1) * HEAD_DIM] for h in range(i * hh, (i + 1) * hh)], axis=0)
              for i in range(2)]

    def attend(k, v):
        s = [_dot_nt(qh, k) for qh in halves]
        p = [jnp.exp2(x - jnp.max(x, axis=-1, keepdims=True)).astype(v.dtype) for x in s]
        acc = [_dot(x, v) for x in p]
        for h in range(group):
            a = acc[h // hh][(h % hh) * tq:(h % hh + 1) * tq]
            o_ref[:, h * HEAD_DIM:(h + 1) * HEAD_DIM] = (a[:, :HEAD_DIM] / a[:, HEAD_DIM:]).astype(o_ref.dtype)

    is_ctx = pl.program_id(1) < n_ctx_tiles

    @pl.when(is_ctx)
    def _():
        attend(k_ref[0:n_ctx, :], v_ref[0:n_ctx, :])

    @pl.when(jnp.logical_not(is_ctx))
    def _():
        attend(k_ref[...], v_ref[...])

    for src, dst in zip(cast_src, cast_dst):
        dst[...] = src[...].astype(dst.dtype)


def attention(q, k, v, n_ctx, cast=(), tq=128):
    N, aw = q.shape
    group = aw // HEAD_DIM // KV_HEADS
    gw = group * HEAD_DIM
    assert N % tq == 0 and n_ctx % tq == 0
    n_q = N // tq
    n_steps = KV_HEADS * n_q
    cast_in, cast_out, cast_shape, cast_args = [], [], [], []
    for a, layer in cast:
        _, rows, cols = a.shape
        rb = -(-(-(-rows // n_steps)) // 16) * 16
        last = -(-rows // rb) - 1
        if rows % rb == 0:
            a2, first = a.reshape(-1, cols), layer * (rows // rb)
        else:
            a2, first = a[layer], 0
        cast_args.append(a2)
        cast_in.append(pl.BlockSpec((rb, cols), lambda g, i, last=last, first=first:
                                    (first + jnp.minimum(g * n_q + i, last), 0)))
        cast_out.append(pl.BlockSpec((rb, cols), lambda g, i, last=last: (jnp.minimum(g * n_q + i, last), 0)))
        cast_shape.append(jax.ShapeDtypeStruct((rows, cols), BF16))
    res = pl.pallas_call(
        functools.partial(_attn_kernel, n_ctx, n_ctx // tq, group, len(cast)),
        grid=(KV_HEADS, n_q),
        in_specs=[pl.BlockSpec((tq, gw), lambda g, i: (i, g)),
                  pl.BlockSpec((N, HEAD_DIM), lambda g, i: (0, g)),
                  pl.BlockSpec((N, 2 * HEAD_DIM), lambda g, i: (0, g))] + cast_in,
        out_specs=[pl.BlockSpec((tq, gw), lambda g, i: (i, g))] + cast_out,
        out_shape=[jax.ShapeDtypeStruct((N, aw), BF16)] + cast_shape,
        compiler_params=_params(("arbitrary", "arbitrary")), name="attention",
    )(q, k, v, *cast_args)
    return res[0], tuple(res[1:])


def rope_tables(n_ctx, n_lat):
    n_rows = n_lat // GRID_W
    row = jnp.repeat(jnp.arange(n_rows), GRID_W).astype(F32)
    col = jnp.tile(jnp.arange(GRID_W), n_rows).astype(F32)
    axis_dim = HEAD_DIM // 2
    inv_freq = ROPE_THETA ** (-jnp.arange(0, axis_dim, 2, dtype=F32) / axis_dim)
    ang = jnp.concatenate([row[:, None] * inv_freq, col[:, None] * inv_freq], -1)
    cos, sin = jnp.cos(ang), jnp.sin(ang)
    cosf = jnp.concatenate([cos, cos], -1)
    sinf = jnp.concatenate([-sin, sin], -1)
    cosf = jnp.concatenate([jnp.ones((n_ctx, HEAD_DIM), F32), cosf], 0)
    sinf = jnp.concatenate([jnp.zeros((n_ctx, HEAD_DIM), F32), sinf], 0)
    return cosf, sinf


def _seg_sum64(x, e128):
    return _dot_exact_rhs(x, e128)


def _rwkv_feat_kernel(n_ctx_tiles, n_tiles, w_cols,
                      z_ref, zp_ref, zn_ref, mu_ref, kk_ref, ka_ref, w0_ref, a0_ref,
                      wup_ref, aup_ref, gup_ref, e_ref,
                      r_ref, v_ref, kap_ref, g_ref, lwf_ref, bf_ref, ktf_ref, lwb_ref, bb_ref, ktb_ref,
                      scr_ref):
    i = pl.program_id(0)
    tm = z_ref.shape[0]
    first = jnp.logical_or(i == 0, i == n_ctx_tiles)
    last = jnp.logical_or(i == n_ctx_tiles - 1, i == n_tiles - 1)
    scr_ref[8:8 + tm, :] = z_ref[...]
    scr_ref[0:8, :] = jnp.where(first, 0.0, zp_ref[...])
    scr_ref[8 + tm:16 + tm, :] = jnp.where(last, 0.0, zn_ref[...])
    z = z_ref[...]
    prev = scr_ref[7:7 + tm, :]
    nxt = scr_ref[9:9 + tm, :]
    zs = z + mu_ref[0:1, :] * (prev - z) + mu_ref[1:2, :] * (nxt - z)

    r = zs[:, 0:w_cols]
    k = zs[:, w_cols:2 * w_cols]
    v = zs[:, 2 * w_cols:3 * w_cols]
    lora = zs[:, 3 * w_cols:]
    wd = jnp.tanh(lora[:, 0:LANES])
    ad = lora[:, LANES:2 * LANES]
    gd = _sigmoid(lora[:, 2 * LANES:3 * LANES])
    r_ref[...] = r.astype(r_ref.dtype)
    v_ref[...] = v.astype(v_ref.dtype)
    g_ref[...] = _dot(gd.astype(BF16), gup_ref[...]).astype(g_ref.dtype)
    kk = k * kk_ref[...]
    e128 = e_ref[...]
    kap = jnp.concatenate(
        [kk[:, s:s + LANES] * lax.rsqrt(jnp.maximum(_seg_sum64(kk[:, s:s + LANES] * kk[:, s:s + LANES], e128), 1e-24))
         for s in range(0, w_cols, LANES)], axis=1)
    kap_ref[...] = kap.astype(kap_ref.dtype)
    outs = ((lwf_ref, bf_ref, ktf_ref), (lwb_ref, bb_ref, ktb_ref))
    for d in range(2):
        lw_ref, b_ref, kt_ref = outs[d]
        lw_ref[...] = -W_DECAY_SCALE * _sigmoid(w0_ref[d:d + 1, :] + _dot(wd.astype(BF16), wup_ref[d]))
        a = _sigmoid(a0_ref[d:d + 1, :] + _dot(ad.astype(BF16), aup_ref[d]))
        kt_ref[...] = (k * (1.0 + (a - 1.0) * ka_ref[...])).astype(kt_ref.dtype)
        b_ref[...] = (a * kap).astype(b_ref.dtype)


def _seg_ones(width=LANES, seg=RWKV_HEAD):
    i = np.arange(width)
    return jnp.asarray((i[:, None] // seg == i[None, :] // seg).astype(np.float32)).astype(BF16)


def rwkv_features(zr, n_ctx, mu, w0, w_up, a0, a_up, g_up, k_k, k_a, tm=256):
    N, zw = zr.shape
    wc = k_k.shape[0]
    assert zw == 3 * wc + 3 * LANES and W_RANK + W_RANK == LANES and A_RANK + A_RANK == LANES and G_RANK == LANES
    zeros = jnp.zeros((W_RANK, wc), F32)
    wup = jnp.stack([jnp.concatenate([w_up[0], zeros], 0), jnp.concatenate([zeros, w_up[1]], 0)], 0)
    aup = jnp.stack([jnp.concatenate([a_up[0], zeros], 0), jnp.concatenate([zeros, a_up[1]], 0)], 0)
    n_tiles = N // tm
    t8 = tm // 8
    row = lambda i: (i, 0)
    full = lambda i: (0, 0)
    sds = lambda dt: jax.ShapeDtypeStruct((N, wc), dt)
    return pl.pallas_call(
        functools.partial(_rwkv_feat_kernel, n_ctx // tm, n_tiles, wc),
        grid=(n_tiles,),
        in_specs=[pl.BlockSpec((tm, zw), row),
                  pl.BlockSpec((8, zw), lambda i: (jnp.maximum(i * t8 - 1, 0), 0)),
                  pl.BlockSpec((8, zw), lambda i: (jnp.minimum((i + 1) * t8, N // 8 - 1), 0)),
                  pl.BlockSpec((2, zw), full), pl.BlockSpec((1, wc), full), pl.BlockSpec((1, wc), full),
                  pl.BlockSpec((2, wc), full), pl.BlockSpec((2, wc), full),
                  pl.BlockSpec((2, LANES, wc), lambda i: (0, 0, 0)), pl.BlockSpec((2, LANES, wc), lambda i: (0, 0, 0)),
                  pl.BlockSpec((LANES, wc), full), pl.BlockSpec((LANES, LANES), full)],
        out_specs=[pl.BlockSpec((tm, wc), row)] * 10,
        out_shape=[sds(BF16)] * 4 + [sds(F32), sds(BF16), sds(BF16)] * 2,
        scratch_shapes=[pltpu.VMEM((tm + 16, zw), F32)],
        compiler_params=_params(("parallel",)), name="rwkv_features",
    )(zr, zr, zr, mu, k_k.reshape(1, wc), k_a.reshape(1, wc), w0, a0,
      wup.astype(BF16), aup.astype(BF16), g_up.astype(BF16), _seg_ones())


def _rwkv_scan_kernel(rev, pairs, n_sub, r_ref, kap_ref, v_ref, lw_ref, b_ref, kt_ref, y_ref, s_ref):
    L = CHUNK
    L2 = 2 * L

    @pl.when(pl.program_id(1) == 0)
    def _():
        s_ref[...] = jnp.zeros(s_ref.shape, F32)

    t_i = lax.broadcasted_iota(jnp.int32, (L, L), 0)
    s_i = lax.broadcasted_iota(jnp.int32, (L, L), 1)
    m_incl64 = ((s_i >= t_i) if rev else (s_i <= t_i)).astype(F32)
    ri = lax.broadcasted_iota(jnp.int32, (L2, L2), 0)
    qi = lax.broadcasted_iota(jnp.int32, (L2, L2), 1)
    rt, qt = ri % L, qi % L
    same = (ri // L) == (qi // L)
    incl = jnp.logical_and(same, (qt >= rt) if rev else (qt <= rt))
    strict = jnp.logical_and(same, (qt > rt) if rev else (qt < rt))
    eye = (ri == qi).astype(F32)
    lane = lax.broadcasted_iota(jnp.int32, (L, LANES), 1)
    hm = [(lane < RWKV_HEAD).astype(F32), (lane >= RWKV_HEAD).astype(F32)]
    bd = ((lax.broadcasted_iota(jnp.int32, (LANES, LANES), 0) // RWKV_HEAD)
          == (lax.broadcasted_iota(jnp.int32, (LANES, LANES), 1) // RWKV_HEAD)).astype(F32)
    lvl_masks = []
    bsz = 1
    while bsz < L:
        grp = (ri // (2 * bsz)) == (qi // (2 * bsz))
        r_odd = (ri // bsz) % 2 == 1
        q_odd = (qi // bsz) % 2 == 1
        off = jnp.logical_and(jnp.logical_not(r_odd), q_odd) if rev else jnp.logical_and(r_odd, jnp.logical_not(q_odd))
        lvl_masks.append(jnp.logical_and(grp, off))
        bsz *= 2

    def stack2(x):
        return jnp.concatenate([x * hm[0], x * hm[1]], axis=0)

    def bmm(a, b):
        return _dot(a.astype(BF16), b.astype(BF16))

    m_incl_bf = m_incl64.astype(BF16)

    units = [(ci, p) for ci in range(n_sub) for p in range(pairs)]

    def blk(ref, unit):
        ci, p = unit
        cc = (n_sub - 1 - ci) if rev else ci
        return ref[cc * L:(cc + 1) * L, p * LANES:(p + 1) * LANES]

    def each(fn, *dicts):
        return {un: fn(*(d[un] for d in dicts)) for un in units}

    def split3(x):
        hi = x.astype(BF16)
        r1 = x - hi.astype(F32)
        mid = r1.astype(BF16)
        lo = (r1 - mid.astype(F32)).astype(BF16)
        return jnp.concatenate([hi, mid, lo], axis=1)

    lw = {un: blk(lw_ref, un) for un in units}
    c3 = each(lambda x: _dot(m_incl_bf, split3(x)), lw)
    cum = each(lambda c: c[:, 0:LANES] + c[:, LANES:2 * LANES] + c[:, 2 * LANES:], c3)
    e_incl = each(jnp.exp, cum)
    e_inv = each(lambda c: jnp.exp(-c), cum)
    p_tot = each(lambda e: e[0:1, :] if rev else e[L - 1:L, :], e_incl)
    xk = {un: stack2(blk(kap_ref, un) * jnp.exp(cum[un] - lw[un])).astype(BF16) for un in units}
    xr = {un: stack2(blk(r_ref, un) * e_incl[un]).astype(BF16) for un in units}
    yb = {un: stack2(blk(b_ref, un) * e_inv[un]).astype(BF16) for un in units}
    yk = {un: stack2(blk(kt_ref, un) * e_inv[un]).astype(BF16) for un in units}
    vs = {un: stack2(blk(v_ref, un)).astype(BF16) for un in units}
    amat = each(lambda a, b, c, d: _dot_nt(jnp.concatenate([a, b], axis=0), jnp.concatenate([c, d], axis=0)),
                xk, xr, yb, yk)
    a_ub = each(lambda a: jnp.where(strict, a[0:L2, 0:L2], 0.0), amat)
    a_uk = each(lambda a: jnp.where(strict, a[0:L2, L2:], 0.0).astype(BF16), amat)
    a_rb = each(lambda a: jnp.where(incl, a[L2:, 0:L2], 0.0).astype(BF16), amat)
    a_rk = each(lambda a: jnp.where(incl, a[L2:, L2:], 0.0).astype(BF16), amat)
    tinv = each(lambda a: eye - jnp.where(lvl_masks[0], a, 0.0), a_ub)
    for lm in lvl_masks[1:]:
        tb = each(lambda t: t.astype(BF16), tinv)
        x1 = each(lambda a, t: _dot(jnp.where(lm, a, 0.0).astype(BF16), t).astype(BF16), a_ub, tb)
        tinv = each(lambda t, tbf, x: t - _dot(tbf, x), tinv, tb, x1)
    tb = each(lambda t: t.astype(BF16), tinv)
    w1 = each(lambda t, x: _dot(t, x).astype(BF16), tb, xk)
    avs = each(lambda a, v: _dot(a, v).astype(BF16), a_uk, vs)
    y_ind = each(_dot, a_rk, vs)
    k2 = each(_dot_tn, vs, yk)
    w2 = each(_dot, tb, avs)

    state = [s_ref[p] for p in range(pairs)]
    for ci in range(n_sub):
        wx = [_dot_nt(jnp.concatenate([w1[(ci, p)], xr[(ci, p)]], axis=0), state[p].astype(BF16))
              for p in range(pairs)]
        ub = [(-(wx[p][0:L2] + w2[(ci, p)])).astype(BF16) for p in range(pairs)]
        upd = [_dot_tn(ub[p], yb[(ci, p)]) + k2[(ci, p)] for p in range(pairs)]
        state = [(state[p] + upd[p] * bd) * p_tot[(ci, p)] for p in range(pairs)]
        cc = (n_sub - 1 - ci) if rev else ci
        for p in range(pairs):
            ystk = wx[p][L2:] + _dot(a_rb[(ci, p)], ub[p]) + y_ind[(ci, p)]
            y_ref[cc * L:(cc + 1) * L, p * LANES:(p + 1) * LANES] = ystk[0:L] + ystk[L:]
    for p in range(pairs):
        s_ref[p] = state[p]


def rwkv_scan(rev, n_ctx, r, kap, v, lw, b, kt, pairs=4, n_sub=4):
    N, wc = r.shape
    rb = n_sub * CHUNK
    bw = pairs * LANES
    assert N % rb == 0 and n_ctx % rb == 0 and wc % bw == 0
    nb, nbc = N // rb, n_ctx // rb
    if rev:
        rmap = lambda h, i: (jnp.where(i < nbc, nbc - 1 - i, nb - 1 - (i - nbc)), h)
    else:
        rmap = lambda h, i: (i, h)
    spec = pl.BlockSpec((rb, bw), rmap)
    return pl.pallas_call(
        functools.partial(_rwkv_scan_kernel, rev, pairs, n_sub),
        grid=(wc // bw, nb),
        in_specs=[spec] * 6, out_specs=spec,
        out_shape=jax.ShapeDtypeStruct((N, wc), F32),
        scratch_shapes=[pltpu.VMEM((pairs, LANES, LANES), F32)],
        compiler_params=_params(("parallel", "arbitrary")), name="rwkv_scan_rev" if rev else "rwkv_scan_fwd",
    )(r, kap, v, lw, b, kt)


def _rwkv_out_kernel(yf_ref, yb_ref, r_ref, v_ref, ktf_ref, ktb_ref, g_ref, rk_ref, gn_ref, e_ref, o_ref):
    e128 = e_ref[...]
    inv = 1.0 / RWKV_HEAD
    for s in range(0, o_ref.shape[1], LANES):
        sl = slice(s, s + LANES)
        y = yf_ref[:, sl] + yb_ref[:, sl]
        mu = _seg_sum64(y, e128) * inv
        yc = y - mu
        var = _seg_sum64(yc * yc, e128) * inv
        yn = yc * lax.rsqrt(var + GN_EPS) * gn_ref[0:1, sl] + gn_ref[1:2, sl]
        rk = r_ref[:, sl] * rk_ref[0:1, sl]
        bonus = _seg_sum64(rk * (ktf_ref[:, sl].astype(F32) + ktb_ref[:, sl]), e128) * v_ref[:, sl]
        o_ref[:, sl] = ((yn + bonus) * g_ref[:, sl]).astype(o_ref.dtype)


def rwkv_output(y_f, y_b, r, v, kt_f, kt_b, g, r_k, lnx_gain, lnx_bias, tm=256):
    N, wc = r.shape
    row = lambda i: (i, 0)
    full = lambda i: (0, 0)
    gn = jnp.stack([lnx_gain, lnx_bias], 0)
    return pl.pallas_call(
        _rwkv_out_kernel, grid=(N // tm,),
        in_specs=[pl.BlockSpec((tm, wc), row)] * 7 + [pl.BlockSpec((1, wc), full), pl.BlockSpec((2, wc), full),
                                                      pl.BlockSpec((LANES, LANES), full)],
        out_specs=pl.BlockSpec((tm, wc), row),
        out_shape=jax.ShapeDtypeStruct((N, wc), BF16),
        compiler_params=_params(("parallel",)), name="rwkv_output",
    )(y_f, y_b, r, v, kt_f, kt_b, g, r_k.reshape(1, wc), gn, _seg_ones())


def _angles(i, j, period):
    return (2.0 * math.pi / period) * ((i * j) % period).astype(np.float64)


def _chan_dft(gdim, scale):
    i = np.arange(gdim)
    ang = _angles(i[:, None], i[None, :], gdim)
    return jnp.asarray(np.concatenate([np.cos(ang), -np.sin(ang)], 1) * scale, F32)


def _fnet1_kernel(groups, x_ref, cs_ref, m_ref, y_ref):
    r1 = x_ref.shape[0]
    gdim = x_ref.shape[1] // groups
    for g in range(groups):
        sl = slice(g * gdim, (g + 1) * gdim)
        ab = _dot3(x_ref[:, sl], cs_ref[...])
        st = jnp.concatenate([ab[:, :gdim], ab[:, gdim:]], axis=0)
        y = _dot3(m_ref[0], st)
        y_ref[0, 0, :, sl] = y[:r1]
        y_ref[1, 0, :, sl] = y[r1:]


def _fnet3_kernel(cs_ref, y_ref, o_ref):
    o_ref[...] = _dot3(cs_ref[...], y_ref[...])


def fnet_latent(zf, groups=FOURIER_GROUPS):
    T, fw = zf.shape
    gdim = fw // groups
    r2 = GRID_W
    r1 = T // r2
    cs = _chan_dft(gdim, 1.0 / math.sqrt(T * gdim))
    t1p = np.arange(r1)[None, :, None]
    t1 = np.arange(r1)[None, None, :]
    t2 = np.arange(r2)[:, None, None]
    th = _angles(t1p, r2 * t1 + t2, T)
    gr, gi = np.cos(th), -np.sin(th)
    mt = jnp.asarray(np.concatenate([np.concatenate([gr, -gi], 2), np.concatenate([gi, gr], 2)], 1), F32)
    y = pl.pallas_call(
        functools.partial(_fnet1_kernel, groups),
        grid=(r2,),
        in_specs=[pl.BlockSpec((r1, fw), lambda j: (0, j)),
                  pl.BlockSpec((gdim, 2 * gdim), lambda j: (0, 0)),
                  pl.BlockSpec((1, 2 * r1, 2 * r1), lambda j: (j, 0, 0))],
        out_specs=pl.BlockSpec((2, 1, r1, fw), lambda j: (0, j, 0, 0)),
        out_shape=jax.ShapeDtypeStruct((2, r2, r1, fw), F32),
        compiler_params=_params(("parallel",)), name="fnet_rows",
    )(zf.reshape(r1, r2 * fw), cs, mt)
    i2 = np.arange(r2)
    ph = _angles(i2[:, None], i2[None, :], r2)
    cs2 = jnp.asarray(np.concatenate([np.cos(ph), np.sin(ph)], 1), F32)
    tc = 4 * fw if r1 % 4 == 0 else fw
    o = pl.pallas_call(
        _fnet3_kernel, grid=(r1 * fw // tc,),
        in_specs=[pl.BlockSpec((r2, 2 * r2), lambda j: (0, 0)), pl.BlockSpec((2 * r2, tc), lambda j: (0, j))],
        out_specs=pl.BlockSpec((r2, tc), lambda j: (0, j)),
        out_shape=jax.ShapeDtypeStruct((r2, r1 * fw), F32),
        compiler_params=_params(("parallel",)), name="fnet_cols",
    )(cs2, y.reshape(2 * r2, r1 * fw))
    return o.reshape(T, fw)


def _fnet_direct_kernel(x_ref, cs_ref, ct_ref, o_ref):
    gdim = x_ref.shape[1]
    ab = _dot3(x_ref[...], cs_ref[...])
    st = jnp.concatenate([ab[:, :gdim], ab[:, gdim:]], axis=0)
    o_ref[...] = _dot3(ct_ref[...], st)


def fnet_direct(zf, groups=FOURIER_GROUPS):
    C, fw = zf.shape
    gdim = fw // groups
    cs = _chan_dft(gdim, 1.0 / math.sqrt(C * gdim))
    i = np.arange(C)
    ang = _angles(i[:, None], i[None, :], C)
    ct = jnp.asarray(np.concatenate([np.cos(ang), np.sin(ang)], 1), F32)
    return pl.pallas_call(
        _fnet_direct_kernel, grid=(groups,),
        in_specs=[pl.BlockSpec((C, gdim), lambda g: (0, g)), pl.BlockSpec((gdim, 2 * gdim), lambda g: (0, 0)),
                  pl.BlockSpec((C, 2 * C), lambda g: (0, 0))],
        out_specs=pl.BlockSpec((C, gdim), lambda g: (0, g)),
        out_shape=jax.ShapeDtypeStruct((C, fw), F32),
        compiler_params=_params(("parallel",)), name="fnet_direct",
    )(zf, cs, ct)


GATHER_DMA_PRIORITY = 1
GATE_LANE0 = N_GROUPS


def _gates_kernel(lg_ref, id_ref, gv_ref):
    lg = lg_ref[...]
    lane = lax.broadcasted_iota(jnp.int32, lg.shape, 1)
    big = jnp.int32(LANES)
    neg = -jnp.inf
    is_g = lane < N_GROUPS
    gl = jnp.where(is_g, lg, neg)
    mg = jnp.max(gl, axis=-1, keepdims=True)
    p_group = 1.0 / jnp.sum(jnp.exp(gl - mg), axis=-1, keepdims=True)
    gsel = jnp.min(jnp.where(gl == mg, lane, big), axis=-1, keepdims=True)
    e_idx = lane - GATE_LANE0
    in_grp = jnp.logical_and(jnp.logical_and(e_idx >= 0, e_idx < N_EXPERTS), e_idx // EXPERTS_PER_GROUP == gsel)
    le = jnp.where(in_grp, lg, neg)
    m1 = jnp.max(le, axis=-1, keepdims=True)
    l1 = jnp.min(jnp.where(le == m1, lane, big), axis=-1, keepdims=True)
    le2 = jnp.where(lane == l1, neg, le)
    m2 = jnp.max(le2, axis=-1, keepdims=True)
    l2 = jnp.min(jnp.where(le2 == m2, lane, big), axis=-1, keepdims=True)
    e2 = jnp.exp(m2 - m1)
    inv = p_group / (1.0 + e2)
    id_ref[...] = jnp.where(lane == 0, l1 - GATE_LANE0, jnp.where(lane == 1, l2 - GATE_LANE0, 0))
    gv_ref[...] = jnp.where(lane == 0, inv, jnp.where(lane == 1, e2 * inv, 0.0))


def moe_gates(logits, tm=256):
    N = logits.shape[0]
    spec = pl.BlockSpec((tm, LANES), lambda i: (i, 0))
    return pl.pallas_call(
        _gates_kernel, grid=(N // tm,),
        in_specs=[spec], out_specs=[spec, spec],
        out_shape=[jax.ShapeDtypeStruct((N, LANES), jnp.int32), jax.ShapeDtypeStruct((N, LANES), F32)],
        compiler_params=_params(("parallel",)), name="moe_gates",
    )(logits)


def moe_dispatch(expert_ids, ts):
    N = expert_ids.shape[0]
    flat = expert_ids.reshape(-1)
    onehot = (flat[:, None] == jnp.arange(N_EXPERTS, dtype=jnp.int32)[None, :]).astype(jnp.int32)
    csum = jnp.cumsum(onehot, axis=0)
    rank = jnp.take_along_axis(csum, flat[:, None], axis=1)[:, 0] - 1
    counts = csum[-1]
    tiles_per = (counts + ts - 1) // ts
    tiles_end = jnp.cumsum(tiles_per)
    pad_start = (tiles_end - tiles_per) * ts
    slot = pad_start[flat] + rank
    n_tiles = -(-2 * N // ts) + N_EXPERTS
    slot_token = jnp.zeros((n_tiles * ts,), jnp.int32).at[slot].set(jnp.arange(2 * N, dtype=jnp.int32) // 2)
    n_used = tiles_end[-1]
    tile_idx = jnp.minimum(jnp.arange(n_tiles, dtype=jnp.int32), n_used - 1)
    tile_expert = jnp.sum((tile_idx[:, None] >= tiles_end[None, :]).astype(jnp.int32), axis=1)
    return slot_token, slot.reshape(N, 2), tile_expert.astype(jnp.int32), n_used.reshape(1).astype(jnp.int32)


def _moe_group_kernel(ts, n_chunks, te_ref, nu_ref, st_ref, u_hbm, w1_ref, w3_ref, w2_ref, o_ref, ubuf, ub16, sem):
    i = pl.program_id(0)
    j = pl.program_id(1)
    used = nu_ref[0]
    share = ts // n_chunks

    def row_copy(slot, b, r):
        return pltpu.make_async_copy(u_hbm.at[pl.ds(st_ref[slot], 1), :], ubuf.at[b, pl.ds(r, 1), :], sem.at[b])

    @pl.when(jnp.logical_and(j == 0, i == 0))
    def _():
        def row(r, carry):
            row_copy(r, 0, r).start(priority=GATHER_DMA_PRIORITY)
            return carry
        lax.fori_loop(0, ts, row, 0)

    @pl.when(jnp.logical_and(j == 0, i <= used))
    def _():
        pltpu.make_async_copy(u_hbm.at[pl.ds(0, ts), :], ubuf.at[i % 2], sem.at[i % 2]).wait()

    @pl.when(jnp.logical_and(j == 0, i < used))
    def _():
        lo, hi = _unpack_bf16_halves(ubuf[i % 2])
        half = lo.shape[1]
        ub16[:, :half] = lo.astype(BF16)
        ub16[:, half:] = hi.astype(BF16)

    @pl.when(j == 0)
    def _():
        o_ref[...] = jnp.zeros(o_ref.shape, o_ref.dtype)

    @pl.when(i < used)
    def _():
        nb = (i + 1) % 2
        r0 = j * share
        for r in range(share):
            row_copy((i + 1) * ts + r0 + r, nb, r0 + r).start(priority=GATHER_DMA_PRIORITY)
        u = ub16[...]
        a = _dot(u, w1_ref[0, 0])
        g = _dot(u, w3_ref[0, 0])
        h = ((a * _sigmoid(a)) * g).astype(BF16)
        nw = o_ref.shape[1] // 4
        for n in range(0, o_ref.shape[1], nw):
            o_ref[:, n:n + nw] += _dot(h, w2_ref[0, 0, :, n:n + nw])


def moe_grouped(u, slot_token, tile_expert, n_used, w1, w3, w2, layer, ts, tc=256):
    _, E, D, De = w1.shape
    n_tiles = tile_expert.shape[0]
    grid_spec = pltpu.PrefetchScalarGridSpec(
        num_scalar_prefetch=3,
        grid=(n_tiles, De // tc),
        in_specs=[pl.BlockSpec(memory_space=pl.ANY),
                  pl.BlockSpec((1, 1, D, tc), lambda i, j, te, nu, st: (layer, te[i], 0, j)),
                  pl.BlockSpec((1, 1, D, tc), lambda i, j, te, nu, st: (layer, te[i], 0, j)),
                  pl.BlockSpec((1, 1, tc, D), lambda i, j, te, nu, st: (layer, te[i], j, 0))],
        out_specs=pl.BlockSpec((ts, D), lambda i, j, te, nu, st: (i, 0)),
        scratch_shapes=[pltpu.VMEM((2, ts, D // 2), jnp.uint32), pltpu.VMEM((ts, D), BF16),
                        pltpu.SemaphoreType.DMA((2,))],
    )
    return pl.pallas_call(
        functools.partial(_moe_group_kernel, ts, De // tc), grid_spec=grid_spec,
        out_shape=jax.ShapeDtypeStruct((n_tiles * ts, D), F32),
        compiler_params=_params(("arbitrary", "arbitrary")), name="moe_grouped",
    )(tile_expert, n_used, slot_token, u, w1, w3, w2)


def _moe_combine_kernel(alpha, n_ctx_tiles, first_tile, sa_ref, sb_ref, o_hbm, x_ref, gv_ref, g_ref, gb_ref, out_ref,
                        abuf, bbuf, sem):
    i = pl.program_id(0)
    tm = x_ref.shape[0]

    def gather(tile, b):
        def row(r, carry):
            t = (first_tile + tile) * tm + r
            pltpu.make_async_copy(o_hbm.at[pl.ds(sa_ref[t], 1), :], abuf.at[b, pl.ds(r, 1), :], sem.at[0, b]).start()
            pltpu.make_async_copy(o_hbm.at[pl.ds(sb_ref[t], 1), :], bbuf.at[b, pl.ds(r, 1), :], sem.at[1, b]).start(
                priority=GATHER_DMA_PRIORITY)
            return carry
        lax.fori_loop(0, tm, row, 0)

    @pl.when(i == 0)
    def _():
        gather(0, 0)

    @pl.when(i + 1 < pl.num_programs(0))
    def _():
        gather(i + 1, (i + 1) % 2)

    b = i % 2
    pltpu.make_async_copy(o_hbm.at[pl.ds(0, tm), :], abuf.at[b], sem.at[0, b]).wait()
    pltpu.make_async_copy(o_hbm.at[pl.ds(0, tm), :], bbuf.at[b], sem.at[1, b]).wait()
    gv = gv_ref[...]
    f = gv[:, 0:1] * abuf[b] + gv[:, 1:2] * bbuf[b]
    g = jnp.where(first_tile + i < n_ctx_tiles, g_ref[1:2, :], g_ref[0:1, :])
    y = alpha * x_ref[...] + g * f
    out_ref[...] = _ln_rows(y, LN_EPS) * gb_ref[0:1, :] + gb_ref[1:2, :]


def moe_combine_ln(xx, o, slots, gate_vals, gates2, gain_bias, n_ctx, alpha, latent_only=False, tm=256):
    N, D = xx.shape
    first = n_ctx // tm if latent_only else 0
    grid_spec = pltpu.PrefetchScalarGridSpec(
        num_scalar_prefetch=2,
        grid=(N // tm - first,),
        in_specs=[pl.BlockSpec(memory_space=pl.ANY),
                  pl.BlockSpec((tm, D), lambda i, sa, sb: (first + i, 0)),
                  pl.BlockSpec((tm, LANES), lambda i, sa, sb: (first + i, 0)),
                  pl.BlockSpec((2, D), lambda i, sa, sb: (0, 0)), pl.BlockSpec((2, D), lambda i, sa, sb: (0, 0))],
        out_specs=pl.BlockSpec((tm, D), lambda i, sa, sb: (i, 0)),
        scratch_shapes=[pltpu.VMEM((2, tm, D), F32), pltpu.VMEM((2, tm, D), F32), pltpu.SemaphoreType.DMA((2, 2))],
    )
    return pl.pallas_call(
        functools.partial(_moe_combine_kernel, alpha, n_ctx // tm, first), grid_spec=grid_spec,
        out_shape=jax.ShapeDtypeStruct((N - first * tm, D), F32),
        compiler_params=_params(("arbitrary",)), name="moe_combine_ln",
    )(slots[:, 0], slots[:, 1], o, xx, gate_vals, gates2, gain_bias)


MOE_SLOT_TILE = 512

def _pick_tile(n, prefer):
    for t in prefer:
        if n % t == 0:
            return t
    raise ValueError(f"no tile for {n}")


def kernel(x, c, ctx, c_ctx, w_mod, b_mod, w_in, q_gain, k_gain, rwkv_mu, w0, w_up, a0, a_up, g_up, k_k, k_a, r_k, lnx_gain, lnx_bias, w_out, ln1_gain, ln1_bias, ln2_gain, ln2_bias, router_group_w, router_group_b, router_expert_w, router_expert_b, w1, w3, w2):
    B, T, D = x.shape
    C = ctx.shape[1]
    assert B == 1
    depth = w_mod.shape[0]
    alpha = (2 * depth) ** 0.25
    N = C + T
    aw = D // 2
    kvw = KV_HEADS * HEAD_DIM
    rw = D // 4
    fw = D // 4
    rcols = 3 * rw + 2 * W_RANK + 2 * A_RANK + G_RANK
    o_zr = aw + 2 * kvw
    o_zf = o_zr + rcols
    tm_big = _pick_tile(N, (768, 384, 256))

    xx = jnp.concatenate([ctx[0], x[0]], 0)
    mods = mod_vectors(c, c_ctx, w_mod, b_mod).reshape(depth, 8, 6, D)
    cosf, sinf = rope_tables(C, T)
    w_qkv = w_in[:, :, :o_zr].astype(BF16)
    w_rwkv = w_in[:, :, o_zr:o_zf].astype(BF16)
    w_four = w_in[:, :, o_zf:].astype(BF16)
    n_exp, _, d_exp = w1.shape[1:]
    side_cast = (w1.reshape(depth, n_exp * D, d_exp), w3.reshape(depth, n_exp * D, d_exp),
                 w2.reshape(depth, n_exp * d_exp, D), w_out)

    for l in range(depth):
        last = l == depth - 1
        mv = mods[l]
        vec = lambda i: jnp.stack([mv[0, i], mv[1, i]], 0)
        ss1 = jnp.stack([mv[0, 0], mv[0, 1], mv[1, 0], mv[1, 1]], 0)
        ss2 = jnp.stack([mv[0, 3], mv[0, 4], mv[1, 3], mv[1, 4]], 0)

        u = ln_modulate(xx, ss1, C)
        zq = matmul(u, w_qkv, l, F32, tm_big, _pick_tile(o_zr, (768, 512, 256, 128)))
        zr = matmul(u, w_rwkv, l, F32, tm_big, _pick_tile(rcols, (1152, 384, 128)))
        zf = matmul(u, w_four, l, F32, tm_big, _pick_tile(fw, (512, 256, 128)))
        qh, kh, vh = qkv_prep(zq, cosf, sinf, q_gain[l], k_gain[l], aw)
        attn, (w1_b, w3_b, w2_b, w_out_b) = attention(qh, kh, vh, C, cast=[(a, l) for a in side_cast])
        w1_b = w1_b.reshape(1, n_exp, D, d_exp)
        w3_b = w3_b.reshape(1, n_exp, D, d_exp)
        w2_b = w2_b.reshape(1, n_exp, d_exp, D)
        w_out_b = w_out_b[None]
        r, v, kap, g, lwf, bf, ktf, lwb, bb, ktb = rwkv_features(
            zr, C, rwkv_mu[l], w0[l], w_up[l], a0[l], a_up[l], g_up[l], k_k[l], k_a[l])
        y_f = rwkv_scan(False, C, r, kap, v, lwf, bf, ktf)
        y_b = rwkv_scan(True, C, r, kap, v, lwb, bb, ktb)
        rwkv = rwkv_output(y_f, y_b, r, v, ktf, ktb, g, r_k[l].reshape(-1), lnx_gain[l], lnx_bias[l])
        fl = fnet_latent(zf[C:])
        fc = fnet_direct(zf[:C]) if not last else jnp.zeros((C, fw), F32)
        fn = jnp.concatenate([fc, fl], 0).astype(BF16)
        m = matmul_parts([attn, rwkv, fn], w_out_b, 0, F32, tm_big, _pick_tile(D, (512, 256, 128)))
        xx = resid_ln(xx, m, vec(2), jnp.stack([ln1_gain[l], ln1_bias[l]], 0), C, alpha)

        wr = jnp.concatenate([router_group_w[l], router_expert_w[l],
                              jnp.zeros((D, LANES - N_GROUPS - N_EXPERTS), F32)], 1)
        br = jnp.concatenate([router_group_b[l], router_expert_b[l],
                              jnp.zeros((LANES - N_GROUPS - N_EXPERTS,), F32)], 0).reshape(1, LANES)
        u2, logits = ln_modulate(xx, ss2, C, router=(wr, br))
        expert_ids, gate_vals = moe_gates(logits)
        slot_token, slots, tile_expert, n_used = moe_dispatch(expert_ids[:, :2], MOE_SLOT_TILE)
        o = moe_grouped(u2, slot_token, tile_expert, n_used, w1_b, w3_b, w2_b, 0, MOE_SLOT_TILE)
        xx = moe_combine_ln(xx, o, slots, gate_vals, vec(5), jnp.stack([ln2_gain[l], ln2_bias[l]], 0), C, alpha,
                            latent_only=last)
    return xx[None]
```

```python
import functools
import math

import numpy as np
import jax
import jax.numpy as jnp
from jax import lax
from jax.experimental import pallas as pl
from jax.experimental.pallas import tpu as pltpu

F32 = jnp.float32
BF16 = jnp.bfloat16

GRID_W = 64
HEAD_DIM = 128
KV_HEADS = 4
RWKV_HEAD = 64
W_RANK = 64
A_RANK = 64
G_RANK = 128
FOURIER_GROUPS = 4
N_GROUPS = 4
EXPERTS_PER_GROUP = 4
N_EXPERTS = N_GROUPS * EXPERTS_PER_GROUP
ROPE_THETA = 10000.0
W_DECAY_SCALE = math.exp(-0.5)
GN_EPS = 64e-5
LN_EPS = 1e-6
LANES = 128
CHUNK = 64

V7X_VMEM_LIMIT_MB = 56


def _params(sem, vmem_mb=V7X_VMEM_LIMIT_MB):
    return pltpu.CompilerParams(dimension_semantics=sem, vmem_limit_bytes=vmem_mb * 1024 * 1024)


def _dot(a, b, prec=None):
    return jnp.dot(a, b, preferred_element_type=F32, precision=prec)


def _dot_nt(a, b, prec=None):
    return lax.dot_general(a, b, (((1,), (1,)), ((), ())), preferred_element_type=F32, precision=prec)


def _dot_tn(a, b, prec=None):
    return lax.dot_general(a, b, (((0,), (0,)), ((), ())), preferred_element_type=F32, precision=prec)


def _sigmoid(x):
    return 1.0 / (1.0 + jnp.exp(-x))


def _split2(x):
    hi = x.astype(BF16)
    return hi, (x - hi.astype(F32)).astype(BF16)


def _dot3(a, b):
    a_hi, a_lo = _split2(a)
    b_hi, b_lo = _split2(b)
    return _dot(a_hi, b_hi) + (_dot(a_hi, b_lo) + _dot(a_lo, b_hi))


def _bf16_bits(x):
    return lax.bitcast_convert_type(x.astype(BF16).astype(F32), jnp.uint32)


def _pack_bf16_halves(x):
    h = x.shape[1] // 2
    return (_bf16_bits(x[:, :h]) >> 16) | _bf16_bits(x[:, h:])


def _unpack_bf16_halves(p):
    lo = lax.bitcast_convert_type(p << 16, F32)
    hi = lax.bitcast_convert_type(p & jnp.uint32(0xFFFF0000), F32)
    return lo, hi


def _dot_exact_rhs(x, e):
    hi = x.astype(BF16)
    r1 = x - hi.astype(F32)
    mid = r1.astype(BF16)
    lo = (r1 - mid.astype(F32)).astype(BF16)
    return _dot(hi, e) + (_dot(mid, e) + _dot(lo, e))


def _mod_kernel(a_ref, w_ref, b_ref, o_ref):
    tn = o_ref.shape[2]
    rows = []
    for r in range(2):
        cols = []
        for j in range(tn // LANES):
            prod = w_ref[0, :, j * LANES:(j + 1) * LANES] * a_ref[r]
            cols.append(jnp.sum(prod, axis=0, keepdims=True))
        rows.append(jnp.concatenate(cols, axis=1) + b_ref[0])
    rows.append(jnp.zeros((6, tn), F32))
    o_ref[0] = jnp.concatenate(rows, axis=0)


def mod_vectors(c, c_ctx, w_mod, b_mod):
    L, D, D6 = w_mod.shape
    acts = jnp.stack([jax.nn.silu(c[0]), jax.nn.silu(c_ctx)], 0)
    a_b = jnp.broadcast_to(acts[:, :, None], (2, D, LANES))
    tn = 512
    return pl.pallas_call(
        _mod_kernel,
        grid=(L, D6 // tn),
        in_specs=[pl.BlockSpec((2, D, LANES), lambda l, j: (0, 0, 0)),
                  pl.BlockSpec((1, D, tn), lambda l, j: (l, 0, j)),
                  pl.BlockSpec((1, 1, tn), lambda l, j: (l, 0, j))],
        out_specs=pl.BlockSpec((1, 8, tn), lambda l, j: (l, 0, j)),
        out_shape=jax.ShapeDtypeStruct((L, 8, D6), F32),
        compiler_params=_params(("parallel", "parallel")),
        name="mod_vectors",
    )(a_b, w_mod, b_mod.reshape(L, 1, D6))


def _ln_rows(x, eps):
    mu = jnp.mean(x, axis=-1, keepdims=True)
    xc = x - mu
    var = jnp.mean(xc * xc, axis=-1, keepdims=True)
    return xc * lax.rsqrt(var + eps)


def _ln_mod_kernel(n_ctx_tiles, with_router, x_ref, ss_ref, *rest):
    if with_router:
        wr_ref, br_ref, u_ref, lg_ref = rest
    else:
        (u_ref,) = rest
    is_ctx = pl.program_id(0) < n_ctx_tiles
    sh = jnp.where(is_ctx, ss_ref[2:3, :], ss_ref[0:1, :])
    sc = jnp.where(is_ctx, ss_ref[3:4, :], ss_ref[1:2, :])
    u = _ln_rows(x_ref[...], LN_EPS) * (1.0 + sc) + sh
    if with_router:
        u_ref[...] = _pack_bf16_halves(u)
        lg_ref[...] = _dot3(u, wr_ref[...]) + br_ref[...]
    else:
        u_ref[...] = u.astype(u_ref.dtype)


def ln_modulate(xx, ss, n_ctx, router=None, tm=256):
    N, D = xx.shape
    assert N % tm == 0 and n_ctx % tm == 0
    in_specs = [pl.BlockSpec((tm, D), lambda i: (i, 0)), pl.BlockSpec((4, D), lambda i: (0, 0))]
    if router is None:
        out_specs = [pl.BlockSpec((tm, D), lambda i: (i, 0))]
        out_shape = [jax.ShapeDtypeStruct((N, D), BF16)]
    else:
        out_specs = [pl.BlockSpec((tm, D // 2), lambda i: (i, 0))]
        out_shape = [jax.ShapeDtypeStruct((N, D // 2), jnp.uint32)]
    args = [xx, ss]
    if router is not None:
        in_specs += [pl.BlockSpec((D, LANES), lambda i: (0, 0)), pl.BlockSpec((1, LANES), lambda i: (0, 0))]
        out_specs.append(pl.BlockSpec((tm, LANES), lambda i: (i, 0)))
        out_shape.append(jax.ShapeDtypeStruct((N, LANES), F32))
        args += list(router)
    res = pl.pallas_call(
        functools.partial(_ln_mod_kernel, n_ctx // tm, router is not None),
        grid=(N // tm,), in_specs=in_specs, out_specs=out_specs, out_shape=out_shape,
        compiler_params=_params(("parallel",)), name="ln_modulate",
    )(*args)
    return res if router is not None else res[0]


def _resid_ln_kernel(alpha, n_ctx_tiles, x_ref, m_ref, g_ref, gb_ref, o_ref):
    is_ctx = pl.program_id(0) < n_ctx_tiles
    g = jnp.where(is_ctx, g_ref[1:2, :], g_ref[0:1, :])
    y = alpha * x_ref[...] + g * m_ref[...].astype(F32)
    o_ref[...] = _ln_rows(y, LN_EPS) * gb_ref[0:1, :] + gb_ref[1:2, :]


def resid_ln(xx, m, gates2, gain_bias, n_ctx, alpha, tm=256):
    N, D = xx.shape
    return pl.pallas_call(
        functools.partial(_resid_ln_kernel, alpha, n_ctx // tm),
        grid=(N // tm,),
        in_specs=[pl.BlockSpec((tm, D), lambda i: (i, 0)), pl.BlockSpec((tm, D), lambda i: (i, 0)),
                  pl.BlockSpec((2, D), lambda i: (0, 0)), pl.BlockSpec((2, D), lambda i: (0, 0))],
        out_specs=pl.BlockSpec((tm, D), lambda i: (i, 0)),
        out_shape=jax.ShapeDtypeStruct((N, D), F32),
        compiler_params=_params(("parallel",)), name="resid_ln",
    )(xx, m, gates2, gain_bias)


def _mm_kernel(a_ref, b_ref, o_ref):
    o_ref[...] = _dot(a_ref[...], b_ref[0]).astype(o_ref.dtype)


def matmul(a, b, layer, out_dtype, tm, tn):
    M, K = a.shape
    _, _, Nn = b.shape
    assert M % tm == 0 and Nn % tn == 0
    return pl.pallas_call(
        _mm_kernel, grid=(M // tm, Nn // tn),
        in_specs=[pl.BlockSpec((tm, K), lambda i, j: (i, 0)), pl.BlockSpec((1, K, tn), lambda i, j: (layer, 0, j))],
        out_specs=pl.BlockSpec((tm, tn), lambda i, j: (i, j)),
        out_shape=jax.ShapeDtypeStruct((M, Nn), out_dtype),
        compiler_params=_params(("parallel", "arbitrary")), name="matmul",
    )(a, b)


def _mm_parts_kernel(n_parts, *refs):
    a_refs, b_refs, o_ref = refs[:n_parts], refs[n_parts:2 * n_parts], refs[2 * n_parts]
    acc = _dot(a_refs[0][...], b_refs[0][0])
    for a_ref, b_ref in zip(a_refs[1:], b_refs[1:]):
        acc = acc + _dot(a_ref[...], b_ref[0])
    o_ref[...] = acc.astype(o_ref.dtype)


def matmul_parts(parts, b, layer, out_dtype, tm, tn):
    M = parts[0].shape[0]
    Nn = b.shape[2]
    a_specs, b_specs, off = [], [], 0
    for p in parts:
        kp = p.shape[1]
        assert off % kp == 0
        a_specs.append(pl.BlockSpec((tm, kp), lambda i, j: (i, 0)))
        b_specs.append(pl.BlockSpec((1, kp, tn), lambda i, j, r=off // kp: (layer, r, j)))
        off += kp
    assert off == b.shape[1] and M % tm == 0 and Nn % tn == 0
    return pl.pallas_call(
        functools.partial(_mm_parts_kernel, len(parts)), grid=(M // tm, Nn // tn),
        in_specs=a_specs + b_specs,
        out_specs=pl.BlockSpec((tm, tn), lambda i, j: (i, j)),
        out_shape=jax.ShapeDtypeStruct((M, Nn), out_dtype),
        compiler_params=_params(("parallel", "arbitrary")), name="matmul_parts",
    )(*parts, *([b] * len(parts)))


def _qkv_prep_kernel(n_q_heads, scale, z_ref, cos_ref, sin_ref, qg_ref, kg_ref, q_ref, k_ref, v_ref):
    cosf = cos_ref[...]
    sinf = sin_ref[...]

    def norm_rope(t, gain):
        t = t * lax.rsqrt(jnp.mean(t * t, axis=-1, keepdims=True) + LN_EPS) * gain
        return t * cosf + pltpu.roll(t, HEAD_DIM // 2, 1) * sinf

    for h in range(n_q_heads):
        sl = slice(h * HEAD_DIM, (h + 1) * HEAD_DIM)
        q_ref[:, sl] = (norm_rope(z_ref[:, sl], qg_ref[...]) * scale).astype(q_ref.dtype)
    qw = n_q_heads * HEAD_DIM
    for h in range(KV_HEADS):
        sl = slice(h * HEAD_DIM, (h + 1) * HEAD_DIM)
        zs = slice(qw + h * HEAD_DIM, qw + (h + 1) * HEAD_DIM)
        k_ref[:, sl] = norm_rope(z_ref[:, zs], kg_ref[...]).astype(k_ref.dtype)
    kvw = KV_HEADS * HEAD_DIM
    ones = jnp.ones((z_ref.shape[0], HEAD_DIM), v_ref.dtype)
    for h in range(KV_HEADS):
        zs = slice(qw + kvw + h * HEAD_DIM, qw + kvw + (h + 1) * HEAD_DIM)
        v_ref[:, 2 * h * HEAD_DIM:(2 * h + 1) * HEAD_DIM] = z_ref[:, zs].astype(v_ref.dtype)
        v_ref[:, (2 * h + 1) * HEAD_DIM:(2 * h + 2) * HEAD_DIM] = ones


def qkv_prep(z, cosf, sinf, q_gain, k_gain, attn_width, tm=256):
    N = z.shape[0]
    n_q = attn_width // HEAD_DIM
    kvw = KV_HEADS * HEAD_DIM
    zw = attn_width + 2 * kvw
    return pl.pallas_call(
        functools.partial(_qkv_prep_kernel, n_q, HEAD_DIM ** -0.5 * math.log2(math.e)),
        grid=(N // tm,),
        in_specs=[pl.BlockSpec((tm, zw), lambda i: (i, 0)),
                  pl.BlockSpec((tm, HEAD_DIM), lambda i: (i, 0)), pl.BlockSpec((tm, HEAD_DIM), lambda i: (i, 0)),
                  pl.BlockSpec((1, HEAD_DIM), lambda i: (0, 0)), pl.BlockSpec((1, HEAD_DIM), lambda i: (0, 0))],
        out_specs=[pl.BlockSpec((tm, attn_width), lambda i: (i, 0)),
                   pl.BlockSpec((tm, kvw), lambda i: (i, 0)), pl.BlockSpec((tm, 2 * kvw), lambda i: (i, 0))],
        out_shape=[jax.ShapeDtypeStruct((N, attn_width), BF16),
                   jax.ShapeDtypeStruct((N, kvw), BF16), jax.ShapeDtypeStruct((N, 2 * kvw), BF16)],
        compiler_params=_params(("parallel",)), name="qkv_prep",
    )(z, cosf, sinf, q_gain.reshape(1, HEAD_DIM), k_gain.reshape(1, HEAD_DIM))


def _attn_kernel(n_ctx, n_ctx_tiles, group, n_cast, q_ref, k_ref, v_ref, *rest):
    cast_src, o_ref, cast_dst = rest[:n_cast], rest[n_cast], rest[n_cast + 1:]
    tq = q_ref.shape[0]
    hh = group // 2
    halves = [jnp.concatenate([q_ref[:, h * HEAD_DIM:(h + 1) * HEAD_DIM] for h in range(i * hh, (i + 1) * hh)], axis=0)
              for i in range(2)]

    def attend(k, v):
        s = [_dot_nt(qh, k) for qh in halves]
        p = [jnp.exp2(x - jnp.max(x, axis=-1, keepdims=True)).astype(v.dtype) for x in s]
        acc = [_dot(x, v) for x in p]
        for h in range(group):
            a = acc[h // hh][(h % hh) * tq:(h % hh + 1) * tq]
            o_ref[:, h * HEAD_DIM:(h + 1) * HEAD_DIM] = (a[:, :HEAD_DIM] / a[:, HEAD_DIM:]).astype(o_ref.dtype)

    is_ctx = pl.program_id(1) < n_ctx_tiles

    @pl.when(is_ctx)
    def _():
        attend(k_ref[0:n_ctx, :], v_ref[0:n_ctx, :])

    @pl.when(jnp.logical_not(is_ctx))
    def _():
        attend(k_ref[...], v_ref[...])

    for src, dst in zip(cast_src, cast_dst):
        dst[...] = src[...].astype(dst.dtype)


def attention(q, k, v, n_ctx, cast=(), tq=128):
    N, aw = q.shape
    group = aw // HEAD_DIM // KV_HEADS
    gw = group * HEAD_DIM
    assert N % tq == 0 and n_ctx % tq == 0
    n_q = N // tq
    n_steps = KV_HEADS * n_q
    cast_in, cast_out, cast_shape, cast_args = [], [], [], []
    for a, layer in cast:
        _, rows, cols = a.shape
        rb = -(-(-(-rows // n_steps)) // 16) * 16
        last = -(-rows // rb) - 1
        if rows % rb == 0:
            a2, first = a.reshape(-1, cols), layer * (rows // rb)
        else:
            a2, first = a[layer], 0
        cast_args.append(a2)
        cast_in.append(pl.BlockSpec((rb, cols), lambda g, i, last=last, first=first:
                                    (first + jnp.minimum(g * n_q + i, last), 0)))
        cast_out.append(pl.BlockSpec((rb, cols), lambda g, i, last=last: (jnp.minimum(g * n_q + i, last), 0)))
        cast_shape.append(jax.ShapeDtypeStruct((rows, cols), BF16))
    res = pl.pallas_call(
        functools.partial(_attn_kernel, n_ctx, n_ctx // tq, group, len(cast)),
        grid=(KV_HEADS, n_q),
        in_specs=[pl.BlockSpec((tq, gw), lambda g, i: (i, g)),
                  pl.BlockSpec((N, HEAD_DIM), lambda g, i: (0, g)),
                  pl.BlockSpec((N, 2 * HEAD_DIM), lambda g, i: (0, g))] + cast_in,
        out_specs=[pl.BlockSpec((tq, gw), lambda g, i: (i, g))] + cast_out,
        out_shape=[jax.ShapeDtypeStruct((N, aw), BF16)] + cast_shape,
        compiler_params=_params(("arbitrary", "arbitrary")), name="attention",
    )(q, k, v, *cast_args)
    return res[0], tuple(res[1:])


def rope_tables(n_ctx, n_lat):
    n_rows = n_lat // GRID_W
    row = jnp.repeat(jnp.arange(n_rows), GRID_W).astype(F32)
    col = jnp.tile(jnp.arange(GRID_W), n_rows).astype(F32)
    axis_dim = HEAD_DIM // 2
    inv_freq = ROPE_THETA ** (-jnp.arange(0, axis_dim, 2, dtype=F32) / axis_dim)
    ang = jnp.concatenate([row[:, None] * inv_freq, col[:, None] * inv_freq], -1)
    cos, sin = jnp.cos(ang), jnp.sin(ang)
    cosf = jnp.concatenate([cos, cos], -1)
    sinf = jnp.concatenate([-sin, sin], -1)
    cosf = jnp.concatenate([jnp.ones((n_ctx, HEAD_DIM), F32), cosf], 0)
    sinf = jnp.concatenate([jnp.zeros((n_ctx, HEAD_DIM), F32), sinf], 0)
    return cosf, sinf


def _seg_sum64(x, e128):
    return _dot_exact_rhs(x, e128)


def _rwkv_feat_kernel(n_ctx_tiles, n_tiles, w_cols,
                      z_ref, zp_ref, zn_ref, mu_ref, kk_ref, ka_ref, w0_ref, a0_ref,
                      wup_ref, aup_ref, gup_ref, e_ref,
                      r_ref, v_ref, kap_ref, g_ref, lwf_ref, bf_ref, ktf_ref, lwb_ref, bb_ref, ktb_ref,
                      scr_ref):
    i = pl.program_id(0)
    tm = z_ref.shape[0]
    first = jnp.logical_or(i == 0, i == n_ctx_tiles)
    last = jnp.logical_or(i == n_ctx_tiles - 1, i == n_tiles - 1)
    scr_ref[8:8 + tm, :] = z_ref[...]
    scr_ref[0:8, :] = jnp.where(first, 0.0, zp_ref[...])
    scr_ref[8 + tm:16 + tm, :] = jnp.where(last, 0.0, zn_ref[...])
    z = z_ref[...]
    prev = scr_ref[7:7 + tm, :]
    nxt = scr_ref[9:9 + tm, :]
    zs = z + mu_ref[0:1, :] * (prev - z) + mu_ref[1:2, :] * (nxt - z)

    r = zs[:, 0:w_cols]
    k = zs[:, w_cols:2 * w_cols]
    v = zs[:, 2 * w_cols:3 * w_cols]
    lora = zs[:, 3 * w_cols:]
    wd = jnp.tanh(lora[:, 0:LANES])
    ad = lora[:, LANES:2 * LANES]
    gd = _sigmoid(lora[:, 2 * LANES:3 * LANES])
    r_ref[...] = r.astype(r_ref.dtype)
    v_ref[...] = v.astype(v_ref.dtype)
    g_ref[...] = _dot(gd.astype(BF16), gup_ref[...]).astype(g_ref.dtype)
    kk = k * kk_ref[...]
    e128 = e_ref[...]
    kap = jnp.concatenate(
        [kk[:, s:s + LANES] * lax.rsqrt(jnp.maximum(_seg_sum64(kk[:, s:s + LANES] * kk[:, s:s + LANES], e128), 1e-24))
         for s in range(0, w_cols, LANES)], axis=1)
    kap_ref[...] = kap.astype(kap_ref.dtype)
    outs = ((lwf_ref, bf_ref, ktf_ref), (lwb_ref, bb_ref, ktb_ref))
    for d in range(2):
        lw_ref, b_ref, kt_ref = outs[d]
        lw_ref[...] = -W_DECAY_SCALE * _sigmoid(w0_ref[d:d + 1, :] + _dot(wd.astype(BF16), wup_ref[d]))
        a = _sigmoid(a0_ref[d:d + 1, :] + _dot(ad.astype(BF16), aup_ref[d]))
        kt_ref[...] = (k * (1.0 + (a - 1.0) * ka_ref[...])).astype(kt_ref.dtype)
        b_ref[...] = (a * kap).astype(b_ref.dtype)


def _seg_ones(width=LANES, seg=RWKV_HEAD):
    i = np.arange(width)
    return jnp.asarray((i[:, None] // seg == i[None, :] // seg).astype(np.float32)).astype(BF16)


def rwkv_features(zr, n_ctx, mu, w0, w_up, a0, a_up, g_up, k_k, k_a, tm=256):
    N, zw = zr.shape
    wc = k_k.shape[0]
    assert zw == 3 * wc + 3 * LANES and W_RANK + W_RANK == LANES and A_RANK + A_RANK == LANES and G_RANK == LANES
    zeros = jnp.zeros((W_RANK, wc), F32)
    wup = jnp.stack([jnp.concatenate([w_up[0], zeros], 0), jnp.concatenate([zeros, w_up[1]], 0)], 0)
    aup = jnp.stack([jnp.concatenate([a_up[0], zeros], 0), jnp.concatenate([zeros, a_up[1]], 0)], 0)
    n_tiles = N // tm
    t8 = tm // 8
    row = lambda i: (i, 0)
    full = lambda i: (0, 0)
    sds = lambda dt: jax.ShapeDtypeStruct((N, wc), dt)
    return pl.pallas_call(
        functools.partial(_rwkv_feat_kernel, n_ctx // tm, n_tiles, wc),
        grid=(n_tiles,),
        in_specs=[pl.BlockSpec((tm, zw), row),
                  pl.BlockSpec((8, zw), lambda i: (jnp.maximum(i * t8 - 1, 0), 0)),
                  pl.BlockSpec((8, zw), lambda i: (jnp.minimum((i + 1) * t8, N // 8 - 1), 0)),
                  pl.BlockSpec((2, zw), full), pl.BlockSpec((1, wc), full), pl.BlockSpec((1, wc), full),
                  pl.BlockSpec((2, wc), full), pl.BlockSpec((2, wc), full),
                  pl.BlockSpec((2, LANES, wc), lambda i: (0, 0, 0)), pl.BlockSpec((2, LANES, wc), lambda i: (0, 0, 0)),
                  pl.BlockSpec((LANES, wc), full), pl.BlockSpec((LANES, LANES), full)],
        out_specs=[pl.BlockSpec((tm, wc), row)] * 10,
        out_shape=[sds(BF16)] * 4 + [sds(F32), sds(BF16), sds(BF16)] * 2,
        scratch_shapes=[pltpu.VMEM((tm + 16, zw), F32)],
        compiler_params=_params(("parallel",)), name="rwkv_features",
    )(zr, zr, zr, mu, k_k.reshape(1, wc), k_a.reshape(1, wc), w0, a0,
      wup.astype(BF16), aup.astype(BF16), g_up.astype(BF16), _seg_ones())


def _rwkv_scan_kernel(rev, pairs, n_sub, r_ref, kap_ref, v_ref, lw_ref, b_ref, kt_ref, y_ref, s_ref):
    L = CHUNK
    L2 = 2 * L

    @pl.when(pl.program_id(1) == 0)
    def _():
        s_ref[...] = jnp.zeros(s_ref.shape, F32)

    t_i = lax.broadcasted_iota(jnp.int32, (L, L), 0)
    s_i = lax.broadcasted_iota(jnp.int32, (L, L), 1)
    m_incl64 = ((s_i >= t_i) if rev else (s_i <= t_i)).astype(F32)
    ri = lax.broadcasted_iota(jnp.int32, (L2, L2), 0)
    qi = lax.broadcasted_iota(jnp.int32, (L2, L2), 1)
    rt, qt = ri % L, qi % L
    same = (ri // L) == (qi // L)
    incl = jnp.logical_and(same, (qt >= rt) if rev else (qt <= rt))
    strict = jnp.logical_and(same, (qt > rt) if rev else (qt < rt))
    eye = (ri == qi).astype(F32)
    lane = lax.broadcasted_iota(jnp.int32, (L, LANES), 1)
    hm = [(lane < RWKV_HEAD).astype(F32), (lane >= RWKV_HEAD).astype(F32)]
    bd = ((lax.broadcasted_iota(jnp.int32, (LANES, LANES), 0) // RWKV_HEAD)
          == (lax.broadcasted_iota(jnp.int32, (LANES, LANES), 1) // RWKV_HEAD)).astype(F32)
    lvl_masks = []
    bsz = 1
    while bsz < L:
        grp = (ri // (2 * bsz)) == (qi // (2 * bsz))
        r_odd = (ri // bsz) % 2 == 1
        q_odd = (qi // bsz) % 2 == 1
        off = jnp.logical_and(jnp.logical_not(r_odd), q_odd) if rev else jnp.logical_and(r_odd, jnp.logical_not(q_odd))
        lvl_masks.append(jnp.logical_and(grp, off))
        bsz *= 2

    def stack2(x):
        return jnp.concatenate([x * hm[0], x * hm[1]], axis=0)

    def bmm(a, b):
        return _dot(a.astype(BF16), b.astype(BF16))

    m_incl_bf = m_incl64.astype(BF16)

    units = [(ci, p) for ci in range(n_sub) for p in range(pairs)]

    def blk(ref, unit):
        ci, p = unit
        cc = (n_sub - 1 - ci) if rev else ci
        return ref[cc * L:(cc + 1) * L, p * LANES:(p + 1) * LANES]

    def each(fn, *dicts):
        return {un: fn(*(d[un] for d in dicts)) for un in units}

    def split3(x):
        hi = x.astype(BF16)
        r1 = x - hi.astype(F32)
        mid = r1.astype(BF16)
        lo = (r1 - mid.astype(F32)).astype(BF16)
        return jnp.concatenate([hi, mid, lo], axis=1)

    lw = {un: blk(lw_ref, un) for un in units}
    c3 = each(lambda x: _dot(m_incl_bf, split3(x)), lw)
    cum = each(lambda c: c[:, 0:LANES] + c[:, LANES:2 * LANES] + c[:, 2 * LANES:], c3)
    e_incl = each(jnp.exp, cum)
    e_inv = each(lambda c: jnp.exp(-c), cum)
    p_tot = each(lambda e: e[0:1, :] if rev else e[L - 1:L, :], e_incl)
    xk = {un: stack2(blk(kap_ref, un) * jnp.exp(cum[un] - lw[un])).astype(BF16) for un in units}
    xr = {un: stack2(blk(r_ref, un) * e_incl[un]).astype(BF16) for un in units}
    yb = {un: stack2(blk(b_ref, un) * e_inv[un]).astype(BF16) for un in units}
    yk = {un: stack2(blk(kt_ref, un) * e_inv[un]).astype(BF16) for un in units}
    vs = {un: stack2(blk(v_ref, un)).astype(BF16) for un in units}
    amat = each(lambda a, b, c, d: _dot_nt(jnp.concatenate([a, b], axis=0), jnp.concatenate([c, d], axis=0)),
                xk, xr, yb, yk)
    a_ub = each(lambda a: jnp.where(strict, a[0:L2, 0:L2], 0.0), amat)
    a_uk = each(lambda a: jnp.where(strict, a[0:L2, L2:], 0.0).astype(BF16), amat)
    a_rb = each(lambda a: jnp.where(incl, a[L2:, 0:L2], 0.0).astype(BF16), amat)
    a_rk = each(lambda a: jnp.where(incl, a[L2:, L2:], 0.0).astype(BF16), amat)
    tinv = each(lambda a: eye - jnp.where(lvl_masks[0], a, 0.0), a_ub)
    for lm in lvl_masks[1:]:
        tb = each(lambda t: t.astype(BF16), tinv)
        x1 = each(lambda a, t: _dot(jnp.where(lm, a, 0.0).astype(BF16), t).astype(BF16), a_ub, tb)
        tinv = each(lambda t, tbf, x: t - _dot(tbf, x), tinv, tb, x1)
    tb = each(lambda t: t.astype(BF16), tinv)
    w1 = each(lambda t, x: _dot(t, x).astype(BF16), tb, xk)
    avs = each(lambda a, v: _dot(a, v).astype(BF16), a_uk, vs)
    y_ind = each(_dot, a_rk, vs)
    k2 = each(_dot_tn, vs, yk)
    w2 = each(_dot, tb, avs)

    state = [s_ref[p] for p in range(pairs)]
    for ci in range(n_sub):
        wx = [_dot_nt(jnp.concatenate([w1[(ci, p)], xr[(ci, p)]], axis=0), state[p].astype(BF16))
              for p in range(pairs)]
        ub = [(-(wx[p][0:L2] + w2[(ci, p)])).astype(BF16) for p in range(pairs)]
        upd = [_dot_tn(ub[p], yb[(ci, p)]) + k2[(ci, p)] for p in range(pairs)]
        state = [(state[p] + upd[p] * bd) * p_tot[(ci, p)] for p in range(pairs)]
        cc = (n_sub - 1 - ci) if rev else ci
        for p in range(pairs):
            ystk = wx[p][L2:] + _dot(a_rb[(ci, p)], ub[p]) + y_ind[(ci, p)]
            y_ref[cc * L:(cc + 1) * L, p * LANES:(p + 1) * LANES] = ystk[0:L] + ystk[L:]
    for p in range(pairs):
        s_ref[p] = state[p]


def rwkv_scan(rev, n_ctx, r, kap, v, lw, b, kt, pairs=4, n_sub=4):
    N, wc = r.shape
    rb = n_sub * CHUNK
    bw = pairs * LANES
    assert N % rb == 0 and n_ctx % rb == 0 and wc % bw == 0
    nb, nbc = N // rb, n_ctx // rb
    if rev:
        rmap = lambda h, i: (jnp.where(i < nbc, nbc - 1 - i, nb - 1 - (i - nbc)), h)
    else:
        rmap = lambda h, i: (i, h)
    spec = pl.BlockSpec((rb, bw), rmap)
    return pl.pallas_call(
        functools.partial(_rwkv_scan_kernel, rev, pairs, n_sub),
        grid=(wc // bw, nb),
        in_specs=[spec] * 6, out_specs=spec,
        out_shape=jax.ShapeDtypeStruct((N, wc), F32),
        scratch_shapes=[pltpu.VMEM((pairs, LANES, LANES), F32)],
        compiler_params=_params(("parallel", "arbitrary")), name="rwkv_scan_rev" if rev else "rwkv_scan_fwd",
    )(r, kap, v, lw, b, kt)


def _rwkv_out_kernel(yf_ref, yb_ref, r_ref, v_ref, ktf_ref, ktb_ref, g_ref, rk_ref, gn_ref, e_ref, o_ref):
    e128 = e_ref[...]
    inv = 1.0 / RWKV_HEAD
    for s in range(0, o_ref.shape[1], LANES):
        sl = slice(s, s + LANES)
        y = yf_ref[:, sl] + yb_ref[:, sl]
        mu = _seg_sum64(y, e128) * inv
        yc = y - mu
        var = _seg_sum64(yc * yc, e128) * inv
        yn = yc * lax.rsqrt(var + GN_EPS) * gn_ref[0:1, sl] + gn_ref[1:2, sl]
        rk = r_ref[:, sl] * rk_ref[0:1, sl]
        bonus = _seg_sum64(rk * (ktf_ref[:, sl].astype(F32) + ktb_ref[:, sl]), e128) * v_ref[:, sl]
        o_ref[:, sl] = ((yn + bonus) * g_ref[:, sl]).astype(o_ref.dtype)


def rwkv_output(y_f, y_b, r, v, kt_f, kt_b, g, r_k, lnx_gain, lnx_bias, tm=256):
    N, wc = r.shape
    row = lambda i: (i, 0)
    full = lambda i: (0, 0)
    gn = jnp.stack([lnx_gain, lnx_bias], 0)
    return pl.pallas_call(
        _rwkv_out_kernel, grid=(N // tm,),
        in_specs=[pl.BlockSpec((tm, wc), row)] * 7 + [pl.BlockSpec((1, wc), full), pl.BlockSpec((2, wc), full),
                                                      pl.BlockSpec((LANES, LANES), full)],
        out_specs=pl.BlockSpec((tm, wc), row),
        out_shape=jax.ShapeDtypeStruct((N, wc), BF16),
        compiler_params=_params(("parallel",)), name="rwkv_output",
    )(y_f, y_b, r, v, kt_f, kt_b, g, r_k.reshape(1, wc), gn, _seg_ones())


def _angles(i, j, period):
    return (2.0 * math.pi / period) * ((i * j) % period).astype(np.float64)


def _chan_dft(gdim, scale):
    i = np.arange(gdim)
    ang = _angles(i[:, None], i[None, :], gdim)
    return jnp.asarray(np.concatenate([np.cos(ang), -np.sin(ang)], 1) * scale, F32)


def _fnet1_kernel(groups, x_ref, cs_ref, m_ref, y_ref):
    r1 = x_ref.shape[0]
    gdim = x_ref.shape[1] // groups
    for g in range(groups):
        sl = slice(g * gdim, (g + 1) * gdim)
        ab = _dot3(x_ref[:, sl], cs_ref[...])
        st = jnp.concatenate([ab[:, :gdim], ab[:, gdim:]], axis=0)
        y = _dot3(m_ref[0], st)
        y_ref[0, 0, :, sl] = y[:r1]
        y_ref[1, 0, :, sl] = y[r1:]


def _fnet3_kernel(cs_ref, y_ref, o_ref):
    o_ref[...] = _dot3(cs_ref[...], y_ref[...])


def fnet_latent(zf, groups=FOURIER_GROUPS):
    T, fw = zf.shape
    gdim = fw // groups
    r2 = GRID_W
    r1 = T // r2
    cs = _chan_dft(gdim, 1.0 / math.sqrt(T * gdim))
    t1p = np.arange(r1)[None, :, None]
    t1 = np.arange(r1)[None, None, :]
    t2 = np.arange(r2)[:, None, None]
    th = _angles(t1p, r2 * t1 + t2, T)
    gr, gi = np.cos(th), -np.sin(th)
    mt = jnp.asarray(np.concatenate([np.concatenate([gr, -gi], 2), np.concatenate([gi, gr], 2)], 1), F32)
    y = pl.pallas_call(
        functools.partial(_fnet1_kernel, groups),
        grid=(r2,),
        in_specs=[pl.BlockSpec((r1, fw), lambda j: (0, j)),
                  pl.BlockSpec((gdim, 2 * gdim), lambda j: (0, 0)),
                  pl.BlockSpec((1, 2 * r1, 2 * r1), lambda j: (j, 0, 0))],
        out_specs=pl.BlockSpec((2, 1, r1, fw), lambda j: (0, j, 0, 0)),
        out_shape=jax.ShapeDtypeStruct((2, r2, r1, fw), F32),
        compiler_params=_params(("parallel",)), name="fnet_rows",
    )(zf.reshape(r1, r2 * fw), cs, mt)
    i2 = np.arange(r2)
    ph = _angles(i2[:, None], i2[None, :], r2)
    cs2 = jnp.asarray(np.concatenate([np.cos(ph), np.sin(ph)], 1), F32)
    tc = 4 * fw if r1 % 4 == 0 else fw
    o = pl.pallas_call(
        _fnet3_kernel, grid=(r1 * fw // tc,),
        in_specs=[pl.BlockSpec((r2, 2 * r2), lambda j: (0, 0)), pl.BlockSpec((2 * r2, tc), lambda j: (0, j))],
        out_specs=pl.BlockSpec((r2, tc), lambda j: (0, j)),
        out_shape=jax.ShapeDtypeStruct((r2, r1 * fw), F32),
        compiler_params=_params(("parallel",)), name="fnet_cols",
    )(cs2, y.reshape(2 * r2, r1 * fw))
    return o.reshape(T, fw)


def _fnet_direct_kernel(x_ref, cs_ref, ct_ref, o_ref):
    gdim = x_ref.shape[1]
    ab = _dot3(x_ref[...], cs_ref[...])
    st = jnp.concatenate([ab[:, :gdim], ab[:, gdim:]], axis=0)
    o_ref[...] = _dot3(ct_ref[...], st)


def fnet_direct(zf, groups=FOURIER_GROUPS):
    C, fw = zf.shape
    gdim = fw // groups
    cs = _chan_dft(gdim, 1.0 / math.sqrt(C * gdim))
    i = np.arange(C)
    ang = _angles(i[:, None], i[None, :], C)
    ct = jnp.asarray(np.concatenate([np.cos(ang), np.sin(ang)], 1), F32)
    return pl.pallas_call(
        _fnet_direct_kernel, grid=(groups,),
        in_specs=[pl.BlockSpec((C, gdim), lambda g: (0, g)), pl.BlockSpec((gdim, 2 * gdim), lambda g: (0, 0)),
                  pl.BlockSpec((C, 2 * C), lambda g: (0, 0))],
        out_specs=pl.BlockSpec((C, gdim), lambda g: (0, g)),
        out_shape=jax.ShapeDtypeStruct((C, fw), F32),
        compiler_params=_params(("parallel",)), name="fnet_direct",
    )(zf, cs, ct)


GATHER_DMA_PRIORITY = 1
GATE_LANE0 = N_GROUPS


def _gates_kernel(lg_ref, id_ref, gv_ref):
    lg = lg_ref[...]
    lane = lax.broadcasted_iota(jnp.int32, lg.shape, 1)
    big = jnp.int32(LANES)
    neg = -jnp.inf
    is_g = lane < N_GROUPS
    gl = jnp.where(is_g, lg, neg)
    mg = jnp.max(gl, axis=-1, keepdims=True)
    p_group = 1.0 / jnp.sum(jnp.exp(gl - mg), axis=-1, keepdims=True)
    gsel = jnp.min(jnp.where(gl == mg, lane, big), axis=-1, keepdims=True)
    e_idx = lane - GATE_LANE0
    in_grp = jnp.logical_and(jnp.logical_and(e_idx >= 0, e_idx < N_EXPERTS), e_idx // EXPERTS_PER_GROUP == gsel)
    le = jnp.where(in_grp, lg, neg)
    m1 = jnp.max(le, axis=-1, keepdims=True)
    l1 = jnp.min(jnp.where(le == m1, lane, big), axis=-1, keepdims=True)
    le2 = jnp.where(lane == l1, neg, le)
    m2 = jnp.max(le2, axis=-1, keepdims=True)
    l2 = jnp.min(jnp.where(le2 == m2, lane, big), axis=-1, keepdims=True)
    e2 = jnp.exp(m2 - m1)
    inv = p_group / (1.0 + e2)
    id_ref[...] = jnp.where(lane == 0, l1 - GATE_LANE0, jnp.where(lane == 1, l2 - GATE_LANE0, 0))
    gv_ref[...] = jnp.where(lane == 0, inv, jnp.where(lane == 1, e2 * inv, 0.0))


def moe_gates(logits, tm=256):
    N = logits.shape[0]
    spec = pl.BlockSpec((tm, LANES), lambda i: (i, 0))
    return pl.pallas_call(
        _gates_kernel, grid=(N // tm,),
        in_specs=[spec], out_specs=[spec, spec],
        out_shape=[jax.ShapeDtypeStruct((N, LANES), jnp.int32), jax.ShapeDtypeStruct((N, LANES), F32)],
        compiler_params=_params(("parallel",)), name="moe_gates",
    )(logits)


def moe_dispatch(expert_ids, ts):
    N = expert_ids.shape[0]
    flat = expert_ids.reshape(-1)
    onehot = (flat[:, None] == jnp.arange(N_EXPERTS, dtype=jnp.int32)[None, :]).astype(jnp.int32)
    csum = jnp.cumsum(onehot, axis=0)
    rank = jnp.take_along_axis(csum, flat[:, None], axis=1)[:, 0] - 1
    counts = csum[-1]
    tiles_per = (counts + ts - 1) // ts
    tiles_end = jnp.cumsum(tiles_per)
    pad_start = (tiles_end - tiles_per) * ts
    slot = pad_start[flat] + rank
    n_tiles = -(-2 * N // ts) + N_EXPERTS
    slot_token = jnp.zeros((n_tiles * ts,), jnp.int32).at[slot].set(jnp.arange(2 * N, dtype=jnp.int32) // 2)
    n_used = tiles_end[-1]
    tile_idx = jnp.minimum(jnp.arange(n_tiles, dtype=jnp.int32), n_used - 1)
    tile_expert = jnp.sum((tile_idx[:, None] >= tiles_end[None, :]).astype(jnp.int32), axis=1)
    return slot_token, slot.reshape(N, 2), tile_expert.astype(jnp.int32), n_used.reshape(1).astype(jnp.int32)


def _moe_group_kernel(ts, n_chunks, te_ref, nu_ref, st_ref, u_hbm, w1_ref, w3_ref, w2_ref, o_ref,
                      ubuf, ub16, acc_ref, sem):
    i = pl.program_id(0)
    j = pl.program_id(1)
    used = nu_ref[0]
    share = ts // n_chunks

    def row_copy(slot, b, r):
        return pltpu.make_async_copy(u_hbm.at[pl.ds(st_ref[slot], 1), :], ubuf.at[b, pl.ds(r, 1), :], sem.at[b])

    @pl.when(jnp.logical_and(j == 0, i == 0))
    def _():
        def row(r, carry):
            row_copy(r, 0, r).start(priority=GATHER_DMA_PRIORITY)
            return carry
        lax.fori_loop(0, ts, row, 0)

    @pl.when(jnp.logical_and(j == 0, i <= used))
    def _():
        pltpu.make_async_copy(u_hbm.at[pl.ds(0, ts), :], ubuf.at[i % 2], sem.at[i % 2]).wait()

    @pl.when(jnp.logical_and(j == 0, i < used))
    def _():
        lo, hi = _unpack_bf16_halves(ubuf[i % 2])
        half = lo.shape[1]
        ub16[:, :half] = lo.astype(BF16)
        ub16[:, half:] = hi.astype(BF16)

    @pl.when(j == 0)
    def _():
        acc_ref[...] = jnp.zeros(acc_ref.shape, F32)

    @pl.when(i < used)
    def _():
        nb = (i + 1) % 2
        r0 = j * share
        for r in range(share):
            row_copy((i + 1) * ts + r0 + r, nb, r0 + r).start(priority=GATHER_DMA_PRIORITY)
        u = ub16[...]
        a = _dot(u, w1_ref[0, 0])
        g = _dot(u, w3_ref[0, 0])
        h = ((a * _sigmoid(a)) * g).astype(BF16)
        nw = acc_ref.shape[1] // 4
        for n in range(0, acc_ref.shape[1], nw):
            acc_ref[:, n:n + nw] += _dot(h, w2_ref[0, 0, :, n:n + nw])

    @pl.when(j == n_chunks - 1)
    def _():
        o_ref[...] = _pack_bf16_halves(acc_ref[...])


def moe_grouped(u, slot_token, tile_expert, n_used, w1, w3, w2, layer, ts, tc=256):
    _, E, D, De = w1.shape
    n_tiles = tile_expert.shape[0]
    grid_spec = pltpu.PrefetchScalarGridSpec(
        num_scalar_prefetch=3,
        grid=(n_tiles, De // tc),
        in_specs=[pl.BlockSpec(memory_space=pl.ANY),
                  pl.BlockSpec((1, 1, D, tc), lambda i, j, te, nu, st: (layer, te[i], 0, j)),
                  pl.BlockSpec((1, 1, D, tc), lambda i, j, te, nu, st: (layer, te[i], 0, j)),
                  pl.BlockSpec((1, 1, tc, D), lambda i, j, te, nu, st: (layer, te[i], j, 0))],
        out_specs=pl.BlockSpec((ts, D // 2), lambda i, j, te, nu, st: (i, 0)),
        scratch_shapes=[pltpu.VMEM((2, ts, D // 2), jnp.uint32), pltpu.VMEM((ts, D), BF16),
                        pltpu.VMEM((ts, D), F32), pltpu.SemaphoreType.DMA((2,))],
    )
    return pl.pallas_call(
        functools.partial(_moe_group_kernel, ts, De // tc), grid_spec=grid_spec,
        out_shape=jax.ShapeDtypeStruct((n_tiles * ts, D // 2), jnp.uint32),
        compiler_params=_params(("arbitrary", "arbitrary")), name="moe_grouped",
    )(tile_expert, n_used, slot_token, u, w1, w3, w2)


def _moe_combine_kernel(alpha, n_ctx_tiles, first_tile, sa_ref, sb_ref, o_hbm, x_ref, gv_ref, g_ref, gb_ref, out_ref,
                        abuf, bbuf, sem):
    i = pl.program_id(0)
    tm = x_ref.shape[0]

    def gather(tile, b):
        def row(r, carry):
            t = (first_tile + tile) * tm + r
            pltpu.make_async_copy(o_hbm.at[pl.ds(sa_ref[t], 1), :], abuf.at[b, pl.ds(r, 1), :], sem.at[0, b]).start()
            pltpu.make_async_copy(o_hbm.at[pl.ds(sb_ref[t], 1), :], bbuf.at[b, pl.ds(r, 1), :], sem.at[1, b]).start(
                priority=GATHER_DMA_PRIORITY)
            return carry
        lax.fori_loop(0, tm, row, 0)

    @pl.when(i == 0)
    def _():
        gather(0, 0)

    @pl.when(i + 1 < pl.num_programs(0))
    def _():
        gather(i + 1, (i + 1) % 2)

    b = i % 2
    pltpu.make_async_copy(o_hbm.at[pl.ds(0, tm), :], abuf.at[b], sem.at[0, b]).wait()
    pltpu.make_async_copy(o_hbm.at[pl.ds(0, tm), :], bbuf.at[b], sem.at[1, b]).wait()
    gv = gv_ref[...]
    a_lo, a_hi = _unpack_bf16_halves(abuf[b])
    b_lo, b_hi = _unpack_bf16_halves(bbuf[b])
    g1, g2 = gv[:, 0:1], gv[:, 1:2]
    f = jnp.concatenate([g1 * a_lo + g2 * b_lo, g1 * a_hi + g2 * b_hi], axis=1)
    g = jnp.where(first_tile + i < n_ctx_tiles, g_ref[1:2, :], g_ref[0:1, :])
    y = alpha * x_ref[...] + g * f
    out_ref[...] = _ln_rows(y, LN_EPS) * gb_ref[0:1, :] + gb_ref[1:2, :]


def moe_combine_ln(xx, o, slots, gate_vals, gates2, gain_bias, n_ctx, alpha, latent_only=False, tm=256):
    N, D = xx.shape
    first = n_ctx // tm if latent_only else 0
    grid_spec = pltpu.PrefetchScalarGridSpec(
        num_scalar_prefetch=2,
        grid=(N // tm - first,),
        in_specs=[pl.BlockSpec(memory_space=pl.ANY),
                  pl.BlockSpec((tm, D), lambda i, sa, sb: (first + i, 0)),
                  pl.BlockSpec((tm, LANES), lambda i, sa, sb: (first + i, 0)),
                  pl.BlockSpec((2, D), lambda i, sa, sb: (0, 0)), pl.BlockSpec((2, D), lambda i, sa, sb: (0, 0))],
        out_specs=pl.BlockSpec((tm, D), lambda i, sa, sb: (i, 0)),
        scratch_shapes=[pltpu.VMEM((2, tm, D // 2), jnp.uint32), pltpu.VMEM((2, tm, D // 2), jnp.uint32),
                        pltpu.SemaphoreType.DMA((2, 2))],
    )
    return pl.pallas_call(
        functools.partial(_moe_combine_kernel, alpha, n_ctx // tm, first), grid_spec=grid_spec,
        out_shape=jax.ShapeDtypeStruct((N - first * tm, D), F32),
        compiler_params=_params(("arbitrary",)), name="moe_combine_ln",
    )(slots[:, 0], slots[:, 1], o, xx, gate_vals, gates2, gain_bias)


MOE_SLOT_TILE = 512

def _pick_tile(n, prefer):
    for t in prefer:
        if n % t == 0:
            return t
    raise ValueError(f"no tile for {n}")


def kernel(x, c, ctx, c_ctx, w_mod, b_mod, w_in, q_gain, k_gain, rwkv_mu, w0, w_up, a0, a_up, g_up, k_k, k_a, r_k, lnx_gain, lnx_bias, w_out, ln1_gain, ln1_bias, ln2_gain, ln2_bias, router_group_w, router_group_b, router_expert_w, router_expert_b, w1, w3, w2):
    B, T, D = x.shape
    C = ctx.shape[1]
    assert B == 1
    depth = w_mod.shape[0]
    alpha = (2 * depth) ** 0.25
    N = C + T
    aw = D // 2
    kvw = KV_HEADS * HEAD_DIM
    rw = D // 4
    fw = D // 4
    rcols = 3 * rw + 2 * W_RANK + 2 * A_RANK + G_RANK
    o_zr = aw + 2 * kvw
    o_zf = o_zr + rcols
    tm_big = _pick_tile(N, (768, 384, 256))

    xx = jnp.concatenate([ctx[0], x[0]], 0)
    mods = mod_vectors(c, c_ctx, w_mod, b_mod).reshape(depth, 8, 6, D)
    cosf, sinf = rope_tables(C, T)
    w_qkv = w_in[:, :, :o_zr].astype(BF16)
    w_rwkv = w_in[:, :, o_zr:o_zf].astype(BF16)
    w_four = w_in[:, :, o_zf:].astype(BF16)
    n_exp, _, d_exp = w1.shape[1:]
    side_cast = (w1.reshape(depth, n_exp * D, d_exp), w3.reshape(depth, n_exp * D, d_exp),
                 w2.reshape(depth, n_exp * d_exp, D), w_out)

    for l in range(depth):
        last = l == depth - 1
        mv = mods[l]
        vec = lambda i: jnp.stack([mv[0, i], mv[1, i]], 0)
        ss1 = jnp.stack([mv[0, 0], mv[0, 1], mv[1, 0], mv[1, 1]], 0)
        ss2 = jnp.stack([mv[0, 3], mv[0, 4], mv[1, 3], mv[1, 4]], 0)

        u = ln_modulate(xx, ss1, C)
        zq = matmul(u, w_qkv, l, F32, tm_big, _pick_tile(o_zr, (768, 512, 256, 128)))
        zr = matmul(u, w_rwkv, l, F32, tm_big, _pick_tile(rcols, (1152, 384, 128)))
        zf = matmul(u, w_four, l, F32, tm_big, _pick_tile(fw, (512, 256, 128)))
        qh, kh, vh = qkv_prep(zq, cosf, sinf, q_gain[l], k_gain[l], aw)
        attn, (w1_b, w3_b, w2_b, w_out_b) = attention(qh, kh, vh, C, cast=[(a, l) for a in side_cast])
        w1_b = w1_b.reshape(1, n_exp, D, d_exp)
        w3_b = w3_b.reshape(1, n_exp, D, d_exp)
        w2_b = w2_b.reshape(1, n_exp, d_exp, D)
        w_out_b = w_out_b[None]
        r, v, kap, g, lwf, bf, ktf, lwb, bb, ktb = rwkv_features(
            zr, C, rwkv_mu[l], w0[l], w_up[l], a0[l], a_up[l], g_up[l], k_k[l], k_a[l])
        y_f = rwkv_scan(False, C, r, kap, v, lwf, bf, ktf)
        y_b = rwkv_scan(True, C, r, kap, v, lwb, bb, ktb)
        rwkv = rwkv_output(y_f, y_b, r, v, ktf, ktb, g, r_k[l].reshape(-1), lnx_gain[l], lnx_bias[l])
        fl = fnet_latent(zf[C:])
        fc = fnet_direct(zf[:C]) if not last else jnp.zeros((C, fw), F32)
        fn = jnp.concatenate([fc, fl], 0).astype(BF16)
        m = matmul_parts([attn, rwkv, fn], w_out_b, 0, F32, tm_big, _pick_tile(D, (512, 256, 128)))
        xx = resid_ln(xx, m, vec(2), jnp.stack([ln1_gain[l], ln1_bias[l]], 0), C, alpha)

        wr = jnp.concatenate([router_group_w[l], router_expert_w[l],
                              jnp.zeros((D, LANES - N_GROUPS - N_EXPERTS), F32)], 1)
        br = jnp.concatenate([router_group_b[l], router_expert_b[l],
                              jnp.zeros((LANES - N_GROUPS - N_EXPERTS,), F32)], 0).reshape(1, LANES)
        u2, logits = ln_modulate(xx, ss2, C, router=(wr, br))
        expert_ids, gate_vals = moe_gates(logits)
        slot_token, slots, tile_expert, n_used = moe_dispatch(expert_ids[:, :2], MOE_SLOT_TILE)
        o = moe_grouped(u2, slot_token, tile_expert, n_used, w1_b, w3_b, w2_b, 0, MOE_SLOT_TILE)
        xx = moe_combine_ln(xx, o, slots, gate_vals, vec(5), jnp.stack([ln2_gain[l], ln2_bias[l]], 0), C, alpha,
                            latent_only=last)
    return xx[None]
```

```python
import functools
import math

import numpy as np
import jax
import jax.numpy as jnp
from jax import lax
from jax.experimental import pallas as pl
from jax.experimental.pallas import tpu as pltpu

F32 = jnp.float32
BF16 = jnp.bfloat16

GRID_W = 64
HEAD_DIM = 128
KV_HEADS = 4
RWKV_HEAD = 64
W_RANK = 64
A_RANK = 64
G_RANK = 128
FOURIER_GROUPS = 4
N_GROUPS = 4
EXPERTS_PER_GROUP = 4
N_EXPERTS = N_GROUPS * EXPERTS_PER_GROUP
ROPE_THETA = 10000.0
W_DECAY_SCALE = math.exp(-0.5)
GN_EPS = 64e-5
LN_EPS = 1e-6
LANES = 128
CHUNK = 64

V7X_VMEM_LIMIT_MB = 56


def _params(sem, vmem_mb=V7X_VMEM_LIMIT_MB):
    return pltpu.CompilerParams(dimension_semantics=sem, vmem_limit_bytes=vmem_mb * 1024 * 1024)


def _dot(a, b, prec=None):
    return jnp.dot(a, b, preferred_element_type=F32, precision=prec)


def _dot_nt(a, b, prec=None):
    return lax.dot_general(a, b, (((1,), (1,)), ((), ())), preferred_element_type=F32, precision=prec)


def _dot_tn(a, b, prec=None):
    return lax.dot_general(a, b, (((0,), (0,)), ((), ())), preferred_element_type=F32, precision=prec)


def _sigmoid(x):
    return 1.0 / (1.0 + jnp.exp(-x))


def _split2(x):
    hi = x.astype(BF16)
    return hi, (x - hi.astype(F32)).astype(BF16)


def _dot3(a, b):
    a_hi, a_lo = _split2(a)
    b_hi, b_lo = _split2(b)
    return _dot(a_hi, b_hi) + (_dot(a_hi, b_lo) + _dot(a_lo, b_hi))


def _bf16_bits(x):
    return lax.bitcast_convert_type(x.astype(BF16).astype(F32), jnp.uint32)


def _pack_bf16_halves(x):
    h = x.shape[1] // 2
    return (_bf16_bits(x[:, :h]) >> 16) | _bf16_bits(x[:, h:])


def _unpack_bf16_halves(p):
    lo = lax.bitcast_convert_type(p << 16, F32)
    hi = lax.bitcast_convert_type(p & jnp.uint32(0xFFFF0000), F32)
    return lo, hi


def _dot_exact_rhs(x, e):
    hi = x.astype(BF16)
    r1 = x - hi.astype(F32)
    mid = r1.astype(BF16)
    lo = (r1 - mid.astype(F32)).astype(BF16)
    return _dot(hi, e) + (_dot(mid, e) + _dot(lo, e))


def _mod_kernel(a_ref, w_ref, b_ref, o_ref):
    tn = o_ref.shape[2]
    rows = []
    for r in range(2):
        cols = []
        for j in range(tn // LANES):
            prod = w_ref[0, :, j * LANES:(j + 1) * LANES] * a_ref[r]
            cols.append(jnp.sum(prod, axis=0, keepdims=True))
        rows.append(jnp.concatenate(cols, axis=1) + b_ref[0])
    rows.append(jnp.zeros((6, tn), F32))
    o_ref[0] = jnp.concatenate(rows, axis=0)


def mod_vectors(c, c_ctx, w_mod, b_mod):
    L, D, D6 = w_mod.shape
    acts = jnp.stack([jax.nn.silu(c[0]), jax.nn.silu(c_ctx)], 0)
    a_b = jnp.broadcast_to(acts[:, :, None], (2, D, LANES))
    tn = 512
    return pl.pallas_call(
        _mod_kernel,
        grid=(L, D6 // tn),
        in_specs=[pl.BlockSpec((2, D, LANES), lambda l, j: (0, 0, 0)),
                  pl.BlockSpec((1, D, tn), lambda l, j: (l, 0, j)),
                  pl.BlockSpec((1, 1, tn), lambda l, j: (l, 0, j))],
        out_specs=pl.BlockSpec((1, 8, tn), lambda l, j: (l, 0, j)),
        out_shape=jax.ShapeDtypeStruct((L, 8, D6), F32),
        compiler_params=_params(("parallel", "parallel")),
        name="mod_vectors",
    )(a_b, w_mod, b_mod.reshape(L, 1, D6))


def _ln_rows(x, eps):
    mu = jnp.mean(x, axis=-1, keepdims=True)
    xc = x - mu
    var = jnp.mean(xc * xc, axis=-1, keepdims=True)
    return xc * lax.rsqrt(var + eps)


def _ln_mod_kernel(n_ctx_tiles, with_router, x_ref, ss_ref, *rest):
    if with_router:
        wr_ref, br_ref, u_ref, lg_ref = rest
    else:
        (u_ref,) = rest
    is_ctx = pl.program_id(0) < n_ctx_tiles
    sh = jnp.where(is_ctx, ss_ref[2:3, :], ss_ref[0:1, :])
    sc = jnp.where(is_ctx, ss_ref[3:4, :], ss_ref[1:2, :])
    u = _ln_rows(x_ref[...], LN_EPS) * (1.0 + sc) + sh
    if with_router:
        u_ref[...] = _pack_bf16_halves(u)
        lg_ref[...] = _dot3(u, wr_ref[...]) + br_ref[...]
    else:
        u_ref[...] = u.astype(u_ref.dtype)


def ln_modulate(xx, ss, n_ctx, router=None, tm=256):
    N, D = xx.shape
    assert N % tm == 0 and n_ctx % tm == 0
    in_specs = [pl.BlockSpec((tm, D), lambda i: (i, 0)), pl.BlockSpec((4, D), lambda i: (0, 0))]
    if router is None:
        out_specs = [pl.BlockSpec((tm, D), lambda i: (i, 0))]
        out_shape = [jax.ShapeDtypeStruct((N, D), BF16)]
    else:
        out_specs = [pl.BlockSpec((tm, D // 2), lambda i: (i, 0))]
        out_shape = [jax.ShapeDtypeStruct((N, D // 2), jnp.uint32)]
    args = [xx, ss]
    if router is not None:
        in_specs += [pl.BlockSpec((D, LANES), lambda i: (0, 0)), pl.BlockSpec((1, LANES), lambda i: (0, 0))]
        out_specs.append(pl.BlockSpec((tm, LANES), lambda i: (i, 0)))
        out_shape.append(jax.ShapeDtypeStruct((N, LANES), F32))
        args += list(router)
    res = pl.pallas_call(
        functools.partial(_ln_mod_kernel, n_ctx // tm, router is not None),
        grid=(N // tm,), in_specs=in_specs, out_specs=out_specs, out_shape=out_shape,
        compiler_params=_params(("parallel",)), name="ln_modulate",
    )(*args)
    return res if router is not None else res[0]


def _resid_ln_kernel(alpha, n_ctx_tiles, x_ref, m_ref, g_ref, gb_ref, o_ref):
    is_ctx = pl.program_id(0) < n_ctx_tiles
    g = jnp.where(is_ctx, g_ref[1:2, :], g_ref[0:1, :])
    y = alpha * x_ref[...] + g * m_ref[...].astype(F32)
    o_ref[...] = _ln_rows(y, LN_EPS) * gb_ref[0:1, :] + gb_ref[1:2, :]


def resid_ln(xx, m, gates2, gain_bias, n_ctx, alpha, tm=256):
    N, D = xx.shape
    return pl.pallas_call(
        functools.partial(_resid_ln_kernel, alpha, n_ctx // tm),
        grid=(N // tm,),
        in_specs=[pl.BlockSpec((tm, D), lambda i: (i, 0)), pl.BlockSpec((tm, D), lambda i: (i, 0)),
                  pl.BlockSpec((2, D), lambda i: (0, 0)), pl.BlockSpec((2, D), lambda i: (0, 0))],
        out_specs=pl.BlockSpec((tm, D), lambda i: (i, 0)),
        out_shape=jax.ShapeDtypeStruct((N, D), F32),
        compiler_params=_params(("parallel",)), name="resid_ln",
    )(xx, m, gates2, gain_bias)


def _mm_kernel(a_ref, b_ref, o_ref):
    o_ref[...] = _dot(a_ref[...], b_ref[0]).astype(o_ref.dtype)


def matmul(a, b, layer, out_dtype, tm, tn):
    M, K = a.shape
    _, _, Nn = b.shape
    assert M % tm == 0 and Nn % tn == 0
    return pl.pallas_call(
        _mm_kernel, grid=(M // tm, Nn // tn),
        in_specs=[pl.BlockSpec((tm, K), lambda i, j: (i, 0)), pl.BlockSpec((1, K, tn), lambda i, j: (layer, 0, j))],
        out_specs=pl.BlockSpec((tm, tn), lambda i, j: (i, j)),
        out_shape=jax.ShapeDtypeStruct((M, Nn), out_dtype),
        compiler_params=_params(("parallel", "arbitrary")), name="matmul",
    )(a, b)


def _mm_parts_kernel(n_parts, *refs):
    a_refs, b_refs, o_ref = refs[:n_parts], refs[n_parts:2 * n_parts], refs[2 * n_parts]
    acc = _dot(a_refs[0][...], b_refs[0][0])
    for a_ref, b_ref in zip(a_refs[1:], b_refs[1:]):
        acc = acc + _dot(a_ref[...], b_ref[0])
    o_ref[...] = acc.astype(o_ref.dtype)


def matmul_parts(parts, b, layer, out_dtype, tm, tn):
    M = parts[0].shape[0]
    Nn = b.shape[2]
    a_specs, b_specs, off = [], [], 0
    for p in parts:
        kp = p.shape[1]
        assert off % kp == 0
        a_specs.append(pl.BlockSpec((tm, kp), lambda i, j: (i, 0)))
        b_specs.append(pl.BlockSpec((1, kp, tn), lambda i, j, r=off // kp: (layer, r, j)))
        off += kp
    assert off == b.shape[1] and M % tm == 0 and Nn % tn == 0
    return pl.pallas_call(
        functools.partial(_mm_parts_kernel, len(parts)), grid=(M // tm, Nn // tn),
        in_specs=a_specs + b_specs,
        out_specs=pl.BlockSpec((tm, tn), lambda i, j: (i, j)),
        out_shape=jax.ShapeDtypeStruct((M, Nn), out_dtype),
        compiler_params=_params(("parallel", "arbitrary")), name="matmul_parts",
    )(*parts, *([b] * len(parts)))


def _qkv_prep_kernel(n_q_heads, scale, z_ref, cos_ref, sin_ref, qg_ref, kg_ref, q_ref, k_ref, v_ref):
    cosf = cos_ref[...]
    sinf = sin_ref[...]

    def norm_rope(t, gain):
        t = t * lax.rsqrt(jnp.mean(t * t, axis=-1, keepdims=True) + LN_EPS) * gain
        return t * cosf + pltpu.roll(t, HEAD_DIM // 2, 1) * sinf

    for h in range(n_q_heads):
        sl = slice(h * HEAD_DIM, (h + 1) * HEAD_DIM)
        q_ref[:, sl] = (norm_rope(z_ref[:, sl], qg_ref[...]) * scale).astype(q_ref.dtype)
    qw = n_q_heads * HEAD_DIM
    for h in range(KV_HEADS):
        sl = slice(h * HEAD_DIM, (h + 1) * HEAD_DIM)
        zs = slice(qw + h * HEAD_DIM, qw + (h + 1) * HEAD_DIM)
        k_ref[:, sl] = norm_rope(z_ref[:, zs], kg_ref[...]).astype(k_ref.dtype)
    kvw = KV_HEADS * HEAD_DIM
    ones = jnp.ones((z_ref.shape[0], HEAD_DIM), v_ref.dtype)
    for h in range(KV_HEADS):
        zs = slice(qw + kvw + h * HEAD_DIM, qw + kvw + (h + 1) * HEAD_DIM)
        v_ref[:, 2 * h * HEAD_DIM:(2 * h + 1) * HEAD_DIM] = z_ref[:, zs].astype(v_ref.dtype)
        v_ref[:, (2 * h + 1) * HEAD_DIM:(2 * h + 2) * HEAD_DIM] = ones


def qkv_prep(z, cosf, sinf, q_gain, k_gain, attn_width, tm=256):
    N = z.shape[0]
    n_q = attn_width // HEAD_DIM
    kvw = KV_HEADS * HEAD_DIM
    zw = attn_width + 2 * kvw
    return pl.pallas_call(
        functools.partial(_qkv_prep_kernel, n_q, HEAD_DIM ** -0.5 * math.log2(math.e)),
        grid=(N // tm,),
        in_specs=[pl.BlockSpec((tm, zw), lambda i: (i, 0)),
                  pl.BlockSpec((tm, HEAD_DIM), lambda i: (i, 0)), pl.BlockSpec((tm, HEAD_DIM), lambda i: (i, 0)),
                  pl.BlockSpec((1, HEAD_DIM), lambda i: (0, 0)), pl.BlockSpec((1, HEAD_DIM), lambda i: (0, 0))],
        out_specs=[pl.BlockSpec((tm, attn_width), lambda i: (i, 0)),
                   pl.BlockSpec((tm, kvw), lambda i: (i, 0)), pl.BlockSpec((tm, 2 * kvw), lambda i: (i, 0))],
        out_shape=[jax.ShapeDtypeStruct((N, attn_width), BF16),
                   jax.ShapeDtypeStruct((N, kvw), BF16), jax.ShapeDtypeStruct((N, 2 * kvw), BF16)],
        compiler_params=_params(("parallel",)), name="qkv_prep",
    )(z, cosf, sinf, q_gain.reshape(1, HEAD_DIM), k_gain.reshape(1, HEAD_DIM))


def _attn_kernel(n_ctx, n_ctx_tiles, group, n_cast, q_ref, k_ref, v_ref, *rest):
    cast_src, o_ref, cast_dst = rest[:n_cast], rest[n_cast], rest[n_cast + 1:]
    tq = q_ref.shape[0]
    hh = group // 2
    halves = [jnp.concatenate([q_ref[:, h * HEAD_DIM:(h + 1) * HEAD_DIM] for h in range(i * hh, (i + 1) * hh)], axis=0)
              for i in range(2)]

    def attend(k, v):
        s = [_dot_nt(qh, k) for qh in halves]
        p = [jnp.exp2(x - jnp.max(x, axis=-1, keepdims=True)).astype(v.dtype) for x in s]
        acc = [_dot(x, v) for x in p]
        for h in range(group):
            a = acc[h // hh][(h % hh) * tq:(h % hh + 1) * tq]
            o_ref[:, h * HEAD_DIM:(h + 1) * HEAD_DIM] = (a[:, :HEAD_DIM] / a[:, HEAD_DIM:]).astype(o_ref.dtype)

    is_ctx = pl.program_id(1) < n_ctx_tiles

    @pl.when(is_ctx)
    def _():
        attend(k_ref[0:n_ctx, :], v_ref[0:n_ctx, :])

    @pl.when(jnp.logical_not(is_ctx))
    def _():
        attend(k_ref[...], v_ref[...])

    for src, dst in zip(cast_src, cast_dst):
        dst[...] = src[...].astype(dst.dtype)


def attention(q, k, v, n_ctx, cast=(), tq=128):
    N, aw = q.shape
    group = aw // HEAD_DIM // KV_HEADS
    gw = group * HEAD_DIM
    assert N % tq == 0 and n_ctx % tq == 0
    n_q = N // tq
    n_steps = KV_HEADS * n_q
    cast_in, cast_out, cast_shape, cast_args = [], [], [], []
    for a, layer in cast:
        _, rows, cols = a.shape
        rb = -(-(-(-rows // n_steps)) // 16) * 16
        last = -(-rows // rb) - 1
        if rows % rb == 0:
            a2, first = a.reshape(-1, cols), layer * (rows // rb)
        else:
            a2, first = a[layer], 0
        cast_args.append(a2)
        cast_in.append(pl.BlockSpec((rb, cols), lambda g, i, last=last, first=first:
                                    (first + jnp.minimum(g * n_q + i, last), 0)))
        cast_out.append(pl.BlockSpec((rb, cols), lambda g, i, last=last: (jnp.minimum(g * n_q + i, last), 0)))
        cast_shape.append(jax.ShapeDtypeStruct((rows, cols), BF16))
    res = pl.pallas_call(
        functools.partial(_attn_kernel, n_ctx, n_ctx // tq, group, len(cast)),
        grid=(KV_HEADS, n_q),
        in_specs=[pl.BlockSpec((tq, gw), lambda g, i: (i, g)),
                  pl.BlockSpec((N, HEAD_DIM), lambda g, i: (0, g)),
                  pl.BlockSpec((N, 2 * HEAD_DIM), lambda g, i: (0, g))] + cast_in,
        out_specs=[pl.BlockSpec((tq, gw), lambda g, i: (i, g))] + cast_out,
        out_shape=[jax.ShapeDtypeStruct((N, aw), BF16)] + cast_shape,
        compiler_params=_params(("arbitrary", "arbitrary")), name="attention",
    )(q, k, v, *cast_args)
    return res[0], tuple(res[1:])


def rope_tables(n_ctx, n_lat):
    n_rows = n_lat // GRID_W
    row = jnp.repeat(jnp.arange(n_rows), GRID_W).astype(F32)
    col = jnp.tile(jnp.arange(GRID_W), n_rows).astype(F32)
    axis_dim = HEAD_DIM // 2
    inv_freq = ROPE_THETA ** (-jnp.arange(0, axis_dim, 2, dtype=F32) / axis_dim)
    ang = jnp.concatenate([row[:, None] * inv_freq, col[:, None] * inv_freq], -1)
    cos, sin = jnp.cos(ang), jnp.sin(ang)
    cosf = jnp.concatenate([cos, cos], -1)
    sinf = jnp.concatenate([-sin, sin], -1)
    cosf = jnp.concatenate([jnp.ones((n_ctx, HEAD_DIM), F32), cosf], 0)
    sinf = jnp.concatenate([jnp.zeros((n_ctx, HEAD_DIM), F32), sinf], 0)
    return cosf, sinf


def _seg_sum64(x, e128):
    return _dot_exact_rhs(x, e128)


def _rwkv_feat_kernel(n_ctx_tiles, n_tiles, w_cols,
                      z_ref, zp_ref, zn_ref, mu_ref, kk_ref, ka_ref, w0_ref, a0_ref,
                      wup_ref, aup_ref, gup_ref, e_ref,
                      r_ref, v_ref, kap_ref, g_ref, lwf_ref, bf_ref, ktf_ref, lwb_ref, bb_ref, ktb_ref,
                      scr_ref):
    i = pl.program_id(0)
    tm = z_ref.shape[0]
    first = jnp.logical_or(i == 0, i == n_ctx_tiles)
    last = jnp.logical_or(i == n_ctx_tiles - 1, i == n_tiles - 1)
    scr_ref[8:8 + tm, :] = z_ref[...]
    scr_ref[0:8, :] = jnp.where(first, 0.0, zp_ref[...])
    scr_ref[8 + tm:16 + tm, :] = jnp.where(last, 0.0, zn_ref[...])
    z = z_ref[...]
    prev = scr_ref[7:7 + tm, :]
    nxt = scr_ref[9:9 + tm, :]
    zs = z + mu_ref[0:1, :] * (prev - z) + mu_ref[1:2, :] * (nxt - z)

    r = zs[:, 0:w_cols]
    k = zs[:, w_cols:2 * w_cols]
    v = zs[:, 2 * w_cols:3 * w_cols]
    lora = zs[:, 3 * w_cols:]
    wd = jnp.tanh(lora[:, 0:LANES])
    ad = lora[:, LANES:2 * LANES]
    gd = _sigmoid(lora[:, 2 * LANES:3 * LANES])
    r_ref[...] = r.astype(r_ref.dtype)
    v_ref[...] = v.astype(v_ref.dtype)
    g_ref[...] = _dot(gd.astype(BF16), gup_ref[...]).astype(g_ref.dtype)
    kk = k * kk_ref[...]
    e128 = e_ref[...]
    kap = jnp.concatenate(
        [kk[:, s:s + LANES] * lax.rsqrt(jnp.maximum(_seg_sum64(kk[:, s:s + LANES] * kk[:, s:s + LANES], e128), 1e-24))
         for s in range(0, w_cols, LANES)], axis=1)
    kap_ref[...] = kap.astype(kap_ref.dtype)
    outs = ((lwf_ref, bf_ref, ktf_ref), (lwb_ref, bb_ref, ktb_ref))
    for d in range(2):
        lw_ref, b_ref, kt_ref = outs[d]
        lw_ref[...] = -W_DECAY_SCALE * _sigmoid(w0_ref[d:d + 1, :] + _dot(wd.astype(BF16), wup_ref[d]))
        a = _sigmoid(a0_ref[d:d + 1, :] + _dot(ad.astype(BF16), aup_ref[d]))
        kt_ref[...] = (k * (1.0 + (a - 1.0) * ka_ref[...])).astype(kt_ref.dtype)
        b_ref[...] = (a * kap).astype(b_ref.dtype)


def _seg_ones(width=LANES, seg=RWKV_HEAD):
    i = np.arange(width)
    return jnp.asarray((i[:, None] // seg == i[None, :] // seg).astype(np.float32)).astype(BF16)


def rwkv_features(zr, n_ctx, mu, w0, w_up, a0, a_up, g_up, k_k, k_a, tm=256):
    N, zw = zr.shape
    wc = k_k.shape[0]
    assert zw == 3 * wc + 3 * LANES and W_RANK + W_RANK == LANES and A_RANK + A_RANK == LANES and G_RANK == LANES
    zeros = jnp.zeros((W_RANK, wc), F32)
    wup = jnp.stack([jnp.concatenate([w_up[0], zeros], 0), jnp.concatenate([zeros, w_up[1]], 0)], 0)
    aup = jnp.stack([jnp.concatenate([a_up[0], zeros], 0), jnp.concatenate([zeros, a_up[1]], 0)], 0)
    n_tiles = N // tm
    t8 = tm // 8
    row = lambda i: (i, 0)
    full = lambda i: (0, 0)
    sds = lambda dt: jax.ShapeDtypeStruct((N, wc), dt)
    return pl.pallas_call(
        functools.partial(_rwkv_feat_kernel, n_ctx // tm, n_tiles, wc),
        grid=(n_tiles,),
        in_specs=[pl.BlockSpec((tm, zw), row),
                  pl.BlockSpec((8, zw), lambda i: (jnp.maximum(i * t8 - 1, 0), 0)),
                  pl.BlockSpec((8, zw), lambda i: (jnp.minimum((i + 1) * t8, N // 8 - 1), 0)),
                  pl.BlockSpec((2, zw), full), pl.BlockSpec((1, wc), full), pl.BlockSpec((1, wc), full),
                  pl.BlockSpec((2, wc), full), pl.BlockSpec((2, wc), full),
                  pl.BlockSpec((2, LANES, wc), lambda i: (0, 0, 0)), pl.BlockSpec((2, LANES, wc), lambda i: (0, 0, 0)),
                  pl.BlockSpec((LANES, wc), full), pl.BlockSpec((LANES, LANES), full)],
        out_specs=[pl.BlockSpec((tm, wc), row)] * 10,
        out_shape=[sds(BF16)] * 4 + [sds(F32), sds(BF16), sds(BF16)] * 2,
        scratch_shapes=[pltpu.VMEM((tm + 16, zw), F32)],
        compiler_params=_params(("parallel",)), name="rwkv_features",
    )(zr, zr, zr, mu, k_k.reshape(1, wc), k_a.reshape(1, wc), w0, a0,
      wup.astype(BF16), aup.astype(BF16), g_up.astype(BF16), _seg_ones())


def _rwkv_scan_kernel(rev, pairs, n_sub, r_ref, kap_ref, v_ref, lw_ref, b_ref, kt_ref, y_ref, s_ref):
    L = CHUNK
    L2 = 2 * L

    @pl.when(pl.program_id(1) == 0)
    def _():
        s_ref[...] = jnp.zeros(s_ref.shape, F32)

    t_i = lax.broadcasted_iota(jnp.int32, (L, L), 0)
    s_i = lax.broadcasted_iota(jnp.int32, (L, L), 1)
    m_incl64 = ((s_i >= t_i) if rev else (s_i <= t_i)).astype(F32)
    ri = lax.broadcasted_iota(jnp.int32, (L2, L2), 0)
    qi = lax.broadcasted_iota(jnp.int32, (L2, L2), 1)
    rt, qt = ri % L, qi % L
    same = (ri // L) == (qi // L)
    incl = jnp.logical_and(same, (qt >= rt) if rev else (qt <= rt))
    strict = jnp.logical_and(same, (qt > rt) if rev else (qt < rt))
    eye = (ri == qi).astype(F32)
    lane = lax.broadcasted_iota(jnp.int32, (L, LANES), 1)
    hm = [(lane < RWKV_HEAD).astype(F32), (lane >= RWKV_HEAD).astype(F32)]
    bd = ((lax.broadcasted_iota(jnp.int32, (LANES, LANES), 0) // RWKV_HEAD)
          == (lax.broadcasted_iota(jnp.int32, (LANES, LANES), 1) // RWKV_HEAD)).astype(F32)
    lvl_masks = []
    bsz = 1
    while bsz < L:
        grp = (ri // (2 * bsz)) == (qi // (2 * bsz))
        r_odd = (ri // bsz) % 2 == 1
        q_odd = (qi // bsz) % 2 == 1
        off = jnp.logical_and(jnp.logical_not(r_odd), q_odd) if rev else jnp.logical_and(r_odd, jnp.logical_not(q_odd))
        lvl_masks.append(jnp.logical_and(grp, off))
        bsz *= 2

    def stack2(x):
        return jnp.concatenate([x * hm[0], x * hm[1]], axis=0)

    def bmm(a, b):
        return _dot(a.astype(BF16), b.astype(BF16))

    m_incl_bf = m_incl64.astype(BF16)

    units = [(ci, p) for ci in range(n_sub) for p in range(pairs)]

    def blk(ref, unit):
        ci, p = unit
        cc = (n_sub - 1 - ci) if rev else ci
        return ref[cc * L:(cc + 1) * L, p * LANES:(p + 1) * LANES]

    def each(fn, *dicts):
        return {un: fn(*(d[un] for d in dicts)) for un in units}

    def split3(x):
        hi = x.astype(BF16)
        r1 = x - hi.astype(F32)
        mid = r1.astype(BF16)
        lo = (r1 - mid.astype(F32)).astype(BF16)
        return jnp.concatenate([hi, mid, lo], axis=1)

    lw = {un: blk(lw_ref, un) for un in units}
    c3 = each(lambda x: _dot(m_incl_bf, split3(x)), lw)
    cum = each(lambda c: c[:, 0:LANES] + c[:, LANES:2 * LANES] + c[:, 2 * LANES:], c3)
    e_incl = each(jnp.exp, cum)
    e_inv = each(lambda c: jnp.exp(-c), cum)
    p_tot = each(lambda e: e[0:1, :] if rev else e[L - 1:L, :], e_incl)
    xk = {un: stack2(blk(kap_ref, un) * jnp.exp(cum[un] - lw[un])).astype(BF16) for un in units}
    xr = {un: stack2(blk(r_ref, un) * e_incl[un]).astype(BF16) for un in units}
    yb = {un: stack2(blk(b_ref, un) * e_inv[un]).astype(BF16) for un in units}
    yk = {un: stack2(blk(kt_ref, un) * e_inv[un]).astype(BF16) for un in units}
    vs = {un: stack2(blk(v_ref, un)).astype(BF16) for un in units}
    amat = each(lambda a, b, c, d: _dot_nt(jnp.concatenate([a, b], axis=0), jnp.concatenate([c, d], axis=0)),
                xk, xr, yb, yk)
    a_ub = each(lambda a: jnp.where(strict, a[0:L2, 0:L2], 0.0), amat)
    a_uk = each(lambda a: jnp.where(strict, a[0:L2, L2:], 0.0).astype(BF16), amat)
    a_rb = each(lambda a: jnp.where(incl, a[L2:, 0:L2], 0.0).astype(BF16), amat)
    a_rk = each(lambda a: jnp.where(incl, a[L2:, L2:], 0.0).astype(BF16), amat)
    tinv = each(lambda a: eye - jnp.where(lvl_masks[0], a, 0.0), a_ub)
    for lm in lvl_masks[1:]:
        tb = each(lambda t: t.astype(BF16), tinv)
        x1 = each(lambda a, t: _dot(jnp.where(lm, a, 0.0).astype(BF16), t).astype(BF16), a_ub, tb)
        tinv = each(lambda t, tbf, x: t - _dot(tbf, x), tinv, tb, x1)
    tb = each(lambda t: t.astype(BF16), tinv)
    w1 = each(lambda t, x: _dot(t, x).astype(BF16), tb, xk)
    avs = each(lambda a, v: _dot(a, v).astype(BF16), a_uk, vs)
    y_ind = each(_dot, a_rk, vs)
    k2 = each(_dot_tn, vs, yk)
    w2 = each(_dot, tb, avs)

    state = [s_ref[p] for p in range(pairs)]
    for ci in range(n_sub):
        wx = [_dot_nt(jnp.concatenate([w1[(ci, p)], xr[(ci, p)]], axis=0), state[p].astype(BF16))
              for p in range(pairs)]
        ub = [(-(wx[p][0:L2] + w2[(ci, p)])).astype(BF16) for p in range(pairs)]
        upd = [_dot_tn(ub[p], yb[(ci, p)]) + k2[(ci, p)] for p in range(pairs)]
        state = [(state[p] + upd[p] * bd) * p_tot[(ci, p)] for p in range(pairs)]
        cc = (n_sub - 1 - ci) if rev else ci
        for p in range(pairs):
            ystk = wx[p][L2:] + _dot(a_rb[(ci, p)], ub[p]) + y_ind[(ci, p)]
            y_ref[cc * L:(cc + 1) * L, p * LANES:(p + 1) * LANES] = ystk[0:L] + ystk[L:]
    for p in range(pairs):
        s_ref[p] = state[p]


def rwkv_scan(rev, n_ctx, r, kap, v, lw, b, kt, pairs=8, n_sub=4):
    N, wc = r.shape
    rb = n_sub * CHUNK
    bw = pairs * LANES
    assert N % rb == 0 and n_ctx % rb == 0 and wc % bw == 0
    nb, nbc = N // rb, n_ctx // rb
    if rev:
        rmap = lambda h, i: (jnp.where(i < nbc, nbc - 1 - i, nb - 1 - (i - nbc)), h)
    else:
        rmap = lambda h, i: (i, h)
    spec = pl.BlockSpec((rb, bw), rmap)
    return pl.pallas_call(
        functools.partial(_rwkv_scan_kernel, rev, pairs, n_sub),
        grid=(wc // bw, nb),
        in_specs=[spec] * 6, out_specs=spec,
        out_shape=jax.ShapeDtypeStruct((N, wc), F32),
        scratch_shapes=[pltpu.VMEM((pairs, LANES, LANES), F32)],
        compiler_params=_params(("parallel", "arbitrary")), name="rwkv_scan_rev" if rev else "rwkv_scan_fwd",
    )(r, kap, v, lw, b, kt)


def _rwkv_out_kernel(yf_ref, yb_ref, r_ref, v_ref, ktf_ref, ktb_ref, g_ref, rk_ref, gn_ref, e_ref, o_ref):
    e128 = e_ref[...]
    inv = 1.0 / RWKV_HEAD
    for s in range(0, o_ref.shape[1], LANES):
        sl = slice(s, s + LANES)
        y = yf_ref[:, sl] + yb_ref[:, sl]
        mu = _seg_sum64(y, e128) * inv
        yc = y - mu
        var = _seg_sum64(yc * yc, e128) * inv
        yn = yc * lax.rsqrt(var + GN_EPS) * gn_ref[0:1, sl] + gn_ref[1:2, sl]
        rk = r_ref[:, sl] * rk_ref[0:1, sl]
        bonus = _seg_sum64(rk * (ktf_ref[:, sl].astype(F32) + ktb_ref[:, sl]), e128) * v_ref[:, sl]
        o_ref[:, sl] = ((yn + bonus) * g_ref[:, sl]).astype(o_ref.dtype)


def rwkv_output(y_f, y_b, r, v, kt_f, kt_b, g, r_k, lnx_gain, lnx_bias, tm=256):
    N, wc = r.shape
    row = lambda i: (i, 0)
    full = lambda i: (0, 0)
    gn = jnp.stack([lnx_gain, lnx_bias], 0)
    return pl.pallas_call(
        _rwkv_out_kernel, grid=(N // tm,),
        in_specs=[pl.BlockSpec((tm, wc), row)] * 7 + [pl.BlockSpec((1, wc), full), pl.BlockSpec((2, wc), full),
                                                      pl.BlockSpec((LANES, LANES), full)],
        out_specs=pl.BlockSpec((tm, wc), row),
        out_shape=jax.ShapeDtypeStruct((N, wc), BF16),
        compiler_params=_params(("parallel",)), name="rwkv_output",
    )(y_f, y_b, r, v, kt_f, kt_b, g, r_k.reshape(1, wc), gn, _seg_ones())


def _angles(i, j, period):
    return (2.0 * math.pi / period) * ((i * j) % period).astype(np.float64)


def _chan_dft(gdim, scale):
    i = np.arange(gdim)
    ang = _angles(i[:, None], i[None, :], gdim)
    return jnp.asarray(np.concatenate([np.cos(ang), -np.sin(ang)], 1) * scale, F32)


def _fnet1_kernel(groups, x_ref, cs_ref, m_ref, y_ref):
    r1 = x_ref.shape[0]
    gdim = x_ref.shape[1] // groups
    for g in range(groups):
        sl = slice(g * gdim, (g + 1) * gdim)
        ab = _dot3(x_ref[:, sl], cs_ref[...])
        st = jnp.concatenate([ab[:, :gdim], ab[:, gdim:]], axis=0)
        y = _dot3(m_ref[0], st)
        y_ref[0, 0, :, sl] = y[:r1]
        y_ref[1, 0, :, sl] = y[r1:]


def _fnet3_kernel(cs_ref, y_ref, o_ref):
    o_ref[...] = _dot3(cs_ref[...], y_ref[...])


def fnet_latent(zf, groups=FOURIER_GROUPS):
    T, fw = zf.shape
    gdim = fw // groups
    r2 = GRID_W
    r1 = T // r2
    cs = _chan_dft(gdim, 1.0 / math.sqrt(T * gdim))
    t1p = np.arange(r1)[None, :, None]
    t1 = np.arange(r1)[None, None, :]
    t2 = np.arange(r2)[:, None, None]
    th = _angles(t1p, r2 * t1 + t2, T)
    gr, gi = np.cos(th), -np.sin(th)
    mt = jnp.asarray(np.concatenate([np.concatenate([gr, -gi], 2), np.concatenate([gi, gr], 2)], 1), F32)
    y = pl.pallas_call(
        functools.partial(_fnet1_kernel, groups),
        grid=(r2,),
        in_specs=[pl.BlockSpec((r1, fw), lambda j: (0, j)),
                  pl.BlockSpec((gdim, 2 * gdim), lambda j: (0, 0)),
                  pl.BlockSpec((1, 2 * r1, 2 * r1), lambda j: (j, 0, 0))],
        out_specs=pl.BlockSpec((2, 1, r1, fw), lambda j: (0, j, 0, 0)),
        out_shape=jax.ShapeDtypeStruct((2, r2, r1, fw), F32),
        compiler_params=_params(("parallel",)), name="fnet_rows",
    )(zf.reshape(r1, r2 * fw), cs, mt)
    i2 = np.arange(r2)
    ph = _angles(i2[:, None], i2[None, :], r2)
    cs2 = jnp.asarray(np.concatenate([np.cos(ph), np.sin(ph)], 1), F32)
    tc = 4 * fw if r1 % 4 == 0 else fw
    o = pl.pallas_call(
        _fnet3_kernel, grid=(r1 * fw // tc,),
        in_specs=[pl.BlockSpec((r2, 2 * r2), lambda j: (0, 0)), pl.BlockSpec((2 * r2, tc), lambda j: (0, j))],
        out_specs=pl.BlockSpec((r2, tc), lambda j: (0, j)),
        out_shape=jax.ShapeDtypeStruct((r2, r1 * fw), F32),
        compiler_params=_params(("parallel",)), name="fnet_cols",
    )(cs2, y.reshape(2 * r2, r1 * fw))
    return o.reshape(T, fw)


def _fnet_direct_kernel(x_ref, cs_ref, ct_ref, o_ref):
    gdim = x_ref.shape[1]
    ab = _dot3(x_ref[...], cs_ref[...])
    st = jnp.concatenate([ab[:, :gdim], ab[:, gdim:]], axis=0)
    o_ref[...] = _dot3(ct_ref[...], st)


def fnet_direct(zf, groups=FOURIER_GROUPS):
    C, fw = zf.shape
    gdim = fw // groups
    cs = _chan_dft(gdim, 1.0 / math.sqrt(C * gdim))
    i = np.arange(C)
    ang = _angles(i[:, None], i[None, :], C)
    ct = jnp.asarray(np.concatenate([np.cos(ang), np.sin(ang)], 1), F32)
    return pl.pallas_call(
        _fnet_direct_kernel, grid=(groups,),
        in_specs=[pl.BlockSpec((C, gdim), lambda g: (0, g)), pl.BlockSpec((gdim, 2 * gdim), lambda g: (0, 0)),
                  pl.BlockSpec((C, 2 * C), lambda g: (0, 0))],
        out_specs=pl.BlockSpec((C, gdim), lambda g: (0, g)),
        out_shape=jax.ShapeDtypeStruct((C, fw), F32),
        compiler_params=_params(("parallel",)), name="fnet_direct",
    )(zf, cs, ct)


GATHER_DMA_PRIORITY = 1
GATE_LANE0 = N_GROUPS


def _gates_kernel(lg_ref, id_ref, gv_ref):
    lg = lg_ref[...]
    lane = lax.broadcasted_iota(jnp.int32, lg.shape, 1)
    big = jnp.int32(LANES)
    neg = -jnp.inf
    is_g = lane < N_GROUPS
    gl = jnp.where(is_g, lg, neg)
    mg = jnp.max(gl, axis=-1, keepdims=True)
    p_group = 1.0 / jnp.sum(jnp.exp(gl - mg), axis=-1, keepdims=True)
    gsel = jnp.min(jnp.where(gl == mg, lane, big), axis=-1, keepdims=True)
    e_idx = lane - GATE_LANE0
    in_grp = jnp.logical_and(jnp.logical_and(e_idx >= 0, e_idx < N_EXPERTS), e_idx // EXPERTS_PER_GROUP == gsel)
    le = jnp.where(in_grp, lg, neg)
    m1 = jnp.max(le, axis=-1, keepdims=True)
    l1 = jnp.min(jnp.where(le == m1, lane, big), axis=-1, keepdims=True)
    le2 = jnp.where(lane == l1, neg, le)
    m2 = jnp.max(le2, axis=-1, keepdims=True)
    l2 = jnp.min(jnp.where(le2 == m2, lane, big), axis=-1, keepdims=True)
    e2 = jnp.exp(m2 - m1)
    inv = p_group / (1.0 + e2)
    id_ref[...] = jnp.where(lane == 0, l1 - GATE_LANE0, jnp.where(lane == 1, l2 - GATE_LANE0, 0))
    gv_ref[...] = jnp.where(lane == 0, inv, jnp.where(lane == 1, e2 * inv, 0.0))


def moe_gates(logits, tm=256):
    N = logits.shape[0]
    spec = pl.BlockSpec((tm, LANES), lambda i: (i, 0))
    return pl.pallas_call(
        _gates_kernel, grid=(N // tm,),
        in_specs=[spec], out_specs=[spec, spec],
        out_shape=[jax.ShapeDtypeStruct((N, LANES), jnp.int32), jax.ShapeDtypeStruct((N, LANES), F32)],
        compiler_params=_params(("parallel",)), name="moe_gates",
    )(logits)


def moe_dispatch(expert_ids, ts):
    N = expert_ids.shape[0]
    flat = expert_ids.reshape(-1)
    onehot = (flat[:, None] == jnp.arange(N_EXPERTS, dtype=jnp.int32)[None, :]).astype(jnp.int32)
    csum = jnp.cumsum(onehot, axis=0)
    rank = jnp.take_along_axis(csum, flat[:, None], axis=1)[:, 0] - 1
    counts = csum[-1]
    tiles_per = (counts + ts - 1) // ts
    tiles_end = jnp.cumsum(tiles_per)
    pad_start = (tiles_end - tiles_per) * ts
    slot = pad_start[flat] + rank
    n_tiles = -(-2 * N // ts) + N_EXPERTS
    slot_token = jnp.zeros((n_tiles * ts,), jnp.int32).at[slot].set(jnp.arange(2 * N, dtype=jnp.int32) // 2)
    n_used = tiles_end[-1]
    tile_idx = jnp.minimum(jnp.arange(n_tiles, dtype=jnp.int32), n_used - 1)
    tile_expert = jnp.sum((tile_idx[:, None] >= tiles_end[None, :]).astype(jnp.int32), axis=1)
    return slot_token, slot.reshape(N, 2), tile_expert.astype(jnp.int32), n_used.reshape(1).astype(jnp.int32)


def _moe_group_kernel(ts, n_chunks, te_ref, nu_ref, st_ref, u_hbm, w1_ref, w3_ref, w2_ref, o_ref, ubuf, ub16, sem):
    i = pl.program_id(0)
    j = pl.program_id(1)
    used = nu_ref[0]
    share = ts // n_chunks

    def row_copy(slot, b, r):
        return pltpu.make_async_copy(u_hbm.at[pl.ds(st_ref[slot], 1), :], ubuf.at[b, pl.ds(r, 1), :], sem.at[b])

    @pl.when(jnp.logical_and(j == 0, i == 0))
    def _():
        def row(r, carry):
            row_copy(r, 0, r).start(priority=GATHER_DMA_PRIORITY)
            return carry
        lax.fori_loop(0, ts, row, 0)

    @pl.when(jnp.logical_and(j == 0, i <= used))
    def _():
        pltpu.make_async_copy(u_hbm.at[pl.ds(0, ts), :], ubuf.at[i % 2], sem.at[i % 2]).wait()

    @pl.when(jnp.logical_and(j == 0, i < used))
    def _():
        lo, hi = _unpack_bf16_halves(ubuf[i % 2])
        half = lo.shape[1]
        ub16[:, :half] = lo.astype(BF16)
        ub16[:, half:] = hi.astype(BF16)

    @pl.when(j == 0)
    def _():
        o_ref[...] = jnp.zeros(o_ref.shape, o_ref.dtype)

    @pl.when(i < used)
    def _():
        nb = (i + 1) % 2
        r0 = j * share
        for r in range(share):
            row_copy((i + 1) * ts + r0 + r, nb, r0 + r).start(priority=GATHER_DMA_PRIORITY)
        u = ub16[...]
        a = _dot(u, w1_ref[0, 0])
        g = _dot(u, w3_ref[0, 0])
        h = ((a * _sigmoid(a)) * g).astype(BF16)
        nw = o_ref.shape[1] // 4
        for n in range(0, o_ref.shape[1], nw):
            o_ref[:, n:n + nw] += _dot(h, w2_ref[0, 0, :, n:n + nw])


def moe_grouped(u, slot_token, tile_expert, n_used, w1, w3, w2, layer, ts, tc=256):
    _, E, D, De = w1.shape
    n_tiles = tile_expert.shape[0]
    grid_spec = pltpu.PrefetchScalarGridSpec(
        num_scalar_prefetch=3,
        grid=(n_tiles, De // tc),
        in_specs=[pl.BlockSpec(memory_space=pl.ANY),
                  pl.BlockSpec((1, 1, D, tc), lambda i, j, te, nu, st: (layer, te[i], 0, j)),
                  pl.BlockSpec((1, 1, D, tc), lambda i, j, te, nu, st: (layer, te[i], 0, j)),
                  pl.BlockSpec((1, 1, tc, D), lambda i, j, te, nu, st: (layer, te[i], j, 0))],
        out_specs=pl.BlockSpec((ts, D), lambda i, j, te, nu, st: (i, 0)),
        scratch_shapes=[pltpu.VMEM((2, ts, D // 2), jnp.uint32), pltpu.VMEM((ts, D), BF16),
                        pltpu.SemaphoreType.DMA((2,))],
    )
    return pl.pallas_call(
        functools.partial(_moe_group_kernel, ts, De // tc), grid_spec=grid_spec,
        out_shape=jax.ShapeDtypeStruct((n_tiles * ts, D), F32),
        compiler_params=_params(("arbitrary", "arbitrary")), name="moe_grouped",
    )(tile_expert, n_used, slot_token, u, w1, w3, w2)


def _moe_combine_kernel(alpha, n_ctx_tiles, first_tile, sa_ref, sb_ref, o_hbm, x_ref, gv_ref, g_ref, gb_ref, out_ref,
                        abuf, bbuf, sem):
    i = pl.program_id(0)
    tm = x_ref.shape[0]

    def gather(tile, b):
        def row(r, carry):
            t = (first_tile + tile) * tm + r
            pltpu.make_async_copy(o_hbm.at[pl.ds(sa_ref[t], 1), :], abuf.at[b, pl.ds(r, 1), :], sem.at[0, b]).start()
            pltpu.make_async_copy(o_hbm.at[pl.ds(sb_ref[t], 1), :], bbuf.at[b, pl.ds(r, 1), :], sem.at[1, b]).start(
                priority=GATHER_DMA_PRIORITY)
            return carry
        lax.fori_loop(0, tm, row, 0)

    @pl.when(i == 0)
    def _():
        gather(0, 0)

    @pl.when(i + 1 < pl.num_programs(0))
    def _():
        gather(i + 1, (i + 1) % 2)

    b = i % 2
    pltpu.make_async_copy(o_hbm.at[pl.ds(0, tm), :], abuf.at[b], sem.at[0, b]).wait()
    pltpu.make_async_copy(o_hbm.at[pl.ds(0, tm), :], bbuf.at[b], sem.at[1, b]).wait()
    gv = gv_ref[...]
    f = gv[:, 0:1] * abuf[b] + gv[:, 1:2] * bbuf[b]
    g = jnp.where(first_tile + i < n_ctx_tiles, g_ref[1:2, :], g_ref[0:1, :])
    y = alpha * x_ref[...] + g * f
    out_ref[...] = _ln_rows(y, LN_EPS) * gb_ref[0:1, :] + gb_ref[1:2, :]


def moe_combine_ln(xx, o, slots, gate_vals, gates2, gain_bias, n_ctx, alpha, latent_only=False, tm=256):
    N, D = xx.shape
    first = n_ctx // tm if latent_only else 0
    grid_spec = pltpu.PrefetchScalarGridSpec(
        num_scalar_prefetch=2,
        grid=(N // tm - first,),
        in_specs=[pl.BlockSpec(memory_space=pl.ANY),
                  pl.BlockSpec((tm, D), lambda i, sa, sb: (first + i, 0)),
                  pl.BlockSpec((tm, LANES), lambda i, sa, sb: (first + i, 0)),
                  pl.BlockSpec((2, D), lambda i, sa, sb: (0, 0)), pl.BlockSpec((2, D), lambda i, sa, sb: (0, 0))],
        out_specs=pl.BlockSpec((tm, D), lambda i, sa, sb: (i, 0)),
        scratch_shapes=[pltpu.VMEM((2, tm, D), F32), pltpu.VMEM((2, tm, D), F32), pltpu.SemaphoreType.DMA((2, 2))],
    )
    return pl.pallas_call(
        functools.partial(_moe_combine_kernel, alpha, n_ctx // tm, first), grid_spec=grid_spec,
        out_shape=jax.ShapeDtypeStruct((N - first * tm, D), F32),
        compiler_params=_params(("arbitrary",)), name="moe_combine_ln",
    )(slots[:, 0], slots[:, 1], o, xx, gate_vals, gates2, gain_bias)


MOE_SLOT_TILE = 512

def _pick_tile(n, prefer):
    for t in prefer:
        if n % t == 0:
            return t
    raise ValueError(f"no tile for {n}")


def kernel(x, c, ctx, c_ctx, w_mod, b_mod, w_in, q_gain, k_gain, rwkv_mu, w0, w_up, a0, a_up, g_up, k_k, k_a, r_k, lnx_gain, lnx_bias, w_out, ln1_gain, ln1_bias, ln2_gain, ln2_bias, router_group_w, router_group_b, router_expert_w, router_expert_b, w1, w3, w2):
    B, T, D = x.shape
    C = ctx.shape[1]
    assert B == 1
    depth = w_mod.shape[0]
    alpha = (2 * depth) ** 0.25
    N = C + T
    aw = D // 2
    kvw = KV_HEADS * HEAD_DIM
    rw = D // 4
    fw = D // 4
    rcols = 3 * rw + 2 * W_RANK + 2 * A_RANK + G_RANK
    o_zr = aw + 2 * kvw
    o_zf = o_zr + rcols
    tm_big = _pick_tile(N, (768, 384, 256))

    xx = jnp.concatenate([ctx[0], x[0]], 0)
    mods = mod_vectors(c, c_ctx, w_mod, b_mod).reshape(depth, 8, 6, D)
    cosf, sinf = rope_tables(C, T)
    w_qkv = w_in[:, :, :o_zr].astype(BF16)
    w_rwkv = w_in[:, :, o_zr:o_zf].astype(BF16)
    w_four = w_in[:, :, o_zf:].astype(BF16)
    n_exp, _, d_exp = w1.shape[1:]
    side_cast = (w1.reshape(depth, n_exp * D, d_exp), w3.reshape(depth, n_exp * D, d_exp),
                 w2.reshape(depth, n_exp * d_exp, D), w_out)

    for l in range(depth):
        last = l == depth - 1
        mv = mods[l]
        vec = lambda i: jnp.stack([mv[0, i], mv[1, i]], 0)
        ss1 = jnp.stack([mv[0, 0], mv[0, 1], mv[1, 0], mv[1, 1]], 0)
        ss2 = jnp.stack([mv[0, 3], mv[0, 4], mv[1, 3], mv[1, 4]], 0)

        u = ln_modulate(xx, ss1, C)
        zq = matmul(u, w_qkv, l, F32, tm_big, _pick_tile(o_zr, (768, 512, 256, 128)))
        zr = matmul(u, w_rwkv, l, F32, tm_big, _pick_tile(rcols, (1152, 384, 128)))
        zf = matmul(u, w_four, l, F32, tm_big, _pick_tile(fw, (512, 256, 128)))
        qh, kh, vh = qkv_prep(zq, cosf, sinf, q_gain[l], k_gain[l], aw)
        attn, (w1_b, w3_b, w2_b, w_out_b) = attention(qh, kh, vh, C, cast=[(a, l) for a in side_cast])
        w1_b = w1_b.reshape(1, n_exp, D, d_exp)
        w3_b = w3_b.reshape(1, n_exp, D, d_exp)
        w2_b = w2_b.reshape(1, n_exp, d_exp, D)
        w_out_b = w_out_b[None]
        r, v, kap, g, lwf, bf, ktf, lwb, bb, ktb = rwkv_features(
            zr, C, rwkv_mu[l], w0[l], w_up[l], a0[l], a_up[l], g_up[l], k_k[l], k_a[l])
        y_f = rwkv_scan(False, C, r, kap, v, lwf, bf, ktf)
        y_b = rwkv_scan(True, C, r, kap, v, lwb, bb, ktb)
        rwkv = rwkv_output(y_f, y_b, r, v, ktf, ktb, g, r_k[l].reshape(-1), lnx_gain[l], lnx_bias[l])
        fl = fnet_latent(zf[C:])
        fc = fnet_direct(zf[:C]) if not last else jnp.zeros((C, fw), F32)
        fn = jnp.concatenate([fc, fl], 0).astype(BF16)
        m = matmul_parts([attn, rwkv, fn], w_out_b, 0, F32, tm_big, _pick_tile(D, (512, 256, 128)))
        xx = resid_ln(xx, m, vec(2), jnp.stack([ln1_gain[l], ln1_bias[l]], 0), C, alpha)

        wr = jnp.concatenate([router_group_w[l], router_expert_w[l],
                              jnp.zeros((D, LANES - N_GROUPS - N_EXPERTS), F32)], 1)
        br = jnp.concatenate([router_group_b[l], router_expert_b[l],
                              jnp.zeros((LANES - N_GROUPS - N_EXPERTS,), F32)], 0).reshape(1, LANES)
        u2, logits = ln_modulate(xx, ss2, C, router=(wr, br))
        expert_ids, gate_vals = moe_gates(logits)
        slot_token, slots, tile_expert, n_used = moe_dispatch(expert_ids[:, :2], MOE_SLOT_TILE)
        o = moe_grouped(u2, slot_token, tile_expert, n_used, w1_b, w3_b, w2_b, 0, MOE_SLOT_TILE)
        xx = moe_combine_ln(xx, o, slots, gate_vals, vec(5), jnp.stack([ln2_gain[l], ln2_bias[l]], 0), C, alpha,
                            latent_only=last)
    return xx[None]
```
